```python
import math
import jax, jax.numpy as jnp
from jax import lax
import numpy as np

D_MODEL = 1024
BATCH = 8
SEQ = 2048
DEPTH = 4
DEC_BATCH = 32
DEC_SEQ = 1
PAST_LEN = 16384
PAGE_SIZE = 128

N_MIXERS = 4
MEM_LEN = 256
D_FF = 2816
RMS_EPS = 1e-6
LN_EPS = 1e-5
ROPE_THETA = 10000.0
Q_BLOCK = 128

MLA_HEADS = 16
MLA_Q_LORA = 256
MLA_KV_LORA = 128
MLA_NOPE = 64
MLA_ROPE = 32
MLA_V_HD = 64
MLA_LATENT = MLA_KV_LORA + MLA_ROPE
MLA_IN = MLA_Q_LORA + MLA_KV_LORA + MLA_ROPE
MLA_SCALE = (MLA_NOPE + MLA_ROPE) ** -0.5

CONV_WIDTH = 31
D_CONV = D_MODEL

D_INNER = 2 * D_MODEL
SSM_HD = 64
SSM_HEADS = D_INNER // SSM_HD
SSM_GROUPS = 4
SSM_REP = SSM_HEADS // SSM_GROUPS
SSM_STATE = 128
SSM_CONV = 4
SSM_CHUNK = 128
SSM_CONV_DIM = D_INNER + 2 * SSM_GROUPS * SSM_STATE
SSM_IN = D_INNER + SSM_CONV_DIM + SSM_HEADS

DIL_GROUPS = ((128, 1), (512, 4), (2048, 16))
N_DIL = len(DIL_GROUPS)
DIL_HEADS = 5
DIL_HD = 64
DIL_WIDTH = N_DIL * DIL_HEADS * DIL_HD

MEM_HEADS = 4
MEM_HD = D_MODEL // MEM_HEADS

kernel_name = "hybrid_mla_conv_ssd_dilated_decode_step"


def rmsnorm(x, g):
    xf = x.astype(jnp.float32)
    y = xf * lax.rsqrt(jnp.mean(xf * xf, axis=-1, keepdims=True) + RMS_EPS)
    return (y * g.astype(jnp.float32)).astype(x.dtype)


def layernorm(x, g, b):
    xf = x.astype(jnp.float32)
    mu = jnp.mean(xf, axis=-1, keepdims=True)
    var = jnp.mean(jnp.square(xf - mu), axis=-1, keepdims=True)
    y = (xf - mu) * lax.rsqrt(var + LN_EPS)
    return (y * g.astype(jnp.float32) + b.astype(jnp.float32)).astype(x.dtype)


def rope(x, pos):
    dim = x.shape[-1]
    half = dim // 2
    inv_freq = jnp.exp(-math.log(ROPE_THETA) * (2.0 / dim) * jnp.arange(half, dtype=jnp.float32))
    ang = pos.astype(jnp.float32)[:, None] * inv_freq[None, :]
    cos = jnp.cos(ang)[:, None, :]
    sin = jnp.sin(ang)[:, None, :]
    xf = x.astype(jnp.float32)
    x1, x2 = xf[..., :half], xf[..., half:]
    return jnp.concatenate([x1 * cos - x2 * sin, x2 * cos + x1 * sin], axis=-1).astype(x.dtype)


def swiglu(x, w_gu, w_down):
    g, u = jnp.split(x @ w_gu, 2, axis=-1)
    return (jax.nn.silu(g) * u) @ w_down


def causal_dwconv(u_ext, w, b):
    y = lax.conv_general_dilated(u_ext, w[:, None, :], window_strides=(1,), padding='VALID',
                                 dimension_numbers=('NWC', 'WIO', 'NWC'), feature_group_count=w.shape[1])
    return y + b


def memory_kv(mem, g, w_kv):
    bsz, m, _ = mem.shape
    return (rmsnorm(mem, g) @ w_kv).reshape(bsz, m, 2, MEM_HEADS, MEM_HD)


def cross_attend(h, kv, w_q, w_o):
    bsz, s, _ = h.shape
    q = (h @ w_q).reshape(bsz, s, MEM_HEADS, MEM_HD)
    sc = jnp.einsum('bshe,bmhe->bhsm', q, kv[:, :, 0]).astype(jnp.float32) * MEM_HD ** -0.5
    p = jax.nn.softmax(sc, axis=-1).astype(kv.dtype)
    o = jnp.einsum('bhsm,bmhe->bshe', p, kv[:, :, 1])
    return o.reshape(bsz, s, D_MODEL) @ w_o


def mla_project(h, pos, w_in, norm_q, w_uq, norm_kv, w_uk):
    bsz, s, _ = h.shape
    a = h @ w_in
    cq = a[..., :MLA_Q_LORA]
    ckv = rmsnorm(a[..., MLA_Q_LORA:MLA_Q_LORA + MLA_KV_LORA], norm_kv)
    kpe = rope(a[..., MLA_Q_LORA + MLA_KV_LORA:][:, :, None, :], pos)[:, :, 0]
    q = (rmsnorm(cq, norm_q) @ w_uq).reshape(bsz, s, MLA_HEADS, MLA_NOPE + MLA_ROPE)
    q_pe = rope(q[..., MLA_NOPE:], pos)
    q_lat = jnp.einsum('bshn,chn->bshc', q[..., :MLA_NOPE], w_uk)
    q_abs = jnp.concatenate([q_lat, q_pe], axis=-1)
    latent = jnp.concatenate([ckv, kpe], axis=-1)
    return q_abs, latent


def mla_attend(q_abs, keys, qpos, kpos):
    sc = jnp.einsum('bqhc,bkc->bhqk', q_abs, keys).astype(jnp.float32) * MLA_SCALE
    sc = jnp.where(kpos[None, :] <= qpos[:, None], sc, -jnp.inf)
    p = jax.nn.softmax(sc, axis=-1).astype(keys.dtype)
    return jnp.einsum('bhqk,bkc->bqhc', p, keys[..., :MLA_KV_LORA])


def mla_prompt_attend(q_abs, latent):
    bsz, s = q_abs.shape[:2]
    nb = s // Q_BLOCK
    qb = q_abs.reshape(bsz, nb, Q_BLOCK, MLA_HEADS, MLA_LATENT).transpose(1, 0, 2, 3, 4)
    kpos = jnp.arange(s, dtype=jnp.int32)
    starts = jnp.arange(nb, dtype=jnp.int32) * Q_BLOCK
    o = lax.map(lambda a: mla_attend(a[0], latent, a[1] + jnp.arange(Q_BLOCK, dtype=jnp.int32), kpos), (qb, starts))
    return o.transpose(1, 0, 2, 3, 4).reshape(bsz, s, MLA_HEADS, MLA_KV_LORA)


def mla_output(o_lat, w_uv, w_o):
    bsz, s = o_lat.shape[:2]
    v = jnp.einsum('bshc,chv->bshv', o_lat, w_uv)
    return v.reshape(bsz, s, MLA_HEADS * MLA_V_HD) @ w_o


def glu(h, w_in, b_in):
    a, g = jnp.split(h @ w_in + b_in, 2, axis=-1)
    return a * jax.nn.sigmoid(g)


def conv_module(u_ext, w_dw, b_dw, ln_g, ln_b, w_out, b_out):
    y = causal_dwconv(u_ext, w_dw, b_dw)
    y = jax.nn.silu(layernorm(y, ln_g, ln_b))
    return y @ w_out + b_out


def ssm_split_in(h, w_in):
    a = h @ w_in
    return a[..., :D_INNER], a[..., D_INNER:D_INNER + SSM_CONV_DIM], a[..., D_INNER + SSM_CONV_DIM:]


def ssm_prepare(xbc_act, dt_raw, dt_bias, a_log):
    bsz, s = xbc_act.shape[:2]
    gn = SSM_GROUPS * SSM_STATE
    xf = xbc_act.astype(jnp.float32)
    xh = xf[..., :D_INNER].reshape(bsz, s, SSM_HEADS, SSM_HD)
    bm = xf[..., D_INNER:D_INNER + gn].reshape(bsz, s, SSM_GROUPS, SSM_STATE)
    cm = xf[..., D_INNER + gn:].reshape(bsz, s, SSM_GROUPS, SSM_STATE)
    dt = jax.nn.softplus(dt_raw.astype(jnp.float32) + dt_bias.astype(jnp.float32))
    a = -jnp.exp(a_log.astype(jnp.float32))
    return xh, bm, cm, dt, a


def ssd_chunked(xh, dt, a, bm, cm):
    bsz, s = xh.shape[:2]
    nc, L = s // SSM_CHUNK, SSM_CHUNK
    X = (xh * dt[..., None]).reshape(bsz, nc, L, SSM_GROUPS, SSM_REP, SSM_HD)
    dA = (dt * a).reshape(bsz, nc, L, SSM_GROUPS, SSM_REP)
    Bc = bm.reshape(bsz, nc, L, SSM_GROUPS, SSM_STATE)
    Cc = cm.reshape(bsz, nc, L, SSM_GROUPS, SSM_STATE)
    cs = jnp.cumsum(dA, axis=2)
    seg = cs[:, :, :, None] - cs[:, :, None, :]
    tril = jnp.arange(L)[:, None] >= jnp.arange(L)[None, :]
    Lmat = jnp.exp(jnp.where(tril[None, None, :, :, None, None], seg, -jnp.inf))
    CB = jnp.einsum('bclgn,bcsgn->bclsg', Cc, Bc)
    y_diag = jnp.einsum('bclsg,bclsgr,bcsgrp->bclgrp', CB, Lmat, X)
    decay_states = jnp.exp(cs[:, :, -1:] - cs)
    chunk_states = jnp.einsum('bclgn,bclgr,bclgrp->bcgrpn', Bc, decay_states, X)
    chunk_decay = jnp.exp(cs[:, :, -1])

    def step(hc, inp):
        st, dec = inp
        return dec[..., None, None] * hc + st, hc

    h0 = jnp.zeros((bsz, SSM_GROUPS, SSM_REP, SSM_HD, SSM_STATE), jnp.float32)
    h_final, h_enter = lax.scan(step, h0, (jnp.moveaxis(chunk_states, 1, 0), jnp.moveaxis(chunk_decay, 1, 0)))
    h_enter = jnp.moveaxis(h_enter, 0, 1)
    y_off = jnp.einsum('bclgn,bcgrpn,bclgr->bclgrp', Cc, h_enter, jnp.exp(cs))
    y = (y_diag + y_off).reshape(bsz, s, SSM_HEADS, SSM_HD)
    return y, h_final.reshape(bsz, SSM_HEADS, SSM_HD, SSM_STATE)


def ssd_recurrent(xh, dt, a, bm, cm, h_init):
    bsz = xh.shape[0]
    ag = a.reshape(SSM_GROUPS, SSM_REP)

    def step(hc, inp):
        x_t, dt_t, b_t, c_t = inp
        dtg = dt_t.reshape(bsz, SSM_GROUPS, SSM_REP)
        xg = x_t.reshape(bsz, SSM_GROUPS, SSM_REP, SSM_HD)
        hc = jnp.exp(dtg * ag)[..., None, None] * hc + jnp.einsum('bgrp,bgn->bgrpn', xg * dtg[..., None], b_t)
        y = jnp.einsum('bgrpn,bgn->bgrp', hc, c_t).reshape(bsz, SSM_HEADS, SSM_HD)
        return hc, y

    h0 = h_init.astype(jnp.float32).reshape(bsz, SSM_GROUPS, SSM_REP, SSM_HD, SSM_STATE)
    h_final, ys = lax.scan(step, h0, (jnp.moveaxis(xh, 1, 0), jnp.moveaxis(dt, 1, 0),
                                      jnp.moveaxis(bm, 1, 0), jnp.moveaxis(cm, 1, 0)))
    return jnp.moveaxis(ys, 0, 1), h_final.reshape(bsz, SSM_HEADS, SSM_HD, SSM_STATE)


def ssm_output(y, xh, z, d, norm_g, w_out):
    bsz, s = y.shape[:2]
    y = (y + xh * d.astype(jnp.float32)[:, None]).reshape(bsz, s, D_INNER)
    yg = (y * jax.nn.silu(z.astype(jnp.float32))).reshape(bsz, s, SSM_GROUPS, D_INNER // SSM_GROUPS)
    yg = yg * lax.rsqrt(jnp.mean(yg * yg, axis=-1, keepdims=True) + RMS_EPS)
    return (yg.reshape(bsz, s, D_INNER) * norm_g.astype(jnp.float32)).astype(z.dtype) @ w_out


def dil_project(h, pos, w_qkv):
    bsz, s, _ = h.shape
    qkv = (h @ w_qkv).reshape(bsz, s, 3, N_DIL * DIL_HEADS, DIL_HD)
    q = rope(qkv[:, :, 0], pos).reshape(bsz, s, N_DIL, DIL_HEADS, DIL_HD)
    k = rope(qkv[:, :, 1], pos).reshape(bsz, s, N_DIL, DIL_HEADS, DIL_HD)
    v = qkv[:, :, 2].reshape(bsz, s, N_DIL, DIL_HEADS, DIL_HD)
    return q, k, v


def dilated_prompt_group(q, k, v, window, dil):
    bsz, s, nh, hd = q.shape
    nk = window // dil
    sub_len = s // dil
    nb = -(-sub_len // nk)
    lp = nb * nk

    def residues(a):
        a = a.reshape(bsz, sub_len, dil, nh, hd).transpose(0, 2, 1, 3, 4)
        return jnp.pad(a, ((0, 0), (0, 0), (0, lp - sub_len), (0, 0), (0, 0)))

    def band(a):
        a = jnp.pad(a, ((0, 0), (0, 0), (nk, 0), (0, 0), (0, 0)))
        prev = a[:, :, :lp].reshape(bsz, dil, nb, nk, nh, hd)
        cur = a[:, :, nk:].reshape(bsz, dil, nb, nk, nh, hd)
        return jnp.concatenate([prev, cur], axis=3)

    qb = residues(q).reshape(bsz, dil, nb, nk, nh, hd)
    kb = band(residues(k))
    vb = band(residues(v))
    qi = jnp.arange(nk)[:, None]
    ki = jnp.arange(2 * nk)[None, :]
    dist = qi + nk - ki
    key_sub = (jnp.arange(nb) * nk)[:, None, None] - nk + ki[None]
    mask = (dist >= 0) & (dist <= nk) & (key_sub >= 0)
    sc = jnp.einsum('brnqhe,brnkhe->brnhqk', qb, kb).astype(jnp.float32) * hd ** -0.5
    sc = jnp.where(mask[None, None, :, None], sc, -jnp.inf)
    lse = jax.nn.logsumexp(sc, axis=-1)
    p = jnp.exp(sc - lse[..., None]).astype(v.dtype)
    o = jnp.einsum('brnhqk,brnkhe->brnqhe', p, vb)
    o = o.reshape(bsz, dil, lp, nh, hd)[:, :, :sub_len].transpose(0, 2, 1, 3, 4).reshape(bsz, s, nh, hd)
    lse = lse.transpose(0, 1, 2, 4, 3).reshape(bsz, dil, lp, nh)[:, :, :sub_len]
    lse = lse.transpose(0, 2, 1, 3).reshape(bsz, s, nh)
    return o, lse


def dilated_sample_group(q, k, v, buf, window, dil):
    buf_len, t = buf.shape[1], q.shape[1]
    kv_all = jnp.concatenate([buf, jnp.stack([k, v], axis=2)], axis=1)
    nk = window // dil
    idx = buf_len + jnp.arange(t)[:, None] - dil * jnp.arange(nk + 1)[None, :]
    valid = idx >= 0
    g = kv_all[:, jnp.maximum(idx, 0)]
    sc = jnp.einsum('bthe,btkhe->bhtk', q, g[:, :, :, 0]).astype(jnp.float32) * q.shape[-1] ** -0.5
    sc = jnp.where(valid[None, None], sc, -jnp.inf)
    lse = jax.nn.logsumexp(sc, axis=-1)
    p = jnp.exp(sc - lse[..., None]).astype(v.dtype)
    o = jnp.einsum('bhtk,btkhe->bthe', p, g[:, :, :, 1])
    return o, lse.transpose(0, 2, 1), kv_all[:, t:]


def dilated_combine(outs, lses, w_o):
    alpha = jax.nn.softmax(jnp.stack(lses, axis=2), axis=2)
    o = jnp.stack(outs, axis=2) * alpha[..., None].astype(outs[0].dtype)
    bsz, s = o.shape[:2]
    return o.reshape(bsz, s, DIL_WIDTH) @ w_o


def setup_inputs(seed: int = 0) -> dict:
    key = jax.random.key(seed)
    ks = iter(jax.random.split(key, 96))
    f32 = jnp.float32

    def nrm(shape, scale=1.0):
        return jax.random.normal(next(ks), shape, f32) * scale

    def gain(shape):
        return 1.0 + nrm(shape, 0.05)

    n_pages = PAST_LEN // PAGE_SIZE
    n_used = DEC_BATCH * n_pages
    n_pool = n_used + n_used // 4
    inp = {}
    inp["x_prompt"] = nrm((BATCH, SEQ, D_MODEL))
    inp["x_sample"] = nrm((DEC_BATCH, DEC_SEQ, D_MODEL))
    inp["cache_mla_latent"] = nrm((n_pool, PAGE_SIZE, MLA_LATENT))
    inp["cache_mem_kv"] = nrm((DEPTH, DEC_BATCH, MEM_LEN, 2, MEM_HEADS, MEM_HD))
    inp["state_conv"] = nrm((DEC_BATCH, CONV_WIDTH - 1, D_CONV))
    inp["state_ssm_conv"] = nrm((DEC_BATCH, SSM_CONV - 1, SSM_CONV_DIM))
    inp["state_ssm"] = nrm((DEC_BATCH, SSM_HEADS, SSM_HD, SSM_STATE), 0.5)
    for g, (win, _) in enumerate(DIL_GROUPS):
        inp["state_dil_kv%d" % g] = nrm((DEC_BATCH, min(win, PAST_LEN), 2, DIL_HEADS, DIL_HD))
    inp["page_table"] = jax.random.permutation(next(ks), n_pool)[:n_used].reshape(DEC_BATCH, n_pages).astype(jnp.int32)
    inp["mem_prompt"] = nrm((BATCH, MEM_LEN, D_MODEL))
    inp["norm_ffn1"] = gain((DEPTH, D_MODEL))
    inp["ffn1_w_gu"] = nrm((DEPTH, D_MODEL, 2 * D_FF), D_MODEL ** -0.5)
    inp["ffn1_w_down"] = nrm((DEPTH, D_FF, D_MODEL), D_FF ** -0.5)
    inp["norm_mix"] = gain((DEPTH, D_MODEL))
    inp["norm_cross"] = gain((DEPTH, D_MODEL))
    inp["norm_mem"] = gain((DEPTH, D_MODEL))
    inp["cross_w_q"] = nrm((DEPTH, D_MODEL, D_MODEL), D_MODEL ** -0.5)
    inp["cross_w_kv"] = nrm((DEPTH, D_MODEL, 2 * D_MODEL), D_MODEL ** -0.5)
    inp["cross_w_o"] = nrm((DEPTH, D_MODEL, D_MODEL), D_MODEL ** -0.5)
    inp["norm_ffn2"] = gain((DEPTH, D_MODEL))
    inp["ffn2_w_gu"] = nrm((DEPTH, D_MODEL, 2 * D_FF), D_MODEL ** -0.5)
    inp["ffn2_w_down"] = nrm((DEPTH, D_FF, D_MODEL), D_FF ** -0.5)
    inp["norm_final"] = gain((D_MODEL,))
    inp["mla_w_in"] = nrm((D_MODEL, MLA_IN), D_MODEL ** -0.5)
    inp["mla_norm_q"] = gain((MLA_Q_LORA,))
    inp["mla_w_uq"] = nrm((MLA_Q_LORA, MLA_HEADS * (MLA_NOPE + MLA_ROPE)), MLA_Q_LORA ** -0.5)
    inp["mla_norm_kv"] = gain((MLA_KV_LORA,))
    inp["mla_w_uk"] = nrm((MLA_KV_LORA, MLA_HEADS, MLA_NOPE), MLA_KV_LORA ** -0.5)
    inp["mla_w_uv"] = nrm((MLA_KV_LORA, MLA_HEADS, MLA_V_HD), MLA_KV_LORA ** -0.5)
    inp["mla_w_o"] = nrm((MLA_HEADS * MLA_V_HD, D_MODEL), (MLA_HEADS * MLA_V_HD) ** -0.5)
    inp["conv_w_in"] = nrm((D_MODEL, 2 * D_CONV), D_MODEL ** -0.5)
    inp["conv_b_in"] = nrm((2 * D_CONV,), 0.02)
    inp["conv_w_dw"] = nrm((CONV_WIDTH, D_CONV), CONV_WIDTH ** -0.5)
    inp["conv_b_dw"] = nrm((D_CONV,), 0.02)
    inp["conv_ln_g"] = gain((D_CONV,))
    inp["conv_ln_b"] = nrm((D_CONV,), 0.02)
    inp["conv_w_out"] = nrm((D_CONV, D_MODEL), D_CONV ** -0.5)
    inp["conv_b_out"] = nrm((D_MODEL,), 0.02)
    inp["ssm_w_in"] = nrm((D_MODEL, SSM_IN), D_MODEL ** -0.5)
    inp["ssm_conv_w"] = nrm((SSM_CONV, SSM_CONV_DIM), SSM_CONV ** -0.5)
    inp["ssm_conv_b"] = nrm((SSM_CONV_DIM,), 0.02)
    dt0 = jnp.exp(jax.random.uniform(next(ks), (SSM_HEADS,), f32, math.log(1e-3), math.log(1e-1)))
    inp["ssm_dt_bias"] = dt0 + jnp.log(-jnp.expm1(-dt0))
    inp["ssm_a_log"] = jnp.log(jax.random.uniform(next(ks), (SSM_HEADS,), f32, 1.0, 16.0))
    inp["ssm_d"] = gain((SSM_HEADS,))
    inp["ssm_norm"] = gain((D_INNER,))
    inp["ssm_w_out"] = nrm((D_INNER, D_MODEL), D_INNER ** -0.5)
    inp["dil_w_qkv"] = nrm((D_MODEL, 3 * DIL_WIDTH), D_MODEL ** -0.5)
    inp["dil_w_o"] = nrm((DIL_WIDTH, D_MODEL), DIL_WIDTH ** -0.5)
    return inp


def reference(x_prompt, x_sample, cache_mla_latent, cache_mem_kv, state_conv, state_ssm_conv, state_ssm,
              state_dil_kv0, state_dil_kv1, state_dil_kv2, page_table, mem_prompt,
              norm_ffn1, ffn1_w_gu, ffn1_w_down, norm_mix, norm_cross, norm_mem,
              cross_w_q, cross_w_kv, cross_w_o, norm_ffn2, ffn2_w_gu, ffn2_w_down, norm_final,
              mla_w_in, mla_norm_q, mla_w_uq, mla_norm_kv, mla_w_uk, mla_w_uv, mla_w_o,
              conv_w_in, conv_b_in, conv_w_dw, conv_b_dw, conv_ln_g, conv_ln_b, conv_w_out, conv_b_out,
              ssm_w_in, ssm_conv_w, ssm_conv_b, ssm_dt_bias, ssm_a_log, ssm_d, ssm_norm, ssm_w_out,
              dil_w_qkv, dil_w_o):
    seq = x_prompt.shape[1]
    dec_batch, dec_seq = x_sample.shape[:2]
    past_len = page_table.shape[1] * cache_mla_latent.shape[1]
    pos_p = jnp.arange(seq, dtype=jnp.int32)
    pos_s = past_len + jnp.arange(dec_seq, dtype=jnp.int32)
    dil_bufs_in = (state_dil_kv0, state_dil_kv1, state_dil_kv2)
    res_p, res_s = x_prompt, x_sample
    mem_kv_layers = []
    for i in range(DEPTH):
        kind = i % N_MIXERS
        res_p = res_p + 0.5 * swiglu(rmsnorm(res_p, norm_ffn1[i]), ffn1_w_gu[i], ffn1_w_down[i])
        res_s = res_s + 0.5 * swiglu(rmsnorm(res_s, norm_ffn1[i]), ffn1_w_gu[i], ffn1_w_down[i])
        hp = rmsnorm(res_p, norm_mix[i])
        hs = rmsnorm(res_s, norm_mix[i])
        if kind == 0:
            q_abs_p, lat_p = mla_project(hp, pos_p, mla_w_in, mla_norm_q, mla_w_uq, mla_norm_kv, mla_w_uk)
            mp = mla_output(mla_prompt_attend(q_abs_p, lat_p), mla_w_uv, mla_w_o)
            q_abs_s, lat_s = mla_project(hs, pos_s, mla_w_in, mla_norm_q, mla_w_uq, mla_norm_kv, mla_w_uk)
            past = cache_mla_latent[page_table].reshape(dec_batch, past_len, MLA_LATENT)
            keys = jnp.concatenate([past, lat_s], axis=1)
            o_s = mla_attend(q_abs_s, keys, pos_s, jnp.arange(past_len + dec_seq, dtype=jnp.int32))
            ms = mla_output(o_s, mla_w_uv, mla_w_o)
            mla_latent_prompt, mla_latent_sample = lat_p, lat_s
        elif kind == 1:
            u_p = glu(hp, conv_w_in, conv_b_in)
            mp = conv_module(jnp.pad(u_p, ((0, 0), (CONV_WIDTH - 1, 0), (0, 0))),
                             conv_w_dw, conv_b_dw, conv_ln_g, conv_ln_b, conv_w_out, conv_b_out)
            conv_state_prompt = u_p[:, seq - (CONV_WIDTH - 1):]
            u_s_ext = jnp.concatenate([state_conv, glu(hs, conv_w_in, conv_b_in)], axis=1)
            ms = conv_module(u_s_ext, conv_w_dw, conv_b_dw, conv_ln_g, conv_ln_b, conv_w_out, conv_b_out)
            conv_state_sample = u_s_ext[:, dec_seq:]
        elif kind == 2:
            z_p, xbc_p, dt_p = ssm_split_in(hp, ssm_w_in)
            act_p = jax.nn.silu(causal_dwconv(jnp.pad(xbc_p, ((0, 0), (SSM_CONV - 1, 0), (0, 0))), ssm_conv_w, ssm_conv_b))
            ssm_conv_prompt = xbc_p[:, seq - (SSM_CONV - 1):]
            xh_p, bm_p, cm_p, dtf_p, a = ssm_prepare(act_p, dt_p, ssm_dt_bias, ssm_a_log)
            y_p, ssm_state_prompt = ssd_chunked(xh_p, dtf_p, a, bm_p, cm_p)
            mp = ssm_output(y_p, xh_p, z_p, ssm_d, ssm_norm, ssm_w_out)
            z_s, xbc_s, dt_s = ssm_split_in(hs, ssm_w_in)
            xbc_s_ext = jnp.concatenate([state_ssm_conv, xbc_s], axis=1)
            act_s = jax.nn.silu(causal_dwconv(xbc_s_ext, ssm_conv_w, ssm_conv_b))
            ssm_conv_sample = xbc_s_ext[:, dec_seq:]
            xh_s, bm_s, cm_s, dtf_s, a = ssm_prepare(act_s, dt_s, ssm_dt_bias, ssm_a_log)
            y_s, ssm_state_sample = ssd_recurrent(xh_s, dtf_s, a, bm_s, cm_s, state_ssm)
            ms = ssm_output(y_s, xh_s, z_s, ssm_d, ssm_norm, ssm_w_out)
        else:
            qp, kp, vp = dil_project(hp, pos_p, dil_w_qkv)
            qs, ksn, vsn = dil_project(hs, pos_s, dil_w_qkv)
            outs_p, lses_p, outs_s, lses_s, dil_new = [], [], [], [], []
            for g, (win, dil) in enumerate(DIL_GROUPS):
                o, l = dilated_prompt_group(qp[:, :, g], kp[:, :, g], vp[:, :, g], win, dil)
                outs_p.append(o)
                lses_p.append(l)
                keep = min(win, seq)
                dil_new.append(jnp.stack([kp[:, seq - keep:, g], vp[:, seq - keep:, g]], axis=2))
                o, l, nbuf = dilated_sample_group(qs[:, :, g], ksn[:, :, g], vsn[:, :, g], dil_bufs_in[g], win, dil)
                outs_s.append(o)
                lses_s.append(l)
                dil_new.append(nbuf)
            mp = dilated_combine(outs_p, lses_p, dil_w_o)
            ms = dilated_combine(outs_s, lses_s, dil_w_o)
            dil_kv0_prompt, dil_kv0_sample = dil_new[0], dil_new[1]
            dil_kv1_prompt, dil_kv1_sample = dil_new[2], dil_new[3]
            dil_kv2_prompt, dil_kv2_sample = dil_new[4], dil_new[5]
        res_p = res_p + mp
        res_s = res_s + ms
        kv_p = memory_kv(mem_prompt, norm_mem[i], cross_w_kv[i])
        mem_kv_layers.append(kv_p)
        res_p = res_p + cross_attend(rmsnorm(res_p, norm_cross[i]), kv_p, cross_w_q[i], cross_w_o[i])
        res_s = res_s + cross_attend(rmsnorm(res_s, norm_cross[i]), cache_mem_kv[i], cross_w_q[i], cross_w_o[i])
        res_p = res_p + 0.5 * swiglu(rmsnorm(res_p, norm_ffn2[i]), ffn2_w_gu[i], ffn2_w_down[i])
        res_s = res_s + 0.5 * swiglu(rmsnorm(res_s, norm_ffn2[i]), ffn2_w_gu[i], ffn2_w_down[i])
    y_prompt = rmsnorm(res_p, norm_final)
    y_sample = rmsnorm(res_s, norm_final)
    mem_kv_prompt = jnp.stack(mem_kv_layers, axis=0)
    return (y_prompt, y_sample, mla_latent_prompt, mla_latent_sample, conv_state_prompt, conv_state_sample,
            ssm_conv_prompt, ssm_conv_sample, ssm_state_prompt, ssm_state_sample,
            dil_kv0_prompt, dil_kv0_sample, dil_kv1_prompt, dil_kv1_sample, dil_kv2_prompt, dil_kv2_sample,
            mem_kv_prompt)
```

```python
import functools
import math

import jax
import jax.numpy as jnp
from jax import lax
from jax.experimental import pallas as pl
from jax.experimental.pallas import tpu as pltpu

F32 = jnp.float32
BF16 = jnp.bfloat16

D_MODEL = 1024
D_FF = 2816
RMS_EPS = 1e-6
LN_EPS = 1e-5
ROPE_THETA = 10000.0

MLA_HEADS = 16
MLA_Q_LORA = 256
MLA_KV_LORA = 128
MLA_NOPE = 64
MLA_ROPE = 32
MLA_V_HD = 64
MLA_LATENT = MLA_KV_LORA + MLA_ROPE
MLA_SCALE = (MLA_NOPE + MLA_ROPE) ** -0.5

CONV_WIDTH = 31
CONV_HALO = 32

D_INNER = 2 * D_MODEL
SSM_HD = 64
SSM_HEADS = D_INNER // SSM_HD
SSM_GROUPS = 4
SSM_REP = SSM_HEADS // SSM_GROUPS
SSM_STATE = 128
SSM_CONV = 4
SSM_CHUNK = 128
SSM_GN = SSM_GROUPS * SSM_STATE
SSM_CONV_DIM = D_INNER + 2 * SSM_GN
SSM_HALO = 8

DIL_GROUPS = ((128, 1), (512, 4), (2048, 16))
N_DIL = 3
DIL_HEADS = 5
DIL_HD = 64
DIL_GW = DIL_HEADS * DIL_HD
DIL_WIDTH = N_DIL * DIL_GW
DIL_NK = 128

MEM_HEADS = 4
MEM_HD = D_MODEL // MEM_HEADS

LANE = 128
NEG_INF = float("-inf")


def _cparams(sem, vmem_mb=48):
    return pltpu.CompilerParams(dimension_semantics=sem, vmem_limit_bytes=vmem_mb * 1024 * 1024)


def _const_spec(shape):
    nd = len(shape)
    return pl.BlockSpec(shape, lambda *_: (0,) * nd, pipeline_mode=pl.Buffered(1))


def _dot(a, b):
    return jnp.dot(a.astype(BF16), b.astype(BF16), preferred_element_type=F32)


def _dot_nt(a, b):
    return lax.dot_general(a.astype(BF16), b.astype(BF16), (((1,), (1,)), ((), ())),
                           preferred_element_type=F32)


def _dot_exact(a, b):
    return jnp.dot(a, b, preferred_element_type=F32, precision=lax.Precision.HIGHEST)


def _dot_nt_exact(a, b):
    return lax.dot_general(a, b, (((1,), (1,)), ((), ())), preferred_element_type=F32,
                           precision=lax.Precision.HIGHEST)


def _rms(x, g):
    return x * lax.rsqrt(jnp.mean(x * x, axis=-1, keepdims=True) + RMS_EPS) * g


def _silu(x):
    return x * jax.nn.sigmoid(x)


def _rot_half(t, dim):
    half = dim // 2
    lane = lax.broadcasted_iota(jnp.int32, t.shape, 1)
    return jnp.where((lane & (dim - 1)) < half, pltpu.roll(t, LANE - half, 1), pltpu.roll(t, half, 1))


def _rope_tables(pos, dim):
    half = dim // 2
    inv_freq = jnp.exp(-math.log(ROPE_THETA) * (2.0 / dim) * jnp.arange(half, dtype=F32))
    ang = pos.astype(F32)[:, None] * inv_freq[None, :]
    cos, sin = jnp.cos(ang), jnp.sin(ang)
    reps = LANE // dim
    return (jnp.tile(jnp.concatenate([cos, cos], axis=1), (1, reps)),
            jnp.tile(jnp.concatenate([-sin, sin], axis=1), (1, reps)))


def _softmax_rows(s):
    m = jnp.max(s, axis=-1, keepdims=True)
    p = jnp.exp(s - m)
    return p / jnp.sum(p, axis=-1, keepdims=True)


def _ffn_kernel(has_final, x_ref, g_ref, wgu_ref, wd_ref, *rest):
    o_ref = rest[-1]
    x = x_ref[...]
    h = _dot(_rms(x, g_ref[...]), wgu_ref[...])
    act = _silu(h[:, :D_FF]) * h[:, D_FF:]
    y = x + 0.5 * _dot(act, wd_ref[...])
    if has_final:
        y = _rms(y, rest[0][...])
    o_ref[...] = y


def _ffn(x, g, wgu, wd, final_g=None, tm=256):
    m, d = x.shape
    tm = min(tm, m)
    args = [x, g.reshape(1, d), wgu, wd]
    in_specs = [pl.BlockSpec((tm, d), lambda i: (i, 0)), _const_spec((1, d)),
                _const_spec(wgu.shape), _const_spec(wd.shape)]
    if final_g is not None:
        args.append(final_g.reshape(1, d))
        in_specs.append(_const_spec((1, d)))
    return pl.pallas_call(
        functools.partial(_ffn_kernel, final_g is not None),
        grid=(m // tm,), in_specs=in_specs,
        out_specs=pl.BlockSpec((tm, d), lambda i: (i, 0)),
        out_shape=jax.ShapeDtypeStruct((m, d), F32),
        compiler_params=_cparams(("parallel",), 56), name="ffn")(*args)


def _norm_proj_kernel(n, has_bias, x_ref, g_ref, *refs):
    xn = _rms(x_ref[...], g_ref[...]).astype(BF16)
    for i in range(n):
        y = jnp.dot(xn, refs[i][...], preferred_element_type=F32)
        if has_bias:
            y = y + refs[n + i][...]
        refs[-n + i][...] = y


def _norm_proj(x, g, ws, bs=None, tm=256, name="norm_proj"):
    m, d = x.shape
    tm = min(tm, m)
    n = len(ws)
    args = [x, g.reshape(1, d)] + list(ws)
    in_specs = [pl.BlockSpec((tm, d), lambda i: (i, 0)), _const_spec((1, d))] + [_const_spec(w.shape) for w in ws]
    if bs is not None:
        args += [b.reshape(1, -1) for b in bs]
        in_specs += [_const_spec((1, b.shape[-1])) for b in bs]
    outs = pl.pallas_call(
        functools.partial(_norm_proj_kernel, n, bs is not None),
        grid=(m // tm,), in_specs=in_specs,
        out_specs=[pl.BlockSpec((tm, w.shape[1]), lambda i: (i, 0)) for w in ws],
        out_shape=[jax.ShapeDtypeStruct((m, w.shape[1]), F32) for w in ws],
        compiler_params=_cparams(("parallel",)), name=name)(*args)
    return outs


def _proj_res_kernel(n, has_bias, *refs):
    a_refs, w_refs = refs[:n], refs[n:2 * n]
    res_ref, o_ref = refs[2 * n], refs[-1]
    y = res_ref[...]
    for a_ref, w_ref in zip(a_refs, w_refs):
        y = y + _dot(a_ref[...], w_ref[...])
    if has_bias:
        y = y + refs[2 * n + 1][...]
    o_ref[...] = y


def _proj_res(a_list, w_list, res, bias=None, tm=256, name="proj_res"):
    m, d = res.shape
    tm = min(tm, m)
    n = len(a_list)
    args = list(a_list) + list(w_list) + [res]
    in_specs = ([pl.BlockSpec((tm, a.shape[1]), lambda i: (i, 0)) for a in a_list]
                + [_const_spec(w.shape) for w in w_list] + [pl.BlockSpec((tm, d), lambda i: (i, 0))])
    if bias is not None:
        args.append(bias.reshape(1, d))
        in_specs.append(_const_spec((1, d)))
    return pl.pallas_call(
        functools.partial(_proj_res_kernel, n, bias is not None),
        grid=(m // tm,), in_specs=in_specs,
        out_specs=pl.BlockSpec((tm, d), lambda i: (i, 0)),
        out_shape=jax.ShapeDtypeStruct((m, d), F32),
        compiler_params=_cparams(("parallel",)), name=name)(*args)


def _memkv_kernel(x_ref, g_ref, w_ref, o_ref):
    o_ref[0] = _dot(_rms(x_ref[...], g_ref[0]), w_ref[0])


def _memkv(mem2d, norm_mem, w_kv, tm=256):
    m, d = mem2d.shape
    depth = w_kv.shape[0]
    n = w_kv.shape[2]
    return pl.pallas_call(
        _memkv_kernel, grid=(depth, m // tm),
        in_specs=[pl.BlockSpec((tm, d), lambda l, i: (i, 0)),
                  pl.BlockSpec((1, 1, d), lambda l, i: (l, 0, 0)),
                  pl.BlockSpec((1, d, n), lambda l, i: (l, 0, 0))],
        out_specs=pl.BlockSpec((1, tm, n), lambda l, i: (l, i, 0)),
        out_shape=jax.ShapeDtypeStruct((depth, m, n), F32),
        compiler_params=_cparams(("parallel", "parallel")), name="memkv")(mem2d, norm_mem.reshape(depth, 1, d), w_kv)


def _xattn_core(q, kv):
    kvb = kv.astype(BF16)
    outs = []
    for h in range(MEM_HEADS):
        kh = kvb[:, h * MEM_HD:(h + 1) * MEM_HD]
        vh = kvb[:, D_MODEL + h * MEM_HD:D_MODEL + (h + 1) * MEM_HD]
        s = _dot_nt(q[:, h * MEM_HD:(h + 1) * MEM_HD], kh) * MEM_HD ** -0.5
        outs.append(_dot(_softmax_rows(s), vh))
    return jnp.concatenate(outs, axis=1)


def _xattn_prompt_kernel(x_ref, g_ref, wq_ref, kv_ref, wo_ref, o_ref):
    x = x_ref[...]
    q = _dot(_rms(x, g_ref[...]), wq_ref[...])
    o_ref[...] = x + _dot(_xattn_core(q, kv_ref[0]), wo_ref[...])


def _xattn_prompt(x, g, wq, kv, wo, seq, tq=256):
    m, d = x.shape
    nq = seq // tq
    bsz, ml, kvw = kv.shape
    return pl.pallas_call(
        _xattn_prompt_kernel, grid=(bsz, nq),
        in_specs=[pl.BlockSpec((tq, d), lambda b, i: (b * nq + i, 0)), _const_spec((1, d)), _const_spec(wq.shape),
                  pl.BlockSpec((1, ml, kvw), lambda b, i: (b, 0, 0)), _const_spec(wo.shape)],
        out_specs=pl.BlockSpec((tq, d), lambda b, i: (b * nq + i, 0)),
        out_shape=jax.ShapeDtypeStruct((m, d), F32),
        compiler_params=_cparams(("parallel", "parallel")), name="xattn_prompt")(x, g.reshape(1, d), wq, kv, wo)


def _xattn_sample_kernel(q_ref, kv_ref, o_ref):
    q = jnp.broadcast_to(q_ref[0], (8, D_MODEL))
    o_ref[0] = _xattn_core(q, kv_ref[0])[0:1]


def _xattn_sample(q, kv):
    bsz, d = q.shape
    _, ml, kvw = kv.shape
    out = pl.pallas_call(
        _xattn_sample_kernel, grid=(bsz,),
        in_specs=[pl.BlockSpec((1, 1, d), lambda b: (b, 0, 0)), pl.BlockSpec((1, ml, kvw), lambda b: (b, 0, 0))],
        out_specs=pl.BlockSpec((1, 1, d), lambda b: (b, 0, 0)),
        out_shape=jax.ShapeDtypeStruct((bsz, 1, d), F32),
        compiler_params=_cparams(("parallel",)), name="xattn_sample")(q.reshape(bsz, 1, d), kv)
    return out.reshape(bsz, d)


def _mla_proj_kernel(x_ref, g_ref, win_ref, nq_ref, wuq_ref, nkv_ref, wuk_ref, cos_ref, sin_ref, q_ref, lat_ref):
    cos, sin = cos_ref[...], sin_ref[...]
    a = _dot(_rms(x_ref[...], g_ref[...]), win_ref[...])
    ckv = _rms(a[:, MLA_Q_LORA:MLA_Q_LORA + MLA_KV_LORA], nkv_ref[...])
    pe = a[:, MLA_Q_LORA + MLA_KV_LORA:]
    kpe = pe * cos + _rot_half(pe, MLA_ROPE) * sin
    lat_ref[:, :MLA_KV_LORA] = ckv
    lat_ref[:, MLA_KV_LORA:] = kpe[:, :MLA_ROPE]
    q = _dot(_rms(a[:, :MLA_Q_LORA], nq_ref[...]), wuq_ref[...])
    n_nope = MLA_HEADS * MLA_NOPE
    for p in range(MLA_HEADS // 2):
        ql = _dot(q[:, p * LANE:(p + 1) * LANE], wuk_ref[p])
        q_ref[2 * p, :, :MLA_KV_LORA] = ql[:, :MLA_KV_LORA]
        q_ref[2 * p + 1, :, :MLA_KV_LORA] = ql[:, MLA_KV_LORA:]
    per_tile = LANE // MLA_ROPE
    for t in range(MLA_HEADS // per_tile):
        tile = q[:, n_nope + t * LANE:n_nope + (t + 1) * LANE]
        qpe = tile * cos + _rot_half(tile, MLA_ROPE) * sin
        for j in range(per_tile):
            q_ref[t * per_tile + j, :, MLA_KV_LORA:] = qpe[:, j * MLA_ROPE:(j + 1) * MLA_ROPE]


def _mla_proj(x, g, w, cos, sin, tm=256):
    m, d = x.shape
    tm = min(tm, m)
    nt = cos.shape[0] // tm
    consts = [g.reshape(1, d), w["win"], w["nq"], w["wuq"], w["nkv"], w["wuk"]]
    return pl.pallas_call(
        _mla_proj_kernel, grid=(m // tm,),
        in_specs=([pl.BlockSpec((tm, d), lambda i: (i, 0))] + [_const_spec(c.shape) for c in consts]
                  + [pl.BlockSpec((tm, LANE), lambda i: (i % nt, 0))] * 2),
        out_specs=[pl.BlockSpec((MLA_HEADS, tm, MLA_LATENT), lambda i: (0, i, 0)),
                   pl.BlockSpec((tm, MLA_LATENT), lambda i: (i, 0))],
        out_shape=[jax.ShapeDtypeStruct((MLA_HEADS, m, MLA_LATENT), F32),
                   jax.ShapeDtypeStruct((m, MLA_LATENT), F32)],
        compiler_params=_cparams(("parallel",)), name="mla_proj")(x, *consts, cos, sin)


def _mla_attn_kernel(tq, tk, q_ref, lat_ref, o_ref):
    i = pl.program_id(1)
    rows = MLA_HEADS * tq
    q = q_ref[...].reshape(rows, MLA_LATENT).astype(BF16)
    row = lax.broadcasted_iota(jnp.int32, (rows, tk), 0)
    col = lax.broadcasted_iota(jnp.int32, (rows, tk), 1)
    rel = col - (row & (tq - 1))
    n_chunks = (i * tq + tq + tk - 1) // tk

    def body(c, carry):
        m, l, acc = carry
        keys = lat_ref[pl.ds(pl.multiple_of(c * tk, tk), tk), :].astype(BF16)
        s = _dot_nt(q, keys) * MLA_SCALE
        s = jnp.where(rel <= i * tq - c * tk, s, NEG_INF)
        m_new = jnp.maximum(m, jnp.max(s, axis=-1, keepdims=True))
        alpha = jnp.exp(m - m_new)
        p = jnp.exp(s - m_new)
        l = alpha * l + jnp.sum(p, axis=-1, keepdims=True)
        acc = alpha * acc + _dot(p, keys[:, :MLA_KV_LORA])
        return m_new, l, acc

    init = (jnp.full((rows, 1), NEG_INF, F32), jnp.zeros((rows, 1), F32), jnp.zeros((rows, MLA_KV_LORA), F32))
    _, l, acc = lax.fori_loop(0, n_chunks, body, init)
    o_ref[...] = (acc / l).reshape(MLA_HEADS, tq, MLA_KV_LORA)


def _mla_attn_prompt(qabs, lat, bsz, seq, tq=128, tk=512):
    m = bsz * seq
    nq = seq // tq
    return pl.pallas_call(
        functools.partial(_mla_attn_kernel, tq, tk), grid=(bsz, nq),
        in_specs=[pl.BlockSpec((MLA_HEADS, tq, MLA_LATENT), lambda b, i: (0, b * nq + i, 0)),
                  pl.BlockSpec((seq, MLA_LATENT), lambda b, i: (b, 0))],
        out_specs=pl.BlockSpec((MLA_HEADS, tq, MLA_KV_LORA), lambda b, i: (0, b * nq + i, 0)),
        out_shape=jax.ShapeDtypeStruct((MLA_HEADS, m, MLA_KV_LORA), F32),
        compiler_params=_cparams(("parallel", "parallel")), name="mla_attn")(qabs, lat)


def _mla_decode_kernel(pages_per_chunk, n_pages, page, pt_ref, q_ref, new_ref, cache_ref, o_ref, buf, sem):
    b = pl.program_id(0)
    n_chunks = n_pages // pages_per_chunk

    def copies(c, slot):
        return [pltpu.make_async_copy(cache_ref.at[pt_ref[b, c * pages_per_chunk + j]],
                                      buf.at[slot, pl.ds(j * page, page)], sem.at[slot])
                for j in range(pages_per_chunk)]

    for cp in copies(0, 0):
        cp.start()
    q = q_ref[0].astype(BF16)

    def body(c, carry):
        m, l, acc = carry
        slot = c & 1

        @pl.when(c + 1 < n_chunks)
        def _():
            for cp in copies(c + 1, 1 - slot):
                cp.start()

        for cp in copies(c, slot):
            cp.wait()
        keys = buf[slot].astype(BF16)
        s = _dot_nt(q, keys) * MLA_SCALE
        m_new = jnp.maximum(m, jnp.max(s, axis=-1, keepdims=True))
        alpha = jnp.exp(m - m_new)
        p = jnp.exp(s - m_new)
        l = alpha * l + jnp.sum(p, axis=-1, keepdims=True)
        acc = alpha * acc + _dot(p, keys[:, :MLA_KV_LORA])
        return m_new, l, acc

    init = (jnp.full((MLA_HEADS, 1), NEG_INF, F32), jnp.zeros((MLA_HEADS, 1), F32),
            jnp.zeros((MLA_HEADS, MLA_KV_LORA), F32))
    m, l, acc = lax.fori_loop(0, n_chunks, body, init)
    new = new_ref[0].astype(BF16).astype(F32)
    s_new = jnp.sum(q.astype(F32) * new, axis=-1, keepdims=True) * MLA_SCALE
    m_new = jnp.maximum(m, s_new)
    alpha = jnp.exp(m - m_new)
    p_new = jnp.exp(s_new - m_new)
    l = alpha * l + p_new
    acc = alpha * acc + p_new.astype(BF16).astype(F32) * new[:, :MLA_KV_LORA]
    o_ref[0] = acc / l


def _mla_attn_sample(q, new_rows, cache, page_table, pages_per_chunk=8):
    bsz = q.shape[0]
    n_pages = page_table.shape[1]
    page = cache.shape[1]
    grid_spec = pltpu.PrefetchScalarGridSpec(
        num_scalar_prefetch=1, grid=(bsz,),
        in_specs=[pl.BlockSpec((1, MLA_HEADS, MLA_LATENT), lambda b, pt: (b, 0, 0)),
                  pl.BlockSpec((1, 1, MLA_LATENT), lambda b, pt: (b, 0, 0)),
                  pl.BlockSpec(memory_space=pl.ANY)],
        out_specs=pl.BlockSpec((1, MLA_HEADS, MLA_KV_LORA), lambda b, pt: (b, 0, 0)),
        scratch_shapes=[pltpu.VMEM((2, pages_per_chunk * page, MLA_LATENT), F32), pltpu.SemaphoreType.DMA((2,))])
    return pl.pallas_call(
        functools.partial(_mla_decode_kernel, pages_per_chunk, n_pages, page), grid_spec=grid_spec,
        out_shape=jax.ShapeDtypeStruct((bsz, MLA_HEADS, MLA_KV_LORA), F32),
        compiler_params=_cparams(("arbitrary",)), name="mla_decode")(page_table, q, new_rows, cache)


def _mla_out_kernel(o_ref, wuv_ref, wo_ref, res_ref, out_ref):
    parts = []
    for p in range(MLA_HEADS // 2):
        pair = jnp.concatenate([o_ref[2 * p], o_ref[2 * p + 1]], axis=1)
        parts.append(_dot(pair, wuv_ref[p]))
    out_ref[...] = res_ref[...] + _dot(jnp.concatenate(parts, axis=1), wo_ref[...])


def _mla_out(o_lat, wuv, wo, res, tm=256):
    m, d = res.shape
    tm = min(tm, m)
    return pl.pallas_call(
        _mla_out_kernel, grid=(m // tm,),
        in_specs=[pl.BlockSpec((MLA_HEADS, tm, MLA_KV_LORA), lambda i: (0, i, 0)), _const_spec(wuv.shape),
                  _const_spec(wo.shape), pl.BlockSpec((tm, d), lambda i: (i, 0))],
        out_specs=pl.BlockSpec((tm, d), lambda i: (i, 0)),
        out_shape=jax.ShapeDtypeStruct((m, d), F32),
        compiler_params=_cparams(("parallel",)), name="mla_out")(o_lat, wuv, wo, res)


def _conv_tail(y, lng, lnb, wout, bout):
    mu = jnp.mean(y, axis=-1, keepdims=True)
    var = jnp.mean(jnp.square(y - mu), axis=-1, keepdims=True)
    yn = (y - mu) * lax.rsqrt(var + LN_EPS) * lng + lnb
    return _dot(_silu(yn), wout) + bout


def _conv_prompt_kernel(ts, x_ref, g_ref, win_ref, bin_ref, wdw_ref, bdw_ref, lng_ref, lnb_ref, wout_ref, bout_ref,
                        o_ref, st_ref, ext):
    j = pl.program_id(1)

    @pl.when(j == 0)
    def _():
        ext[0:CONV_HALO, :] = jnp.zeros((CONV_HALO, D_MODEL), F32)

    x = x_ref[...]
    a = _dot(_rms(x, g_ref[...]), win_ref[...]) + bin_ref[...]
    ext[CONV_HALO:CONV_HALO + ts, :] = a[:, :D_MODEL] * jax.nn.sigmoid(a[:, D_MODEL:])
    base = CONV_HALO - (CONV_WIDTH - 1)
    y = jnp.zeros((ts, D_MODEL), F32)
    for k in range(CONV_WIDTH):
        y = y + wdw_ref[k:k + 1, :] * ext[base + k:base + k + ts, :]
    y = y + bdw_ref[...]
    o_ref[...] = x + _conv_tail(y, lng_ref[...], lnb_ref[...], wout_ref[...], bout_ref[...])

    @pl.when(j == pl.num_programs(1) - 1)
    def _():
        st_ref[0] = ext[CONV_HALO + ts - (CONV_WIDTH - 1):CONV_HALO + ts, :]

    ext[0:CONV_HALO, :] = ext[ts:ts + CONV_HALO, :]


def _conv_prompt(x, g, w, bsz, seq, ts=256):
    m, d = x.shape
    ns = seq // ts
    consts = [g.reshape(1, d), w["win"], w["bin"], w["wdw"], w["bdw"], w["lng"], w["lnb"], w["wout"], w["bout"]]
    return pl.pallas_call(
        functools.partial(_conv_prompt_kernel, ts), grid=(bsz, ns),
        in_specs=[pl.BlockSpec((ts, d), lambda b, j: (b * ns + j, 0))] + [_const_spec(c.shape) for c in consts],
        out_specs=[pl.BlockSpec((ts, d), lambda b, j: (b * ns + j, 0)),
                   pl.BlockSpec((1, CONV_WIDTH - 1, d), lambda b, j: (b, 0, 0))],
        out_shape=[jax.ShapeDtypeStruct((m, d), F32), jax.ShapeDtypeStruct((bsz, CONV_WIDTH - 1, d), F32)],
        scratch_shapes=[pltpu.VMEM((CONV_HALO + ts, d), F32)],
        compiler_params=_cparams(("parallel", "arbitrary")), name="conv_prompt")(x, *consts)


def _conv_sample_kernel(x_ref, g_ref, win_ref, bin_ref, st_ref, wdw_ref, bdw_ref, lng_ref, lnb_ref, wout_ref,
                        bout_ref, o_ref, u_ref):
    x = x_ref[...]
    a = _dot(_rms(x, g_ref[...]), win_ref[...]) + bin_ref[...]
    u = a[:, :D_MODEL] * jax.nn.sigmoid(a[:, D_MODEL:])
    u_ref[...] = u
    y = jnp.zeros_like(u)
    for k in range(CONV_WIDTH - 1):
        y = y + wdw_ref[k:k + 1, :] * st_ref[k]
    y = y + wdw_ref[CONV_WIDTH - 1:CONV_WIDTH, :] * u + bdw_ref[...]
    o_ref[...] = x + _conv_tail(y, lng_ref[...], lnb_ref[...], wout_ref[...], bout_ref[...])


def _conv_sample(x, g, w, state_t):
    m, d = x.shape
    args = [x, g.reshape(1, d), w["win"], w["bin"], state_t, w["wdw"], w["bdw"], w["lng"], w["lnb"], w["wout"],
            w["bout"]]
    return pl.pallas_call(
        _conv_sample_kernel, grid=(1,),
        in_specs=[_const_spec(a.shape) for a in args],
        out_specs=[_const_spec((m, d)), _const_spec((m, d))],
        out_shape=[jax.ShapeDtypeStruct((m, d), F32), jax.ShapeDtypeStruct((m, d), F32)],
        compiler_params=_cparams(("arbitrary",)), name="conv_sample")(*args)


def _ssm_in_prompt_kernel(ts, x_ref, g_ref, wz_ref, wxbc_ref, wdt_ref, cw_ref, cb_ref, dtb_ref,
                          z_ref, act_ref, dt_ref, cst_ref, ext):
    j = pl.program_id(1)

    @pl.when(j == 0)
    def _():
        ext[0:SSM_HALO, :] = jnp.zeros((SSM_HALO, SSM_CONV_DIM), F32)

    xn = _rms(x_ref[...], g_ref[...]).astype(BF16)
    z_ref[...] = jnp.dot(xn, wz_ref[...], preferred_element_type=F32)
    dt_ref[...] = jax.nn.softplus(jnp.dot(xn, wdt_ref[...], preferred_element_type=F32) + dtb_ref[...])
    ext[SSM_HALO:SSM_HALO + ts, :] = jnp.dot(xn, wxbc_ref[...], preferred_element_type=F32)
    base = SSM_HALO - (SSM_CONV - 1)
    y = jnp.zeros((ts, SSM_CONV_DIM), F32)
    for k in range(SSM_CONV):
        y = y + cw_ref[k:k + 1, :] * ext[base + k:base + k + ts, :]
    act_ref[...] = _silu(y + cb_ref[...])

    @pl.when(j == pl.num_programs(1) - 1)
    def _():
        cst_ref[0] = ext[SSM_HALO + ts - (SSM_CONV - 1):SSM_HALO + ts, :]

    ext[0:SSM_HALO, :] = ext[ts:ts + SSM_HALO, :]


def _ssm_in_prompt(x, g, w, bsz, seq, ts=256):
    m, d = x.shape
    ns = seq // ts
    consts = [g.reshape(1, d), w["wz"], w["wxbc"], w["wdt"], w["cw"], w["cb"], w["dtb"]]
    row = lambda b, j: (b * ns + j, 0)
    return pl.pallas_call(
        functools.partial(_ssm_in_prompt_kernel, ts), grid=(bsz, ns),
        in_specs=[pl.BlockSpec((ts, d), row)] + [_const_spec(c.shape) for c in consts],
        out_specs=[pl.BlockSpec((ts, D_INNER), row), pl.BlockSpec((ts, SSM_CONV_DIM), row),
                   pl.BlockSpec((ts, SSM_HEADS), row),
                   pl.BlockSpec((1, SSM_CONV - 1, SSM_CONV_DIM), lambda b, j: (b, 0, 0))],
        out_shape=[jax.ShapeDtypeStruct((m, D_INNER), F32), jax.ShapeDtypeStruct((m, SSM_CONV_DIM), F32),
                   jax.ShapeDtypeStruct((m, SSM_HEADS), F32),
                   jax.ShapeDtypeStruct((bsz, SSM_CONV - 1, SSM_CONV_DIM), F32)],
        scratch_shapes=[pltpu.VMEM((SSM_HALO + ts, SSM_CONV_DIM), F32)],
        compiler_params=_cparams(("parallel", "arbitrary")), name="ssm_in_prompt")(x, *consts)


def _ssd_scan_kernel(act_ref, dt_ref, alog_ref, d_ref, y_ref, h_ref):
    c = pl.program_id(1)
    L = SSM_CHUNK

    @pl.when(c == 0)
    def _():
        h_ref[...] = jnp.zeros(h_ref.shape, F32)

    dt = dt_ref[...]
    da = dt * (-jnp.exp(alog_ref[...]))
    ri = lax.broadcasted_iota(jnp.int32, (L, L), 0)
    ci = lax.broadcasted_iota(jnp.int32, (L, L), 1)
    tril = ri >= ci
    cs = _dot_exact(tril.astype(F32), da)
    eye = (lax.broadcasted_iota(jnp.int32, (SSM_HEADS, SSM_HEADS), 0)
           == lax.broadcasted_iota(jnp.int32, (SSM_HEADS, SSM_HEADS), 1)).astype(F32)
    cs_t = _dot_nt_exact(eye, cs)
    cs_last = cs[L - 1:L, :]
    ecs = jnp.exp(cs)
    dec = jnp.exp(cs_last - cs)
    cdec = jnp.exp(cs_last)
    dvec = d_ref[...]
    for g in range(SSM_GROUPS):
        bg = act_ref[:, D_INNER + g * SSM_STATE:D_INNER + (g + 1) * SSM_STATE]
        cg = act_ref[:, D_INNER + SSM_GN + g * SSM_STATE:D_INNER + SSM_GN + (g + 1) * SSM_STATE].astype(BF16)
        cb = _dot_nt(cg, bg)
        bg_t = bg.T.astype(BF16)
        for r8 in range(SSM_REP):
            r = g * SSM_REP + r8
            xr = act_ref[:, r * SSM_HD:(r + 1) * SSM_HD]
            xd = xr * dt[:, r:r + 1]
            lmat = jnp.exp(jnp.where(tril, cs[:, r:r + 1] - cs_t[r:r + 1, :], NEG_INF))
            h_t = h_ref[0, r]
            y = _dot(cb * lmat, xd) + _dot(cg, h_t) * ecs[:, r:r + 1] + xr * dvec[:, r:r + 1]
            y_ref[:, r * SSM_HD:(r + 1) * SSM_HD] = y
            h_ref[0, r] = cdec[:, r:r + 1] * h_t + _dot(bg_t, xd * dec[:, r:r + 1])


def _ssd_scan(act, dt, alog, dvec, bsz, seq):
    m = act.shape[0]
    nc = seq // SSM_CHUNK
    row = lambda b, c: (b * nc + c, 0)
    return pl.pallas_call(
        _ssd_scan_kernel, grid=(bsz, nc),
        in_specs=[pl.BlockSpec((SSM_CHUNK, SSM_CONV_DIM), row), pl.BlockSpec((SSM_CHUNK, SSM_HEADS), row),
                  _const_spec((1, SSM_HEADS)), _const_spec((1, SSM_HEADS))],
        out_specs=[pl.BlockSpec((SSM_CHUNK, D_INNER), row),
                   pl.BlockSpec((1, SSM_HEADS, SSM_STATE, SSM_HD), lambda b, c: (b, 0, 0, 0))],
        out_shape=[jax.ShapeDtypeStruct((m, D_INNER), F32),
                   jax.ShapeDtypeStruct((bsz, SSM_HEADS, SSM_STATE, SSM_HD), F32)],
        compiler_params=_cparams(("parallel", "arbitrary")), name="ssd_scan")(
            act, dt, alog.reshape(1, SSM_HEADS), dvec.reshape(1, SSM_HEADS))


def _ssm_out_kernel(y_ref, z_ref, n_ref, w_ref, res_ref, o_ref):
    yz = y_ref[...] * _silu(z_ref[...])
    gw = D_INNER // SSM_GROUPS
    parts = []
    for g in range(SSM_GROUPS):
        seg = yz[:, g * gw:(g + 1) * gw]
        parts.append(seg * lax.rsqrt(jnp.mean(seg * seg, axis=-1, keepdims=True) + RMS_EPS))
    o_ref[...] = res_ref[...] + _dot(jnp.concatenate(parts, axis=1) * n_ref[...], w_ref[...])


def _ssm_out(y, z, norm, wout, res, tm=256):
    m, d = res.shape
    tm = min(tm, m)
    row = lambda i: (i, 0)
    return pl.pallas_call(
        _ssm_out_kernel, grid=(m // tm,),
        in_specs=[pl.BlockSpec((tm, D_INNER), row), pl.BlockSpec((tm, D_INNER), row), _const_spec((1, D_INNER)),
                  _const_spec(wout.shape), pl.BlockSpec((tm, d), row)],
        out_specs=pl.BlockSpec((tm, d), row),
        out_shape=jax.ShapeDtypeStruct((m, d), F32),
        compiler_params=_cparams(("parallel",)), name="ssm_out")(y, z, norm.reshape(1, D_INNER), wout, res)


def _ssm_in_sample_kernel(x_ref, g_ref, wz_ref, wxbc_ref, wdt_ref, cw_ref, cb_ref, dtb_ref, st_ref,
                          z_ref, xbc_ref, act_ref, dt_ref):
    xn = _rms(x_ref[...], g_ref[...]).astype(BF16)
    z_ref[...] = jnp.dot(xn, wz_ref[...], preferred_element_type=F32)
    dt_ref[...] = jax.nn.softplus(jnp.dot(xn, wdt_ref[...], preferred_element_type=F32) + dtb_ref[...])
    xbc = jnp.dot(xn, wxbc_ref[...], preferred_element_type=F32)
    xbc_ref[...] = xbc
    y = jnp.zeros_like(xbc)
    for k in range(SSM_CONV - 1):
        y = y + cw_ref[k:k + 1, :] * st_ref[k]
    y = y + cw_ref[SSM_CONV - 1:SSM_CONV, :] * xbc
    act_ref[...] = _silu(y + cb_ref[...])


def _ssm_in_sample(x, g, w, state_t):
    m, d = x.shape
    args = [x, g.reshape(1, d), w["wz"], w["wxbc"], w["wdt"], w["cw"], w["cb"], w["dtb"], state_t]
    shapes = [(m, D_INNER), (m, SSM_CONV_DIM), (m, SSM_CONV_DIM), (m, SSM_HEADS)]
    return pl.pallas_call(
        _ssm_in_sample_kernel, grid=(1,),
        in_specs=[_const_spec(a.shape) for a in args],
        out_specs=[_const_spec(s) for s in shapes],
        out_shape=[jax.ShapeDtypeStruct(s, F32) for s in shapes],
        compiler_params=_cparams(("arbitrary",)), name="ssm_in_sample")(*args)


def _ssm_step_kernel(act_ref, dt_ref, alog_ref, d_ref, h_ref, y_ref, hn_ref):
    act = act_ref[0]
    dt = dt_ref[0]
    decay = jnp.exp(dt * (-jnp.exp(alog_ref[...])))
    dvec = d_ref[...]
    eye = (lax.broadcasted_iota(jnp.int32, (SSM_HD, SSM_HD), 0)
           == lax.broadcasted_iota(jnp.int32, (SSM_HD, SSM_HD), 1))
    for g in range(SSM_GROUPS):
        bg = act[:, D_INNER + g * SSM_STATE:D_INNER + (g + 1) * SSM_STATE]
        cg = jnp.broadcast_to(act[:, D_INNER + SSM_GN + g * SSM_STATE:D_INNER + SSM_GN + (g + 1) * SSM_STATE],
                              (8, SSM_STATE))
        for r8 in range(SSM_REP):
            r = g * SSM_REP + r8
            xr = act[:, r * SSM_HD:(r + 1) * SSM_HD]
            xdt = xr * dt[:, r:r + 1]
            xcol = jnp.sum(jnp.where(eye, jnp.broadcast_to(xdt, (SSM_HD, SSM_HD)), 0.0), axis=1, keepdims=True)
            h_new = decay[:, r:r + 1] * h_ref[0, r] + xcol * bg
            hn_ref[0, r] = h_new
            y = _dot_nt(cg, h_new)[0:1] + xr * dvec[:, r:r + 1]
            y_ref[0, :, r * SSM_HD:(r + 1) * SSM_HD] = y


def _ssm_step(act, dt, alog, dvec, state):
    bsz = act.shape[0]
    blk3 = lambda b: (b, 0, 0)
    y, hn = pl.pallas_call(
        _ssm_step_kernel, grid=(bsz,),
        in_specs=[pl.BlockSpec((1, 1, SSM_CONV_DIM), blk3), pl.BlockSpec((1, 1, SSM_HEADS), blk3),
                  _const_spec((1, SSM_HEADS)), _const_spec((1, SSM_HEADS)),
                  pl.BlockSpec((1, SSM_HEADS, SSM_HD, SSM_STATE), lambda b: (b, 0, 0, 0))],
        out_specs=[pl.BlockSpec((1, 1, D_INNER), blk3),
                   pl.BlockSpec((1, SSM_HEADS, SSM_HD, SSM_STATE), lambda b: (b, 0, 0, 0))],
        out_shape=[jax.ShapeDtypeStruct((bsz, 1, D_INNER), F32),
                   jax.ShapeDtypeStruct((bsz, SSM_HEADS, SSM_HD, SSM_STATE), F32)],
        compiler_params=_cparams(("parallel",)), name="ssm_step")(
            act.reshape(bsz, 1, SSM_CONV_DIM), dt.reshape(bsz, 1, SSM_HEADS), alog.reshape(1, SSM_HEADS),
            dvec.reshape(1, SSM_HEADS), state)
    return y.reshape(bsz, D_INNER), hn


def _dil_qkv_kernel(x_ref, g_ref, w_ref, cos_ref, sin_ref, q0_ref, q1_ref, q2_ref, kv0_ref, kv1_ref, kv2_ref):
    cos, sin = cos_ref[...], sin_ref[...]
    h = _dot(_rms(x_ref[...], g_ref[...]), w_ref[...])
    tiles = []
    for t in range(2 * DIL_WIDTH // LANE):
        tile = h[:, t * LANE:(t + 1) * LANE]
        tiles.append(tile * cos + _rot_half(tile, DIL_HD) * sin)
    qk = jnp.concatenate(tiles, axis=1)
    for g, (q_ref, kv_ref) in enumerate(((q0_ref, kv0_ref), (q1_ref, kv1_ref), (q2_ref, kv2_ref))):
        q_ref[...] = qk[:, g * DIL_GW:(g + 1) * DIL_GW]
        kv_ref[:, :DIL_GW] = qk[:, DIL_WIDTH + g * DIL_GW:DIL_WIDTH + (g + 1) * DIL_GW]
        kv_ref[:, DIL_GW:] = h[:, 2 * DIL_WIDTH + g * DIL_GW:2 * DIL_WIDTH + (g + 1) * DIL_GW]


def _dil_qkv(x, g, w, cos, sin, tm=256):
    m, d = x.shape
    tm = min(tm, m)
    nt = cos.shape[0] // tm
    row = lambda i: (i, 0)
    widths = [DIL_GW] * N_DIL + [2 * DIL_GW] * N_DIL
    return pl.pallas_call(
        _dil_qkv_kernel, grid=(m // tm,),
        in_specs=[pl.BlockSpec((tm, d), row), _const_spec((1, d)), _const_spec(w.shape)]
        + [pl.BlockSpec((tm, LANE), lambda i: (i % nt, 0))] * 2,
        out_specs=[pl.BlockSpec((tm, wd), row) for wd in widths],
        out_shape=[jax.ShapeDtypeStruct((m, wd), F32) for wd in widths],
        compiler_params=_cparams(("parallel",)), name="dil_qkv")(x, g.reshape(1, d), w, cos, sin)


def _dil_band_kernel(sub_len, q_ref, kv_ref, o_ref, lse_ref):
    nk = DIL_NK
    qi = lax.broadcasted_iota(jnp.int32, (nk, 2 * nk), 0)
    ki = lax.broadcasted_iota(jnp.int32, (nk, 2 * nk), 1)
    band = (ki <= qi + nk) & (ki >= qi)
    causal = lax.broadcasted_iota(jnp.int32, (nk, nk), 1) <= lax.broadcasted_iota(jnp.int32, (nk, nk), 0)
    for blk in range(sub_len // nk):
        rows = pl.ds(blk * nk, nk)
        q = q_ref[0, rows, :]
        if blk == 0:
            kvb, mask = kv_ref[0, rows, :].astype(BF16), causal
        else:
            kvb, mask = kv_ref[0, pl.ds((blk - 1) * nk, 2 * nk), :].astype(BF16), band
        for h in range(DIL_HEADS):
            cols = slice(h * DIL_HD, (h + 1) * DIL_HD)
            s = _dot_nt(q[:, cols], kvb[:, cols]) * DIL_HD ** -0.5
            s = jnp.where(mask, s, NEG_INF)
            mx = jnp.max(s, axis=-1, keepdims=True)
            lse = mx + jnp.log(jnp.sum(jnp.exp(s - mx), axis=-1, keepdims=True))
            o_ref[0, rows, cols] = _dot(jnp.exp(s - lse), kvb[:, DIL_GW + h * DIL_HD:DIL_GW + (h + 1) * DIL_HD])
            lse_ref[0, rows, cols] = jnp.broadcast_to(lse, (nk, DIL_HD))


def _dil_prompt(q, kv, dil, bsz, seq):
    sub_len = seq // dil

    def split(a):
        w = a.shape[-1]
        return a.reshape(bsz, sub_len, dil, w).transpose(0, 2, 1, 3).reshape(bsz * dil, sub_len, w)

    def merge(a):
        return a.reshape(bsz, dil, sub_len, DIL_GW).transpose(0, 2, 1, 3).reshape(bsz * seq, DIL_GW)

    blk = lambda b: (b, 0, 0)
    o, lse = pl.pallas_call(
        functools.partial(_dil_band_kernel, sub_len), grid=(bsz * dil,),
        in_specs=[pl.BlockSpec((1, sub_len, DIL_GW), blk), pl.BlockSpec((1, sub_len, 2 * DIL_GW), blk)],
        out_specs=[pl.BlockSpec((1, sub_len, DIL_GW), blk)] * 2,
        out_shape=[jax.ShapeDtypeStruct((bsz * dil, sub_len, DIL_GW), F32)] * 2,
        compiler_params=_cparams(("parallel",)), name="dil_band_d%d" % dil)(split(q), split(kv))
    return merge(o), merge(lse)


def _dil_combine(outs, lses):
    mx = jnp.maximum(jnp.maximum(lses[0], lses[1]), lses[2])
    es = [jnp.exp(l - mx) for l in lses]
    den = es[0] + es[1] + es[2]
    return [o * (e / den) for o, e in zip(outs, es)]


def _dil_out_kernel(o0_ref, o1_ref, o2_ref, l0_ref, l1_ref, l2_ref, w_ref, res_ref, out_ref):
    parts = _dil_combine([o0_ref[...], o1_ref[...], o2_ref[...]], [l0_ref[...], l1_ref[...], l2_ref[...]])
    y = res_ref[...]
    for g in range(N_DIL):
        y = y + _dot(parts[g], w_ref[g])
    out_ref[...] = y


def _dil_out(outs, lses, w, res, tm=256):
    m, d = res.shape
    tm = min(tm, m)
    row = lambda i: (i, 0)
    return pl.pallas_call(
        _dil_out_kernel, grid=(m // tm,),
        in_specs=[pl.BlockSpec((tm, DIL_GW), row)] * 6 + [_const_spec(w.shape), pl.BlockSpec((tm, d), row)],
        out_specs=pl.BlockSpec((tm, d), row),
        out_shape=jax.ShapeDtypeStruct((m, d), F32),
        compiler_params=_cparams(("parallel",)), name="dil_out")(*outs, *lses, w, res)


def _dil_sample_kernel(q0_ref, q1_ref, q2_ref, n0_ref, n1_ref, n2_ref, b0_ref, b1_ref, b2_ref, o_ref):
    outs, lses = [], []
    for (win, dil), q_ref, n_ref, b_ref in zip(DIL_GROUPS, (q0_ref, q1_ref, q2_ref), (n0_ref, n1_ref, n2_ref),
                                               (b0_ref, b1_ref, b2_ref)):
        nk = win // dil
        nchunk = 2 * DIL_GW // LANE
        kv = jnp.concatenate([b_ref[0, pl.ds(c, nk, stride=nchunk * dil), :] for c in range(nchunk)],
                             axis=1).astype(BF16)
        q = q_ref[0].astype(BF16)
        new = n_ref[0].astype(BF16).astype(F32)
        og, lg = [], []
        for h in range(DIL_HEADS):
            qh = q[:, h * DIL_HD:(h + 1) * DIL_HD]
            s = _dot_nt(jnp.broadcast_to(qh, (8, DIL_HD)), kv[:, h * DIL_HD:(h + 1) * DIL_HD])[0:1] * DIL_HD ** -0.5
            s_new = jnp.sum(qh.astype(F32) * new[:, h * DIL_HD:(h + 1) * DIL_HD], axis=-1, keepdims=True) * DIL_HD ** -0.5
            mx = jnp.maximum(jnp.max(s, axis=-1, keepdims=True), s_new)
            lse = mx + jnp.log(jnp.sum(jnp.exp(s - mx), axis=-1, keepdims=True) + jnp.exp(s_new - mx))
            p = jnp.broadcast_to(jnp.exp(s - lse), (8, nk))
            o = _dot(p, kv[:, DIL_GW + h * DIL_HD:DIL_GW + (h + 1) * DIL_HD])[0:1]
            p_new = jnp.exp(s_new - lse).astype(BF16).astype(F32)
            og.append(o + p_new * new[:, DIL_GW + h * DIL_HD:DIL_GW + (h + 1) * DIL_HD])
            lg.append(jnp.broadcast_to(lse, (1, DIL_HD)))
        outs.append(jnp.concatenate(og, axis=1))
        lses.append(jnp.concatenate(lg, axis=1))
    o_ref[0] = jnp.concatenate(_dil_combine(outs, lses), axis=1)


def _dil_sample(qs, news, bufs):
    bsz = qs[0].shape[0]
    blk = lambda b: (b, 0, 0)
    args = ([q.reshape(bsz, 1, DIL_GW) for q in qs] + [n.reshape(bsz, 1, 2 * DIL_GW) for n in news]
            + [b.reshape(bsz, -1, LANE) for b in bufs])
    out = pl.pallas_call(
        _dil_sample_kernel, grid=(bsz,),
        in_specs=[pl.BlockSpec((1,) + a.shape[1:], blk) for a in args],
        out_specs=pl.BlockSpec((1, 1, DIL_WIDTH), blk),
        out_shape=jax.ShapeDtypeStruct((bsz, 1, DIL_WIDTH), F32),
        compiler_params=_cparams(("parallel",)), name="dil_sample")(*args)
    return out.reshape(bsz, DIL_WIDTH)


def _block_diag_pairs(w):
    h, k, n = w.shape
    z = jnp.zeros((h // 2, k, n), w.dtype)
    top = jnp.concatenate([w[0::2], z], axis=2)
    bot = jnp.concatenate([z, w[1::2]], axis=2)
    return jnp.concatenate([top, bot], axis=1)


def _prep_mla(mla_w_in, mla_norm_q, mla_w_uq, mla_norm_kv, mla_w_uk, mla_w_uv, mla_w_o):
    pad = jnp.zeros((D_MODEL, LANE - MLA_ROPE), F32)
    wuq = mla_w_uq.reshape(MLA_Q_LORA, MLA_HEADS, MLA_NOPE + MLA_ROPE)
    wuq = jnp.concatenate([wuq[:, :, :MLA_NOPE].reshape(MLA_Q_LORA, -1), wuq[:, :, MLA_NOPE:].reshape(MLA_Q_LORA, -1)],
                          axis=1)
    return {
        "win": jnp.concatenate([mla_w_in, pad], axis=1).astype(BF16),
        "nq": mla_norm_q.reshape(1, -1), "nkv": mla_norm_kv.reshape(1, -1),
        "wuq": wuq.astype(BF16),
        "wuk": _block_diag_pairs(jnp.transpose(mla_w_uk, (1, 2, 0))).astype(BF16),
        "wuv": _block_diag_pairs(jnp.transpose(mla_w_uv, (1, 0, 2))).astype(BF16),
        "wo": mla_w_o.astype(BF16),
    }


def _prep_conv(conv_w_in, conv_b_in, conv_w_dw, conv_b_dw, conv_ln_g, conv_ln_b, conv_w_out, conv_b_out):
    return {"win": conv_w_in.astype(BF16), "bin": conv_b_in.reshape(1, -1), "wdw": conv_w_dw,
            "bdw": conv_b_dw.reshape(1, -1), "lng": conv_ln_g.reshape(1, -1), "lnb": conv_ln_b.reshape(1, -1),
            "wout": conv_w_out.astype(BF16), "bout": conv_b_out.reshape(1, -1)}


def _prep_ssm(ssm_w_in, ssm_conv_w, ssm_conv_b, ssm_dt_bias):
    w = ssm_w_in.astype(BF16)
    return {"wz": w[:, :D_INNER], "wxbc": w[:, D_INNER:D_INNER + SSM_CONV_DIM], "wdt": w[:, D_INNER + SSM_CONV_DIM:],
            "cw": ssm_conv_w, "cb": ssm_conv_b.reshape(1, -1), "dtb": ssm_dt_bias.reshape(1, -1)}


def kernel(x_prompt, x_sample, cache_mla_latent, cache_mem_kv, state_conv, state_ssm_conv, state_ssm, state_dil_kv0, state_dil_kv1, state_dil_kv2, page_table, mem_prompt, norm_ffn1, ffn1_w_gu, ffn1_w_down, norm_mix, norm_cross, norm_mem, cross_w_q, cross_w_kv, cross_w_o, norm_ffn2, ffn2_w_gu, ffn2_w_down, norm_final, mla_w_in, mla_norm_q, mla_w_uq, mla_norm_kv, mla_w_uk, mla_w_uv, mla_w_o, conv_w_in, conv_b_in, conv_w_dw, conv_b_dw, conv_ln_g, conv_ln_b, conv_w_out, conv_b_out, ssm_w_in, ssm_conv_w, ssm_conv_b, ssm_dt_bias, ssm_a_log, ssm_d, ssm_norm, ssm_w_out, dil_w_qkv, dil_w_o):
    bsz, seq, d = x_prompt.shape
    dec = x_sample.shape[0]
    assert x_sample.shape[1] == 1 and d == D_MODEL
    depth = norm_ffn1.shape[0]
    past_len = page_table.shape[1] * cache_mla_latent.shape[1]
    pos_p = jnp.arange(seq, dtype=jnp.int32)
    pos_s = jnp.full((dec,), past_len, dtype=jnp.int32)

    res_p = x_prompt.reshape(bsz * seq, d)
    res_s = x_sample.reshape(dec, d)
    mem_len = mem_prompt.shape[1]
    mem_kv = _memkv(mem_prompt.reshape(bsz * mem_len, d), norm_mem, cross_w_kv.astype(BF16))
    outs = {}

    for i in range(depth):
        kind = i % 4
        wgu, wd = ffn1_w_gu[i].astype(BF16), ffn1_w_down[i].astype(BF16)
        res_p = _ffn(res_p, norm_ffn1[i], wgu, wd)
        res_s = _ffn(res_s, norm_ffn1[i], wgu, wd)
        g_mix = norm_mix[i]
        if kind == 0:
            w = _prep_mla(mla_w_in, mla_norm_q, mla_w_uq, mla_norm_kv, mla_w_uk, mla_w_uv, mla_w_o)
            cos_p, sin_p = _rope_tables(pos_p, MLA_ROPE)
            cos_s, sin_s = _rope_tables(pos_s, MLA_ROPE)
            q_p, lat_p = _mla_proj(res_p, g_mix, w, cos_p, sin_p)
            o_p = _mla_attn_prompt(q_p, lat_p, bsz, seq)
            res_p = _mla_out(o_p, w["wuv"], w["wo"], res_p)
            q_s, lat_s = _mla_proj(res_s, g_mix, w, cos_s, sin_s)
            o_s = _mla_attn_sample(jnp.transpose(q_s, (1, 0, 2)), lat_s.reshape(dec, 1, MLA_LATENT),
                                   cache_mla_latent, page_table)
            res_s = _mla_out(jnp.transpose(o_s, (1, 0, 2)), w["wuv"], w["wo"], res_s)
            outs["mla_p"] = lat_p.reshape(bsz, seq, MLA_LATENT)
            outs["mla_s"] = lat_s.reshape(dec, 1, MLA_LATENT)
        elif kind == 1:
            w = _prep_conv(conv_w_in, conv_b_in, conv_w_dw, conv_b_dw, conv_ln_g, conv_ln_b, conv_w_out, conv_b_out)
            res_p, outs["conv_p"] = _conv_prompt(res_p, g_mix, w, bsz, seq)
            res_s, u_s = _conv_sample(res_s, g_mix, w, jnp.transpose(state_conv, (1, 0, 2)))
            outs["conv_s"] = jnp.concatenate([state_conv[:, 1:], u_s[:, None, :]], axis=1)
        elif kind == 2:
            w = _prep_ssm(ssm_w_in, ssm_conv_w, ssm_conv_b, ssm_dt_bias)
            wout = ssm_w_out.astype(BF16)
            z_p, act_p, dt_p, outs["ssmc_p"] = _ssm_in_prompt(res_p, g_mix, w, bsz, seq)
            y_p, h_t = _ssd_scan(act_p, dt_p, ssm_a_log, ssm_d, bsz, seq)
            outs["ssm_p"] = jnp.swapaxes(h_t, 2, 3)
            res_p = _ssm_out(y_p, z_p, ssm_norm, wout, res_p)
            z_s, xbc_s, act_s, dt_s = _ssm_in_sample(res_s, g_mix, w, jnp.transpose(state_ssm_conv, (1, 0, 2)))
            outs["ssmc_s"] = jnp.concatenate([state_ssm_conv[:, 1:], xbc_s[:, None, :]], axis=1)
            y_s, outs["ssm_s"] = _ssm_step(act_s, dt_s, ssm_a_log, ssm_d, state_ssm)
            res_s = _ssm_out(y_s, z_s, ssm_norm, wout, res_s)
        else:
            wqkv = dil_w_qkv.astype(BF16)
            wo = dil_w_o.astype(BF16).reshape(N_DIL, DIL_GW, d)
            cos_p, sin_p = _rope_tables(pos_p, DIL_HD)
            cos_s, sin_s = _rope_tables(pos_s, DIL_HD)
            qkv_p = _dil_qkv(res_p, g_mix, wqkv, cos_p, sin_p)
            qkv_s = _dil_qkv(res_s, g_mix, wqkv, cos_s, sin_s)
            bufs = (state_dil_kv0, state_dil_kv1, state_dil_kv2)
            o_list, l_list = [], []
            for g, (win, dil) in enumerate(DIL_GROUPS):
                kv3 = qkv_p[N_DIL + g].reshape(bsz, seq, 2 * DIL_GW)
                o, l = _dil_prompt(qkv_p[g], qkv_p[N_DIL + g], dil, bsz, seq)
                o_list.append(o)
                l_list.append(l)
                keep = min(win, seq)
                outs["dil%d_p" % g] = kv3[:, seq - keep:].reshape(bsz, keep, 2, DIL_HEADS, DIL_HD)
                new = qkv_s[N_DIL + g].reshape(dec, 1, 2, DIL_HEADS, DIL_HD)
                outs["dil%d_s" % g] = jnp.concatenate([bufs[g][:, 1:], new], axis=1)
            res_p = _dil_out(o_list, l_list, wo, res_p)
            o_s = _dil_sample(qkv_s[:N_DIL], qkv_s[N_DIL:], [b.reshape(dec, b.shape[1], 2 * DIL_GW) for b in bufs])
            res_s = _proj_res([o_s], [dil_w_o.astype(BF16)], res_s, name="dil_out_sample")

        wq, wo_x = cross_w_q[i].astype(BF16), cross_w_o[i].astype(BF16)
        res_p = _xattn_prompt(res_p, norm_cross[i], wq, mem_kv[i].reshape(bsz, mem_len, 2 * d), wo_x, seq)
        (q_s,) = _norm_proj(res_s, norm_cross[i], [wq], name="xattn_q_sample")
        a_s = _xattn_sample(q_s, cache_mem_kv[i].reshape(dec, mem_len, 2 * d))
        res_s = _proj_res([a_s], [wo_x], res_s, name="xattn_o_sample")

        wgu, wd = ffn2_w_gu[i].astype(BF16), ffn2_w_down[i].astype(BF16)
        final_g = norm_final if i == depth - 1 else None
        res_p = _ffn(res_p, norm_ffn2[i], wgu, wd, final_g)
        res_s = _ffn(res_s, norm_ffn2[i], wgu, wd, final_g)

    return (res_p.reshape(bsz, seq, d), res_s.reshape(dec, 1, d),
            outs["mla_p"], outs["mla_s"], outs["conv_p"], outs["conv_s"], outs["ssmc_p"], outs["ssmc_s"],
            outs["ssm_p"], outs["ssm_s"],
            outs["dil0_p"], outs["dil0_s"], outs["dil1_p"], outs["dil1_s"], outs["dil2_p"], outs["dil2_s"],
            mem_kv.reshape(depth, bsz, mem_len, 2, MEM_HEADS, MEM_HD))
```

```python
import functools
import math

import jax
import jax.numpy as jnp
from jax import lax
from jax.experimental import pallas as pl
from jax.experimental.pallas import tpu as pltpu

F32 = jnp.float32
BF16 = jnp.bfloat16

D_MODEL = 1024
D_FF = 2816
RMS_EPS = 1e-6
LN_EPS = 1e-5
ROPE_THETA = 10000.0

MLA_HEADS = 16
MLA_Q_LORA = 256
MLA_KV_LORA = 128
MLA_NOPE = 64
MLA_ROPE = 32
MLA_V_HD = 64
MLA_LATENT = MLA_KV_LORA + MLA_ROPE
MLA_SCALE = (MLA_NOPE + MLA_ROPE) ** -0.5

CONV_WIDTH = 31
CONV_HALO = 32

D_INNER = 2 * D_MODEL
SSM_HD = 64
SSM_HEADS = D_INNER // SSM_HD
SSM_GROUPS = 4
SSM_REP = SSM_HEADS // SSM_GROUPS
SSM_STATE = 128
SSM_CONV = 4
SSM_CHUNK = 128
SSM_GN = SSM_GROUPS * SSM_STATE
SSM_CONV_DIM = D_INNER + 2 * SSM_GN
SSM_HALO = 8

DIL_GROUPS = ((128, 1), (512, 4), (2048, 16))
N_DIL = 3
DIL_HEADS = 5
DIL_HD = 64
DIL_GW = DIL_HEADS * DIL_HD
DIL_WIDTH = N_DIL * DIL_GW
DIL_NK = 128

MEM_HEADS = 4
MEM_HD = D_MODEL // MEM_HEADS

LANE = 128
NEG_INF = float("-inf")


def _cparams(sem, vmem_mb=48):
    return pltpu.CompilerParams(dimension_semantics=sem, vmem_limit_bytes=vmem_mb * 1024 * 1024)


def _const_spec(shape):
    nd = len(shape)
    return pl.BlockSpec(shape, lambda *_: (0,) * nd, pipeline_mode=pl.Buffered(1))


def _dot(a, b):
    return jnp.dot(a.astype(BF16), b.astype(BF16), preferred_element_type=F32)


def _dot_nt(a, b):
    return lax.dot_general(a.astype(BF16), b.astype(BF16), (((1,), (1,)), ((), ())),
                           preferred_element_type=F32)


def _dot_exact(a, b):
    return jnp.dot(a, b, preferred_element_type=F32, precision=lax.Precision.HIGHEST)


def _dot_nt_exact(a, b):
    return lax.dot_general(a, b, (((1,), (1,)), ((), ())), preferred_element_type=F32,
                           precision=lax.Precision.HIGHEST)


def _rms(x, g):
    return x * lax.rsqrt(jnp.mean(x * x, axis=-1, keepdims=True) + RMS_EPS) * g


def _silu(x):
    return x * jax.nn.sigmoid(x)


def _rot_half(t, dim):
    half = dim // 2
    lane = lax.broadcasted_iota(jnp.int32, t.shape, 1)
    return jnp.where((lane & (dim - 1)) < half, pltpu.roll(t, LANE - half, 1), pltpu.roll(t, half, 1))


def _rope_tables(pos, dim):
    half = dim // 2
    inv_freq = jnp.exp(-math.log(ROPE_THETA) * (2.0 / dim) * jnp.arange(half, dtype=F32))
    ang = pos.astype(F32)[:, None] * inv_freq[None, :]
    cos, sin = jnp.cos(ang), jnp.sin(ang)
    reps = LANE // dim
    return (jnp.tile(jnp.concatenate([cos, cos], axis=1), (1, reps)),
            jnp.tile(jnp.concatenate([-sin, sin], axis=1), (1, reps)))


def _softmax_rows(s):
    m = jnp.max(s, axis=-1, keepdims=True)
    p = jnp.exp(s - m)
    return p / jnp.sum(p, axis=-1, keepdims=True)


def _cast_kernel(x_ref, o_ref):
    o_ref[...] = x_ref[...].astype(BF16)


def _cast_bf16(w, rows=256):
    shape = w.shape
    w2 = w.reshape(-1, shape[-1])
    m, n = w2.shape
    rows = min(rows, m)
    out = pl.pallas_call(
        _cast_kernel, grid=(pl.cdiv(m, rows),),
        in_specs=[pl.BlockSpec((rows, n), lambda i: (i, 0))],
        out_specs=pl.BlockSpec((rows, n), lambda i: (i, 0)),
        out_shape=jax.ShapeDtypeStruct((m, n), BF16),
        compiler_params=_cparams(("parallel",)), name="cast_bf16")(w2)
    return out.reshape(shape)


def _layer_spec(shape, layer):
    nd = len(shape)
    return pl.BlockSpec((1,) + tuple(shape[1:]), lambda *_: (layer,) + (0,) * (nd - 1), pipeline_mode=pl.Buffered(1))


def _ffn_kernel(has_final, x_ref, g_ref, wgu_ref, wd_ref, *rest):
    o_ref = rest[-1]
    x = x_ref[...]
    h = _dot(_rms(x, g_ref[0]), wgu_ref[0])
    act = _silu(h[:, :D_FF]) * h[:, D_FF:]
    y = x + 0.5 * _dot(act, wd_ref[0])
    if has_final:
        y = _rms(y, rest[0][...])
    o_ref[...] = y


def _ffn(x, g, wgu, wd, layer, final_g=None, tm=256):
    m, d = x.shape
    tm = min(tm, m)
    g = g.reshape(g.shape[0], 1, d)
    args = [x, g, wgu, wd]
    in_specs = [pl.BlockSpec((tm, d), lambda i: (i, 0)), _layer_spec(g.shape, layer),
                _layer_spec(wgu.shape, layer), _layer_spec(wd.shape, layer)]
    if final_g is not None:
        args.append(final_g.reshape(1, d))
        in_specs.append(_const_spec((1, d)))
    return pl.pallas_call(
        functools.partial(_ffn_kernel, final_g is not None),
        grid=(m // tm,), in_specs=in_specs,
        out_specs=pl.BlockSpec((tm, d), lambda i: (i, 0)),
        out_shape=jax.ShapeDtypeStruct((m, d), F32),
        compiler_params=_cparams(("parallel",), 56), name="ffn")(*args)


def _norm_proj_kernel(n, has_bias, x_ref, g_ref, *refs):
    xn = _rms(x_ref[...], g_ref[...]).astype(BF16)
    for i in range(n):
        y = jnp.dot(xn, refs[i][...], preferred_element_type=F32)
        if has_bias:
            y = y + refs[n + i][...]
        refs[-n + i][...] = y


def _norm_proj(x, g, ws, bs=None, tm=256, name="norm_proj"):
    m, d = x.shape
    tm = min(tm, m)
    n = len(ws)
    args = [x, g.reshape(1, d)] + list(ws)
    in_specs = [pl.BlockSpec((tm, d), lambda i: (i, 0)), _const_spec((1, d))] + [_const_spec(w.shape) for w in ws]
    if bs is not None:
        args += [b.reshape(1, -1) for b in bs]
        in_specs += [_const_spec((1, b.shape[-1])) for b in bs]
    outs = pl.pallas_call(
        functools.partial(_norm_proj_kernel, n, bs is not None),
        grid=(m // tm,), in_specs=in_specs,
        out_specs=[pl.BlockSpec((tm, w.shape[1]), lambda i: (i, 0)) for w in ws],
        out_shape=[jax.ShapeDtypeStruct((m, w.shape[1]), F32) for w in ws],
        compiler_params=_cparams(("parallel",)), name=name)(*args)
    return outs


def _proj_res_kernel(n, has_bias, *refs):
    a_refs, w_refs = refs[:n], refs[n:2 * n]
    res_ref, o_ref = refs[2 * n], refs[-1]
    y = res_ref[...]
    for a_ref, w_ref in zip(a_refs, w_refs):
        y = y + _dot(a_ref[...], w_ref[...])
    if has_bias:
        y = y + refs[2 * n + 1][...]
    o_ref[...] = y


def _proj_res(a_list, w_list, res, bias=None, tm=256, name="proj_res"):
    m, d = res.shape
    tm = min(tm, m)
    n = len(a_list)
    args = list(a_list) + list(w_list) + [res]
    in_specs = ([pl.BlockSpec((tm, a.shape[1]), lambda i: (i, 0)) for a in a_list]
                + [_const_spec(w.shape) for w in w_list] + [pl.BlockSpec((tm, d), lambda i: (i, 0))])
    if bias is not None:
        args.append(bias.reshape(1, d))
        in_specs.append(_const_spec((1, d)))
    return pl.pallas_call(
        functools.partial(_proj_res_kernel, n, bias is not None),
        grid=(m // tm,), in_specs=in_specs,
        out_specs=pl.BlockSpec((tm, d), lambda i: (i, 0)),
        out_shape=jax.ShapeDtypeStruct((m, d), F32),
        compiler_params=_cparams(("parallel",)), name=name)(*args)


def _memkv_kernel(x_ref, g_ref, w_ref, o_ref):
    o_ref[0] = _dot(_rms(x_ref[...], g_ref[0]), w_ref[0])


def _memkv(mem2d, norm_mem, w_kv, tm=256):
    m, d = mem2d.shape
    depth = w_kv.shape[0]
    n = w_kv.shape[2]
    return pl.pallas_call(
        _memkv_kernel, grid=(depth, m // tm),
        in_specs=[pl.BlockSpec((tm, d), lambda l, i: (i, 0)),
                  pl.BlockSpec((1, 1, d), lambda l, i: (l, 0, 0)),
                  pl.BlockSpec((1, d, n), lambda l, i: (l, 0, 0))],
        out_specs=pl.BlockSpec((1, tm, n), lambda l, i: (l, i, 0)),
        out_shape=jax.ShapeDtypeStruct((depth, m, n), F32),
        compiler_params=_cparams(("parallel", "parallel")), name="memkv")(mem2d, norm_mem.reshape(depth, 1, d), w_kv)


def _xattn_core(q, kv):
    kvb = kv.astype(BF16)
    outs = []
    for h in range(MEM_HEADS):
        kh = kvb[:, h * MEM_HD:(h + 1) * MEM_HD]
        vh = kvb[:, D_MODEL + h * MEM_HD:D_MODEL + (h + 1) * MEM_HD]
        s = _dot_nt(q[:, h * MEM_HD:(h + 1) * MEM_HD], kh) * MEM_HD ** -0.5
        outs.append(_dot(_softmax_rows(s), vh))
    return jnp.concatenate(outs, axis=1)


def _xattn_prompt_kernel(x_ref, g_ref, wq_ref, kv_ref, wo_ref, o_ref):
    x = x_ref[...]
    q = _dot(_rms(x, g_ref[0]), wq_ref[0])
    o_ref[...] = x + _dot(_xattn_core(q, kv_ref[0]), wo_ref[0])


def _xattn_prompt(x, g, wq, kv, wo, layer, bsz, seq, tq=256):
    m, d = x.shape
    nq = seq // tq
    _, ml, kvw = kv.shape
    g = g.reshape(g.shape[0], 1, d)
    return pl.pallas_call(
        _xattn_prompt_kernel, grid=(bsz, nq),
        in_specs=[pl.BlockSpec((tq, d), lambda b, i: (b * nq + i, 0)), _layer_spec(g.shape, layer),
                  _layer_spec(wq.shape, layer),
                  pl.BlockSpec((1, ml, kvw), lambda b, i: (layer * bsz + b, 0, 0)), _layer_spec(wo.shape, layer)],
        out_specs=pl.BlockSpec((tq, d), lambda b, i: (b * nq + i, 0)),
        out_shape=jax.ShapeDtypeStruct((m, d), F32),
        compiler_params=_cparams(("parallel", "parallel")), name="xattn_prompt")(x, g, wq, kv, wo)


def _xattn_sample_kernel(q_ref, kv_ref, o_ref):
    q = jnp.broadcast_to(q_ref[0], (8, D_MODEL))
    outs = []
    for h in range(MEM_HEADS):
        kh = kv_ref[0, 0, :, 0, h, :]
        vh = kv_ref[0, 0, :, 1, h, :]
        s = _dot_nt(q[:, h * MEM_HD:(h + 1) * MEM_HD], kh) * MEM_HD ** -0.5
        outs.append(_dot(_softmax_rows(s), vh))
    o_ref[0] = jnp.concatenate(outs, axis=1)[0:1]


def _xattn_sample(q, cache, layer):
    bsz, d = q.shape
    out = pl.pallas_call(
        _xattn_sample_kernel, grid=(bsz,),
        in_specs=[pl.BlockSpec((1, 1, d), lambda b: (b, 0, 0)),
                  pl.BlockSpec((1, 1) + cache.shape[2:], lambda b: (layer, b, 0, 0, 0, 0))],
        out_specs=pl.BlockSpec((1, 1, d), lambda b: (b, 0, 0)),
        out_shape=jax.ShapeDtypeStruct((bsz, 1, d), F32),
        compiler_params=_cparams(("parallel",)), name="xattn_sample")(q.reshape(bsz, 1, d), cache)
    return out.reshape(bsz, d)


def _mla_proj_kernel(x_ref, g_ref, win_ref, nq_ref, wuq_ref, nkv_ref, wuk_ref, cos_ref, sin_ref, q_ref, lat_ref):
    cos, sin = cos_ref[...], sin_ref[...]
    a = _dot(_rms(x_ref[...], g_ref[...]), win_ref[...])
    ckv = _rms(a[:, MLA_Q_LORA:MLA_Q_LORA + MLA_KV_LORA], nkv_ref[...])
    pe = a[:, MLA_Q_LORA + MLA_KV_LORA:]
    kpe = pe * cos + _rot_half(pe, MLA_ROPE) * sin
    lat_ref[:, :MLA_KV_LORA] = ckv
    lat_ref[:, MLA_KV_LORA:] = kpe[:, :MLA_ROPE]
    q = _dot(_rms(a[:, :MLA_Q_LORA], nq_ref[...]), wuq_ref[...])
    n_nope = MLA_HEADS * MLA_NOPE
    for p in range(MLA_HEADS // 2):
        ql = _dot(q[:, p * LANE:(p + 1) * LANE], wuk_ref[p])
        q_ref[2 * p, :, :MLA_KV_LORA] = ql[:, :MLA_KV_LORA]
        q_ref[2 * p + 1, :, :MLA_KV_LORA] = ql[:, MLA_KV_LORA:]
    per_tile = LANE // MLA_ROPE
    for t in range(MLA_HEADS // per_tile):
        tile = q[:, n_nope + t * LANE:n_nope + (t + 1) * LANE]
        qpe = tile * cos + _rot_half(tile, MLA_ROPE) * sin
        for j in range(per_tile):
            q_ref[t * per_tile + j, :, MLA_KV_LORA:] = qpe[:, j * MLA_ROPE:(j + 1) * MLA_ROPE]


def _mla_proj(x, g, w, cos, sin, tm=256):
    m, d = x.shape
    tm = min(tm, m)
    nt = cos.shape[0] // tm
    consts = [g.reshape(1, d), w["win"], w["nq"], w["wuq"], w["nkv"], w["wuk"]]
    return pl.pallas_call(
        _mla_proj_kernel, grid=(m // tm,),
        in_specs=([pl.BlockSpec((tm, d), lambda i: (i, 0))] + [_const_spec(c.shape) for c in consts]
                  + [pl.BlockSpec((tm, LANE), lambda i: (i % nt, 0))] * 2),
        out_specs=[pl.BlockSpec((MLA_HEADS, tm, MLA_LATENT), lambda i: (0, i, 0)),
                   pl.BlockSpec((tm, MLA_LATENT), lambda i: (i, 0))],
        out_shape=[jax.ShapeDtypeStruct((MLA_HEADS, m, MLA_LATENT), F32),
                   jax.ShapeDtypeStruct((m, MLA_LATENT), F32)],
        compiler_params=_cparams(("parallel",)), name="mla_proj")(x, *consts, cos, sin)


def _mla_attn_kernel(tq, tk, q_ref, lat_ref, o_ref):
    i = pl.program_id(1)
    rows = MLA_HEADS * tq
    q = q_ref[...].reshape(rows, MLA_LATENT).astype(BF16)
    row = lax.broadcasted_iota(jnp.int32, (rows, tk), 0)
    col = lax.broadcasted_iota(jnp.int32, (rows, tk), 1)
    rel = col - (row & (tq - 1))
    n_chunks = (i * tq + tq + tk - 1) // tk

    def body(c, carry):
        m, l, acc = carry
        keys = lat_ref[pl.ds(pl.multiple_of(c * tk, tk), tk), :].astype(BF16)
        s = _dot_nt(q, keys) * MLA_SCALE
        s = jnp.where(rel <= i * tq - c * tk, s, NEG_INF)
        m_new = jnp.maximum(m, jnp.max(s, axis=-1, keepdims=True))
        alpha = jnp.exp(m - m_new)
        p = jnp.exp(s - m_new)
        l = alpha * l + jnp.sum(p, axis=-1, keepdims=True)
        acc = alpha * acc + _dot(p, keys[:, :MLA_KV_LORA])
        return m_new, l, acc

    init = (jnp.full((rows, 1), NEG_INF, F32), jnp.zeros((rows, 1), F32), jnp.zeros((rows, MLA_KV_LORA), F32))
    _, l, acc = lax.fori_loop(0, n_chunks, body, init)
    o_ref[...] = (acc / l).reshape(MLA_HEADS, tq, MLA_KV_LORA)


def _mla_attn_prompt(qabs, lat, bsz, seq, tq=128, tk=512):
    m = bsz * seq
    nq = seq // tq
    return pl.pallas_call(
        functools.partial(_mla_attn_kernel, tq, tk), grid=(bsz, nq),
        in_specs=[pl.BlockSpec((MLA_HEADS, tq, MLA_LATENT), lambda b, i: (0, b * nq + i, 0)),
                  pl.BlockSpec((seq, MLA_LATENT), lambda b, i: (b, 0))],
        out_specs=pl.BlockSpec((MLA_HEADS, tq, MLA_KV_LORA), lambda b, i: (0, b * nq + i, 0)),
        out_shape=jax.ShapeDtypeStruct((MLA_HEADS, m, MLA_KV_LORA), F32),
        compiler_params=_cparams(("parallel", "parallel")), name="mla_attn")(qabs, lat)


def _mla_decode_kernel(pages_per_chunk, n_pages, page, pt_ref, q_ref, new_ref, cache_ref, o_ref, buf, sem):
    b = pl.program_id(0)
    n_chunks = n_pages // pages_per_chunk

    def copies(c, slot):
        return [pltpu.make_async_copy(cache_ref.at[pt_ref[b, c * pages_per_chunk + j]],
                                      buf.at[slot, :, pl.ds(j * page, page)], sem.at[slot])
                for j in range(pages_per_chunk)]

    for cp in copies(0, 0):
        cp.start()
    q = q_ref[0].astype(BF16)

    def body(c, carry):
        m, l, acc = carry
        slot = c & 1

        @pl.when(c + 1 < n_chunks)
        def _():
            for cp in copies(c + 1, 1 - slot):
                cp.start()

        for cp in copies(c, slot):
            cp.wait()
        keys_t = buf[slot].astype(BF16)
        s = _dot(q, keys_t) * MLA_SCALE
        m_new = jnp.maximum(m, jnp.max(s, axis=-1, keepdims=True))
        alpha = jnp.exp(m - m_new)
        p = jnp.exp(s - m_new)
        l = alpha * l + jnp.sum(p, axis=-1, keepdims=True)
        acc = alpha * acc + _dot_nt(p, keys_t[:MLA_KV_LORA, :])
        return m_new, l, acc

    init = (jnp.full((MLA_HEADS, 1), NEG_INF, F32), jnp.zeros((MLA_HEADS, 1), F32),
            jnp.zeros((MLA_HEADS, MLA_KV_LORA), F32))
    m, l, acc = lax.fori_loop(0, n_chunks, body, init)
    new = new_ref[0].astype(BF16).astype(F32)
    s_new = jnp.sum(q.astype(F32) * new, axis=-1, keepdims=True) * MLA_SCALE
    m_new = jnp.maximum(m, s_new)
    alpha = jnp.exp(m - m_new)
    p_new = jnp.exp(s_new - m_new)
    l = alpha * l + p_new
    acc = alpha * acc + p_new.astype(BF16).astype(F32) * new[:, :MLA_KV_LORA]
    o_ref[0] = acc / l


def _mla_attn_sample(q, new_rows, cache_t, page_table, pages_per_chunk=8):
    bsz = q.shape[0]
    n_pages = page_table.shape[1]
    page = cache_t.shape[2]
    grid_spec = pltpu.PrefetchScalarGridSpec(
        num_scalar_prefetch=1, grid=(bsz,),
        in_specs=[pl.BlockSpec((1, MLA_HEADS, MLA_LATENT), lambda b, pt: (b, 0, 0)),
                  pl.BlockSpec((1, 1, MLA_LATENT), lambda b, pt: (b, 0, 0)),
                  pl.BlockSpec(memory_space=pl.ANY)],
        out_specs=pl.BlockSpec((1, MLA_HEADS, MLA_KV_LORA), lambda b, pt: (b, 0, 0)),
        scratch_shapes=[pltpu.VMEM((2, MLA_LATENT, pages_per_chunk * page), F32), pltpu.SemaphoreType.DMA((2,))])
    return pl.pallas_call(
        functools.partial(_mla_decode_kernel, pages_per_chunk, n_pages, page), grid_spec=grid_spec,
        out_shape=jax.ShapeDtypeStruct((bsz, MLA_HEADS, MLA_KV_LORA), F32),
        compiler_params=_cparams(("arbitrary",)), name="mla_decode")(page_table, q, new_rows, cache_t)


def _mla_out_kernel(o_ref, wuv_ref, wo_ref, res_ref, out_ref):
    parts = []
    for p in range(MLA_HEADS // 2):
        pair = jnp.concatenate([o_ref[2 * p], o_ref[2 * p + 1]], axis=1)
        parts.append(_dot(pair, wuv_ref[p]))
    out_ref[...] = res_ref[...] + _dot(jnp.concatenate(parts, axis=1), wo_ref[...])


def _mla_out(o_lat, wuv, wo, res, tm=256):
    m, d = res.shape
    tm = min(tm, m)
    return pl.pallas_call(
        _mla_out_kernel, grid=(m // tm,),
        in_specs=[pl.BlockSpec((MLA_HEADS, tm, MLA_KV_LORA), lambda i: (0, i, 0)), _const_spec(wuv.shape),
                  _const_spec(wo.shape), pl.BlockSpec((tm, d), lambda i: (i, 0))],
        out_specs=pl.BlockSpec((tm, d), lambda i: (i, 0)),
        out_shape=jax.ShapeDtypeStruct((m, d), F32),
        compiler_params=_cparams(("parallel",)), name="mla_out")(o_lat, wuv, wo, res)


def _conv_tail(y, lng, lnb, wout, bout):
    mu = jnp.mean(y, axis=-1, keepdims=True)
    var = jnp.mean(jnp.square(y - mu), axis=-1, keepdims=True)
    yn = (y - mu) * lax.rsqrt(var + LN_EPS) * lng + lnb
    return _dot(_silu(yn), wout) + bout


def _conv_prompt_kernel(ts, x_ref, g_ref, win_ref, bin_ref, wdw_ref, bdw_ref, lng_ref, lnb_ref, wout_ref, bout_ref,
                        o_ref, st_ref, ext):
    j = pl.program_id(1)

    @pl.when(j == 0)
    def _():
        ext[0:CONV_HALO, :] = jnp.zeros((CONV_HALO, D_MODEL), F32)

    x = x_ref[...]
    a = _dot(_rms(x, g_ref[...]), win_ref[...]) + bin_ref[...]
    ext[CONV_HALO:CONV_HALO + ts, :] = a[:, :D_MODEL] * jax.nn.sigmoid(a[:, D_MODEL:])
    base = CONV_HALO - (CONV_WIDTH - 1)
    y = jnp.zeros((ts, D_MODEL), F32)
    for k in range(CONV_WIDTH):
        y = y + wdw_ref[k:k + 1, :] * ext[base + k:base + k + ts, :]
    y = y + bdw_ref[...]
    o_ref[...] = x + _conv_tail(y, lng_ref[...], lnb_ref[...], wout_ref[...], bout_ref[...])

    @pl.when(j == pl.num_programs(1) - 1)
    def _():
        st_ref[0] = ext[CONV_HALO + ts - (CONV_WIDTH - 1):CONV_HALO + ts, :]

    ext[0:CONV_HALO, :] = ext[ts:ts + CONV_HALO, :]


def _conv_prompt(x, g, w, bsz, seq, ts=256):
    m, d = x.shape
    ns = seq // ts
    consts = [g.reshape(1, d), w["win"], w["bin"], w["wdw"], w["bdw"], w["lng"], w["lnb"], w["wout"], w["bout"]]
    return pl.pallas_call(
        functools.partial(_conv_prompt_kernel, ts), grid=(bsz, ns),
        in_specs=[pl.BlockSpec((ts, d), lambda b, j: (b * ns + j, 0))] + [_const_spec(c.shape) for c in consts],
        out_specs=[pl.BlockSpec((ts, d), lambda b, j: (b * ns + j, 0)),
                   pl.BlockSpec((1, CONV_WIDTH - 1, d), lambda b, j: (b, 0, 0))],
        out_shape=[jax.ShapeDtypeStruct((m, d), F32), jax.ShapeDtypeStruct((bsz, CONV_WIDTH - 1, d), F32)],
        scratch_shapes=[pltpu.VMEM((CONV_HALO + ts, d), F32)],
        compiler_params=_cparams(("parallel", "arbitrary")), name="conv_prompt")(x, *consts)


def _conv_sample_kernel(x_ref, g_ref, win_ref, bin_ref, st_ref, wdw_ref, bdw_ref, lng_ref, lnb_ref, wout_ref,
                        bout_ref, o_ref, u_ref):
    x = x_ref[...]
    a = _dot(_rms(x, g_ref[...]), win_ref[...]) + bin_ref[...]
    u = a[:, :D_MODEL] * jax.nn.sigmoid(a[:, D_MODEL:])
    u_ref[...] = u
    y = jnp.zeros_like(u)
    for k in range(CONV_WIDTH - 1):
        y = y + wdw_ref[k:k + 1, :] * st_ref[k]
    y = y + wdw_ref[CONV_WIDTH - 1:CONV_WIDTH, :] * u + bdw_ref[...]
    o_ref[...] = x + _conv_tail(y, lng_ref[...], lnb_ref[...], wout_ref[...], bout_ref[...])


def _conv_sample(x, g, w, state_t):
    m, d = x.shape
    args = [x, g.reshape(1, d), w["win"], w["bin"], state_t, w["wdw"], w["bdw"], w["lng"], w["lnb"], w["wout"],
            w["bout"]]
    return pl.pallas_call(
        _conv_sample_kernel, grid=(1,),
        in_specs=[_const_spec(a.shape) for a in args],
        out_specs=[_const_spec((m, d)), _const_spec((m, d))],
        out_shape=[jax.ShapeDtypeStruct((m, d), F32), jax.ShapeDtypeStruct((m, d), F32)],
        compiler_params=_cparams(("arbitrary",)), name="conv_sample")(*args)


def _ssm_in_prompt_kernel(ts, x_ref, g_ref, wz_ref, wxbc_ref, wdt_ref, cw_ref, cb_ref, dtb_ref,
                          z_ref, act_ref, dt_ref, cst_ref, ext):
    j = pl.program_id(1)

    @pl.when(j == 0)
    def _():
        ext[0:SSM_HALO, :] = jnp.zeros((SSM_HALO, SSM_CONV_DIM), F32)

    xn = _rms(x_ref[...], g_ref[...]).astype(BF16)
    z_ref[...] = jnp.dot(xn, wz_ref[...], preferred_element_type=F32)
    dt_ref[...] = jax.nn.softplus(jnp.dot(xn, wdt_ref[...], preferred_element_type=F32) + dtb_ref[...])
    ext[SSM_HALO:SSM_HALO + ts, :] = jnp.dot(xn, wxbc_ref[...], preferred_element_type=F32)
    base = SSM_HALO - (SSM_CONV - 1)
    y = jnp.zeros((ts, SSM_CONV_DIM), F32)
    for k in range(SSM_CONV):
        y = y + cw_ref[k:k + 1, :] * ext[base + k:base + k + ts, :]
    act_ref[...] = _silu(y + cb_ref[...])

    @pl.when(j == pl.num_programs(1) - 1)
    def _():
        cst_ref[0] = ext[SSM_HALO + ts - (SSM_CONV - 1):SSM_HALO + ts, :]

    ext[0:SSM_HALO, :] = ext[ts:ts + SSM_HALO, :]


def _ssm_in_prompt(x, g, w, bsz, seq, ts=256):
    m, d = x.shape
    ns = seq // ts
    consts = [g.reshape(1, d), w["wz"], w["wxbc"], w["wdt"], w["cw"], w["cb"], w["dtb"]]
    row = lambda b, j: (b * ns + j, 0)
    return pl.pallas_call(
        functools.partial(_ssm_in_prompt_kernel, ts), grid=(bsz, ns),
        in_specs=[pl.BlockSpec((ts, d), row)] + [_const_spec(c.shape) for c in consts],
        out_specs=[pl.BlockSpec((ts, D_INNER), row), pl.BlockSpec((ts, SSM_CONV_DIM), row),
                   pl.BlockSpec((ts, SSM_HEADS), row),
                   pl.BlockSpec((1, SSM_CONV - 1, SSM_CONV_DIM), lambda b, j: (b, 0, 0))],
        out_shape=[jax.ShapeDtypeStruct((m, D_INNER), F32), jax.ShapeDtypeStruct((m, SSM_CONV_DIM), F32),
                   jax.ShapeDtypeStruct((m, SSM_HEADS), F32),
                   jax.ShapeDtypeStruct((bsz, SSM_CONV - 1, SSM_CONV_DIM), F32)],
        scratch_shapes=[pltpu.VMEM((SSM_HALO + ts, SSM_CONV_DIM), F32)],
        compiler_params=_cparams(("parallel", "arbitrary")), name="ssm_in_prompt")(x, *consts)


def _ssd_scan_kernel(act_ref, dt_ref, alog_ref, d_ref, y_ref, h_ref):
    c = pl.program_id(1)
    L = SSM_CHUNK

    @pl.when(c == 0)
    def _():
        h_ref[...] = jnp.zeros(h_ref.shape, F32)

    dt = dt_ref[...]
    da = dt * (-jnp.exp(alog_ref[...]))
    ri = lax.broadcasted_iota(jnp.int32, (L, L), 0)
    ci = lax.broadcasted_iota(jnp.int32, (L, L), 1)
    tril = ri >= ci
    cs = _dot_exact(tril.astype(F32), da)
    eye = (lax.broadcasted_iota(jnp.int32, (SSM_HEADS, SSM_HEADS), 0)
           == lax.broadcasted_iota(jnp.int32, (SSM_HEADS, SSM_HEADS), 1)).astype(F32)
    cs_t = _dot_nt_exact(eye, cs)
    dt_t = _dot_nt_exact(eye, dt)
    w_t = dt_t * jnp.exp(cs_t[:, L - 1:L] - cs_t)
    cdec = jnp.exp(cs[L - 1:L, :])
    dvec = d_ref[...]
    for g in range(SSM_GROUPS):
        bg = act_ref[:, D_INNER + g * SSM_STATE:D_INNER + (g + 1) * SSM_STATE]
        cg = act_ref[:, D_INNER + SSM_GN + g * SSM_STATE:D_INNER + SSM_GN + (g + 1) * SSM_STATE]
        cb = _dot_nt(cg, bg)
        bg_t = bg.T
        for r8 in range(SSM_REP):
            r = g * SSM_REP + r8
            xr = act_ref[:, r * SSM_HD:(r + 1) * SSM_HD]
            h_t = h_ref[0, r]
            csb = jnp.broadcast_to(cs[:, r:r + 1], (L, L))
            lmat = jnp.exp(jnp.where(tril, csb - cs_t[r:r + 1, :], NEG_INF)) * (cb * dt_t[r:r + 1, :])
            lhs = jnp.concatenate([lmat, cg * jnp.exp(csb)], axis=1)
            rhs = jnp.concatenate([xr, h_t], axis=0)
            y_ref[:, r * SSM_HD:(r + 1) * SSM_HD] = _dot(lhs, rhs) + xr * dvec[:, r:r + 1]
            h_ref[0, r] = cdec[:, r:r + 1] * h_t + _dot(bg_t * w_t[r:r + 1, :], xr)


def _ssd_scan(act, dt, alog, dvec, bsz, seq):
    m = act.shape[0]
    nc = seq // SSM_CHUNK
    row = lambda b, c: (b * nc + c, 0)
    return pl.pallas_call(
        _ssd_scan_kernel, grid=(bsz, nc),
        in_specs=[pl.BlockSpec((SSM_CHUNK, SSM_CONV_DIM), row), pl.BlockSpec((SSM_CHUNK, SSM_HEADS), row),
                  _const_spec((1, SSM_HEADS)), _const_spec((1, SSM_HEADS))],
        out_specs=[pl.BlockSpec((SSM_CHUNK, D_INNER), row),
                   pl.BlockSpec((1, SSM_HEADS, SSM_STATE, SSM_HD), lambda b, c: (b, 0, 0, 0))],
        out_shape=[jax.ShapeDtypeStruct((m, D_INNER), F32),
                   jax.ShapeDtypeStruct((bsz, SSM_HEADS, SSM_STATE, SSM_HD), F32)],
        compiler_params=_cparams(("parallel", "arbitrary")), name="ssd_scan")(
            act, dt, alog.reshape(1, SSM_HEADS), dvec.reshape(1, SSM_HEADS))


def _ssm_out_kernel(y_ref, z_ref, n_ref, w_ref, res_ref, o_ref):
    yz = y_ref[...] * _silu(z_ref[...])
    gw = D_INNER // SSM_GROUPS
    parts = []
    for g in range(SSM_GROUPS):
        seg = yz[:, g * gw:(g + 1) * gw]
        parts.append(seg * lax.rsqrt(jnp.mean(seg * seg, axis=-1, keepdims=True) + RMS_EPS))
    o_ref[...] = res_ref[...] + _dot(jnp.concatenate(parts, axis=1) * n_ref[...], w_ref[...])


def _ssm_out(y, z, norm, wout, res, tm=256):
    m, d = res.shape
    tm = min(tm, m)
    row = lambda i: (i, 0)
    return pl.pallas_call(
        _ssm_out_kernel, grid=(m // tm,),
        in_specs=[pl.BlockSpec((tm, D_INNER), row), pl.BlockSpec((tm, D_INNER), row), _const_spec((1, D_INNER)),
                  _const_spec(wout.shape), pl.BlockSpec((tm, d), row)],
        out_specs=pl.BlockSpec((tm, d), row),
        out_shape=jax.ShapeDtypeStruct((m, d), F32),
        compiler_params=_cparams(("parallel",)), name="ssm_out")(y, z, norm.reshape(1, D_INNER), wout, res)


def _ssm_in_sample_kernel(x_ref, g_ref, wz_ref, wxbc_ref, wdt_ref, cw_ref, cb_ref, dtb_ref, st_ref,
                          z_ref, xbc_ref, act_ref, dt_ref):
    xn = _rms(x_ref[...], g_ref[...]).astype(BF16)
    z_ref[...] = jnp.dot(xn, wz_ref[...], preferred_element_type=F32)
    dt_ref[...] = jax.nn.softplus(jnp.dot(xn, wdt_ref[...], preferred_element_type=F32) + dtb_ref[...])
    xbc = jnp.dot(xn, wxbc_ref[...], preferred_element_type=F32)
    xbc_ref[...] = xbc
    y = jnp.zeros_like(xbc)
    for k in range(SSM_CONV - 1):
        y = y + cw_ref[k:k + 1, :] * st_ref[k]
    y = y + cw_ref[SSM_CONV - 1:SSM_CONV, :] * xbc
    act_ref[...] = _silu(y + cb_ref[...])


def _ssm_in_sample(x, g, w, state_t):
    m, d = x.shape
    args = [x, g.reshape(1, d), w["wz"], w["wxbc"], w["wdt"], w["cw"], w["cb"], w["dtb"], state_t]
    shapes = [(m, D_INNER), (m, SSM_CONV_DIM), (m, SSM_CONV_DIM), (m, SSM_HEADS)]
    return pl.pallas_call(
        _ssm_in_sample_kernel, grid=(1,),
        in_specs=[_const_spec(a.shape) for a in args],
        out_specs=[_const_spec(s) for s in shapes],
        out_shape=[jax.ShapeDtypeStruct(s, F32) for s in shapes],
        compiler_params=_cparams(("arbitrary",)), name="ssm_in_sample")(*args)


def _ssm_step_kernel(act_ref, dt_ref, alog_ref, d_ref, h_ref, y_ref, hn_ref):
    act = act_ref[0]
    dt = dt_ref[0]
    decay = jnp.exp(dt * (-jnp.exp(alog_ref[...])))
    dvec = d_ref[...]
    eye = (lax.broadcasted_iota(jnp.int32, (SSM_HD, SSM_HD), 0)
           == lax.broadcasted_iota(jnp.int32, (SSM_HD, SSM_HD), 1))
    for g in range(SSM_GROUPS):
        bg = act[:, D_INNER + g * SSM_STATE:D_INNER + (g + 1) * SSM_STATE]
        cg = jnp.broadcast_to(act[:, D_INNER + SSM_GN + g * SSM_STATE:D_INNER + SSM_GN + (g + 1) * SSM_STATE],
                              (8, SSM_STATE))
        for r8 in range(SSM_REP):
            r = g * SSM_REP + r8
            xr = act[:, r * SSM_HD:(r + 1) * SSM_HD]
            xdt = xr * dt[:, r:r + 1]
            xcol = jnp.sum(jnp.where(eye, jnp.broadcast_to(xdt, (SSM_HD, SSM_HD)), 0.0), axis=1, keepdims=True)
            h_new = decay[:, r:r + 1] * h_ref[0, r] + xcol * bg
            hn_ref[0, r] = h_new
            y = _dot_nt(cg, h_new)[0:1] + xr * dvec[:, r:r + 1]
            y_ref[0, :, r * SSM_HD:(r + 1) * SSM_HD] = y


def _ssm_step(act, dt, alog, dvec, state):
    bsz = act.shape[0]
    blk3 = lambda b: (b, 0, 0)
    y, hn = pl.pallas_call(
        _ssm_step_kernel, grid=(bsz,),
        in_specs=[pl.BlockSpec((1, 1, SSM_CONV_DIM), blk3), pl.BlockSpec((1, 1, SSM_HEADS), blk3),
                  _const_spec((1, SSM_HEADS)), _const_spec((1, SSM_HEADS)),
                  pl.BlockSpec((1, SSM_HEADS, SSM_HD, SSM_STATE), lambda b: (b, 0, 0, 0))],
        out_specs=[pl.BlockSpec((1, 1, D_INNER), blk3),
                   pl.BlockSpec((1, SSM_HEADS, SSM_HD, SSM_STATE), lambda b: (b, 0, 0, 0))],
        out_shape=[jax.ShapeDtypeStruct((bsz, 1, D_INNER), F32),
                   jax.ShapeDtypeStruct((bsz, SSM_HEADS, SSM_HD, SSM_STATE), F32)],
        compiler_params=_cparams(("parallel",)), name="ssm_step")(
            act.reshape(bsz, 1, SSM_CONV_DIM), dt.reshape(bsz, 1, SSM_HEADS), alog.reshape(1, SSM_HEADS),
            dvec.reshape(1, SSM_HEADS), state)
    return y.reshape(bsz, D_INNER), hn


def _dil_qkv_kernel(x_ref, g_ref, w_ref, cos_ref, sin_ref, q0_ref, q1_ref, q2_ref, kv0_ref, kv1_ref, kv2_ref):
    cos, sin = cos_ref[...], sin_ref[...]
    h = _dot(_rms(x_ref[...], g_ref[...]), w_ref[...])
    tiles = []
    for t in range(2 * DIL_WIDTH // LANE):
        tile = h[:, t * LANE:(t + 1) * LANE]
        tiles.append(tile * cos + _rot_half(tile, DIL_HD) * sin)
    qk = jnp.concatenate(tiles, axis=1)
    for g, (q_ref, kv_ref) in enumerate(((q0_ref, kv0_ref), (q1_ref, kv1_ref), (q2_ref, kv2_ref))):
        q_ref[...] = qk[:, g * DIL_GW:(g + 1) * DIL_GW]
        kv_ref[:, :DIL_GW] = qk[:, DIL_WIDTH + g * DIL_GW:DIL_WIDTH + (g + 1) * DIL_GW]
        kv_ref[:, DIL_GW:] = h[:, 2 * DIL_WIDTH + g * DIL_GW:2 * DIL_WIDTH + (g + 1) * DIL_GW]


def _dil_qkv(x, g, w, cos, sin, tm=256):
    m, d = x.shape
    tm = min(tm, m)
    nt = cos.shape[0] // tm
    row = lambda i: (i, 0)
    widths = [DIL_GW] * N_DIL + [2 * DIL_GW] * N_DIL
    return pl.pallas_call(
        _dil_qkv_kernel, grid=(m // tm,),
        in_specs=[pl.BlockSpec((tm, d), row), _const_spec((1, d)), _const_spec(w.shape)]
        + [pl.BlockSpec((tm, LANE), lambda i: (i % nt, 0))] * 2,
        out_specs=[pl.BlockSpec((tm, wd), row) for wd in widths],
        out_shape=[jax.ShapeDtypeStruct((m, wd), F32) for wd in widths],
        compiler_params=_cparams(("parallel",)), name="dil_qkv")(x, g.reshape(1, d), w, cos, sin)


def _dil_band_kernel(sub_len, q_ref, kv_ref, o_ref, lse_ref):
    nk = DIL_NK
    qi = lax.broadcasted_iota(jnp.int32, (nk, 2 * nk), 0)
    ki = lax.broadcasted_iota(jnp.int32, (nk, 2 * nk), 1)
    band = (ki <= qi + nk) & (ki >= qi)
    causal = lax.broadcasted_iota(jnp.int32, (nk, nk), 1) <= lax.broadcasted_iota(jnp.int32, (nk, nk), 0)
    for blk in range(sub_len // nk):
        rows = pl.ds(blk * nk, nk)
        q = q_ref[0, rows, :]
        if blk == 0:
            kvb, mask = kv_ref[0, rows, :].astype(BF16), causal
        else:
            kvb, mask = kv_ref[0, pl.ds((blk - 1) * nk, 2 * nk), :].astype(BF16), band
        for h in range(DIL_HEADS):
            cols = slice(h * DIL_HD, (h + 1) * DIL_HD)
            s = _dot_nt(q[:, cols], kvb[:, cols]) * DIL_HD ** -0.5
            s = jnp.where(mask, s, NEG_INF)
            mx = jnp.max(s, axis=-1, keepdims=True)
            lse = mx + jnp.log(jnp.sum(jnp.exp(s - mx), axis=-1, keepdims=True))
            o_ref[0, rows, cols] = _dot(jnp.exp(s - lse), kvb[:, DIL_GW + h * DIL_HD:DIL_GW + (h + 1) * DIL_HD])
            lse_ref[0, rows, cols] = jnp.broadcast_to(lse, (nk, DIL_HD))


def _dil_prompt(q, kv, dil, bsz, seq):
    sub_len = seq // dil

    def split(a):
        w = a.shape[-1]
        return a.reshape(bsz, sub_len, dil, w).transpose(0, 2, 1, 3).reshape(bsz * dil, sub_len, w)

    def merge(a):
        return a.reshape(bsz, dil, sub_len, DIL_GW).transpose(0, 2, 1, 3).reshape(bsz * seq, DIL_GW)

    blk = lambda b: (b, 0, 0)
    o, lse = pl.pallas_call(
        functools.partial(_dil_band_kernel, sub_len), grid=(bsz * dil,),
        in_specs=[pl.BlockSpec((1, sub_len, DIL_GW), blk), pl.BlockSpec((1, sub_len, 2 * DIL_GW), blk)],
        out_specs=[pl.BlockSpec((1, sub_len, DIL_GW), blk)] * 2,
        out_shape=[jax.ShapeDtypeStruct((bsz * dil, sub_len, DIL_GW), F32)] * 2,
        compiler_params=_cparams(("parallel",)), name="dil_band_d%d" % dil)(split(q), split(kv))
    return merge(o), merge(lse)


def _dil_combine(outs, lses):
    mx = jnp.maximum(jnp.maximum(lses[0], lses[1]), lses[2])
    es = [jnp.exp(l - mx) for l in lses]
    den = es[0] + es[1] + es[2]
    return [o * (e / den) for o, e in zip(outs, es)]


def _dil_out_kernel(o0_ref, o1_ref, o2_ref, l0_ref, l1_ref, l2_ref, w_ref, res_ref, out_ref):
    parts = _dil_combine([o0_ref[...], o1_ref[...], o2_ref[...]], [l0_ref[...], l1_ref[...], l2_ref[...]])
    y = res_ref[...]
    for g in range(N_DIL):
        y = y + _dot(parts[g], w_ref[g])
    out_ref[...] = y


def _dil_out(outs, lses, w, res, tm=256):
    m, d = res.shape
    tm = min(tm, m)
    row = lambda i: (i, 0)
    return pl.pallas_call(
        _dil_out_kernel, grid=(m // tm,),
        in_specs=[pl.BlockSpec((tm, DIL_GW), row)] * 6 + [_const_spec(w.shape), pl.BlockSpec((tm, d), row)],
        out_specs=pl.BlockSpec((tm, d), row),
        out_shape=jax.ShapeDtypeStruct((m, d), F32),
        compiler_params=_cparams(("parallel",)), name="dil_out")(*outs, *lses, w, res)


def _dil_sample_kernel(q0_ref, q1_ref, q2_ref, n0_ref, n1_ref, n2_ref, b0_ref, b1_ref, b2_ref, o_ref):
    outs, lses = [], []
    for (win, dil), q_ref, n_ref, b_ref in zip(DIL_GROUPS, (q0_ref, q1_ref, q2_ref), (n0_ref, n1_ref, n2_ref),
                                               (b0_ref, b1_ref, b2_ref)):
        nk = win // dil
        kv = b_ref[0].astype(BF16)
        q = q_ref[0].astype(BF16)
        new = n_ref[0].astype(BF16).astype(F32)
        og, lg = [], []
        for h in range(DIL_HEADS):
            qh = q[:, h * DIL_HD:(h + 1) * DIL_HD]
            s = _dot_nt(jnp.broadcast_to(qh, (8, DIL_HD)), kv[:, h * DIL_HD:(h + 1) * DIL_HD])[0:1] * DIL_HD ** -0.5
            s_new = jnp.sum(qh.astype(F32) * new[:, h * DIL_HD:(h + 1) * DIL_HD], axis=-1, keepdims=True) * DIL_HD ** -0.5
            mx = jnp.maximum(jnp.max(s, axis=-1, keepdims=True), s_new)
            lse = mx + jnp.log(jnp.sum(jnp.exp(s - mx), axis=-1, keepdims=True) + jnp.exp(s_new - mx))
            p = jnp.broadcast_to(jnp.exp(s - lse), (8, nk))
            o = _dot(p, kv[:, DIL_GW + h * DIL_HD:DIL_GW + (h + 1) * DIL_HD])[0:1]
            p_new = jnp.exp(s_new - lse).astype(BF16).astype(F32)
            og.append(o + p_new * new[:, DIL_GW + h * DIL_HD:DIL_GW + (h + 1) * DIL_HD])
            lg.append(jnp.broadcast_to(lse, (1, DIL_HD)))
        outs.append(jnp.concatenate(og, axis=1))
        lses.append(jnp.concatenate(lg, axis=1))
    o_ref[0] = jnp.concatenate(_dil_combine(outs, lses), axis=1)


def _dil_sample(qs, news, bufs):
    bsz = qs[0].shape[0]
    blk = lambda b: (b, 0, 0)
    args = [q.reshape(bsz, 1, DIL_GW) for q in qs] + [n.reshape(bsz, 1, 2 * DIL_GW) for n in news] + list(bufs)
    out = pl.pallas_call(
        _dil_sample_kernel, grid=(bsz,),
        in_specs=[pl.BlockSpec((1,) + a.shape[1:], blk) for a in args],
        out_specs=pl.BlockSpec((1, 1, DIL_WIDTH), blk),
        out_shape=jax.ShapeDtypeStruct((bsz, 1, DIL_WIDTH), F32),
        compiler_params=_cparams(("parallel",)), name="dil_sample")(*args)
    return out.reshape(bsz, DIL_WIDTH)


def _block_diag_pairs(w):
    h, k, n = w.shape
    z = jnp.zeros((h // 2, k, n), w.dtype)
    top = jnp.concatenate([w[0::2], z], axis=2)
    bot = jnp.concatenate([z, w[1::2]], axis=2)
    return jnp.concatenate([top, bot], axis=1)


def _prep_mla(mla_w_in, mla_norm_q, mla_w_uq, mla_norm_kv, mla_w_uk, mla_w_uv, mla_w_o):
    pad = jnp.zeros((D_MODEL, LANE - MLA_ROPE), F32)
    wuq = mla_w_uq.reshape(MLA_Q_LORA, MLA_HEADS, MLA_NOPE + MLA_ROPE)
    wuq = jnp.concatenate([wuq[:, :, :MLA_NOPE].reshape(MLA_Q_LORA, -1), wuq[:, :, MLA_NOPE:].reshape(MLA_Q_LORA, -1)],
                          axis=1)
    return {
        "win": jnp.concatenate([mla_w_in, pad], axis=1).astype(BF16),
        "nq": mla_norm_q.reshape(1, -1), "nkv": mla_norm_kv.reshape(1, -1),
        "wuq": wuq.astype(BF16),
        "wuk": _block_diag_pairs(jnp.transpose(mla_w_uk, (1, 2, 0))).astype(BF16),
        "wuv": _block_diag_pairs(jnp.transpose(mla_w_uv, (1, 0, 2))).astype(BF16),
        "wo": _cast_bf16(mla_w_o),
    }


def _prep_conv(conv_w_in, conv_b_in, conv_w_dw, conv_b_dw, conv_ln_g, conv_ln_b, conv_w_out, conv_b_out):
    return {"win": _cast_bf16(conv_w_in), "bin": conv_b_in.reshape(1, -1), "wdw": conv_w_dw,
            "bdw": conv_b_dw.reshape(1, -1), "lng": conv_ln_g.reshape(1, -1), "lnb": conv_ln_b.reshape(1, -1),
            "wout": _cast_bf16(conv_w_out), "bout": conv_b_out.reshape(1, -1)}


def _prep_ssm(ssm_w_in, ssm_conv_w, ssm_conv_b, ssm_dt_bias):
    w = ssm_w_in.astype(BF16)
    return {"wz": w[:, :D_INNER], "wxbc": w[:, D_INNER:D_INNER + SSM_CONV_DIM], "wdt": w[:, D_INNER + SSM_CONV_DIM:],
            "cw": ssm_conv_w, "cb": ssm_conv_b.reshape(1, -1), "dtb": ssm_dt_bias.reshape(1, -1)}


def kernel(x_prompt, x_sample, cache_mla_latent, cache_mem_kv, state_conv, state_ssm_conv, state_ssm, state_dil_kv0, state_dil_kv1, state_dil_kv2, page_table, mem_prompt, norm_ffn1, ffn1_w_gu, ffn1_w_down, norm_mix, norm_cross, norm_mem, cross_w_q, cross_w_kv, cross_w_o, norm_ffn2, ffn2_w_gu, ffn2_w_down, norm_final, mla_w_in, mla_norm_q, mla_w_uq, mla_norm_kv, mla_w_uk, mla_w_uv, mla_w_o, conv_w_in, conv_b_in, conv_w_dw, conv_b_dw, conv_ln_g, conv_ln_b, conv_w_out, conv_b_out, ssm_w_in, ssm_conv_w, ssm_conv_b, ssm_dt_bias, ssm_a_log, ssm_d, ssm_norm, ssm_w_out, dil_w_qkv, dil_w_o):
    bsz, seq, d = x_prompt.shape
    dec = x_sample.shape[0]
    assert x_sample.shape[1] == 1 and d == D_MODEL
    depth = norm_ffn1.shape[0]
    past_len = page_table.shape[1] * cache_mla_latent.shape[1]
    pos_p = jnp.arange(seq, dtype=jnp.int32)
    pos_s = jnp.full((dec,), past_len, dtype=jnp.int32)

    res_p = x_prompt.reshape(bsz * seq, d)
    res_s = x_sample.reshape(dec, d)
    mem_len = mem_prompt.shape[1]
    w1gu, w1d = _cast_bf16(ffn1_w_gu), _cast_bf16(ffn1_w_down)
    w2gu, w2d = _cast_bf16(ffn2_w_gu), _cast_bf16(ffn2_w_down)
    xwq, xwo = _cast_bf16(cross_w_q), _cast_bf16(cross_w_o)
    mem_kv = _memkv(mem_prompt.reshape(bsz * mem_len, d), norm_mem, _cast_bf16(cross_w_kv))
    mem_kv_b = mem_kv.reshape(depth * bsz, mem_len, 2 * d)
    outs = {}

    for i in range(depth):
        kind = i % 4
        res_p = _ffn(res_p, norm_ffn1, w1gu, w1d, i)
        res_s = _ffn(res_s, norm_ffn1, w1gu, w1d, i)
        g_mix = norm_mix[i]
        if kind == 0:
            w = _prep_mla(mla_w_in, mla_norm_q, mla_w_uq, mla_norm_kv, mla_w_uk, mla_w_uv, mla_w_o)
            cos_p, sin_p = _rope_tables(pos_p, MLA_ROPE)
            cos_s, sin_s = _rope_tables(pos_s, MLA_ROPE)
            q_p, lat_p = _mla_proj(res_p, g_mix, w, cos_p, sin_p)
            o_p = _mla_attn_prompt(q_p, lat_p, bsz, seq)
            res_p = _mla_out(o_p, w["wuv"], w["wo"], res_p)
            q_s, lat_s = _mla_proj(res_s, g_mix, w, cos_s, sin_s)
            o_s = _mla_attn_sample(jnp.transpose(q_s, (1, 0, 2)), lat_s.reshape(dec, 1, MLA_LATENT),
                                   jnp.transpose(cache_mla_latent, (0, 2, 1)), page_table)
            res_s = _mla_out(jnp.transpose(o_s, (1, 0, 2)), w["wuv"], w["wo"], res_s)
            outs["mla_p"] = lat_p.reshape(bsz, seq, MLA_LATENT)
            outs["mla_s"] = lat_s.reshape(dec, 1, MLA_LATENT)
        elif kind == 1:
            w = _prep_conv(conv_w_in, conv_b_in, conv_w_dw, conv_b_dw, conv_ln_g, conv_ln_b, conv_w_out, conv_b_out)
            res_p, outs["conv_p"] = _conv_prompt(res_p, g_mix, w, bsz, seq)
            res_s, u_s = _conv_sample(res_s, g_mix, w, jnp.transpose(state_conv, (1, 0, 2)))
            outs["conv_s"] = jnp.concatenate([state_conv[:, 1:], u_s[:, None, :]], axis=1)
        elif kind == 2:
            w = _prep_ssm(ssm_w_in, ssm_conv_w, ssm_conv_b, ssm_dt_bias)
            wout = _cast_bf16(ssm_w_out)
            z_p, act_p, dt_p, outs["ssmc_p"] = _ssm_in_prompt(res_p, g_mix, w, bsz, seq)
            y_p, h_t = _ssd_scan(act_p, dt_p, ssm_a_log, ssm_d, bsz, seq)
            outs["ssm_p"] = jnp.swapaxes(h_t, 2, 3)
            res_p = _ssm_out(y_p, z_p, ssm_norm, wout, res_p)
            z_s, xbc_s, act_s, dt_s = _ssm_in_sample(res_s, g_mix, w, jnp.transpose(state_ssm_conv, (1, 0, 2)))
            outs["ssmc_s"] = jnp.concatenate([state_ssm_conv[:, 1:], xbc_s[:, None, :]], axis=1)
            y_s, outs["ssm_s"] = _ssm_step(act_s, dt_s, ssm_a_log, ssm_d, state_ssm)
            res_s = _ssm_out(y_s, z_s, ssm_norm, wout, res_s)
        else:
            wqkv = _cast_bf16(dil_w_qkv)
            wo2 = _cast_bf16(dil_w_o)
            wo = wo2.reshape(N_DIL, DIL_GW, d)
            cos_p, sin_p = _rope_tables(pos_p, DIL_HD)
            cos_s, sin_s = _rope_tables(pos_s, DIL_HD)
            qkv_p = _dil_qkv(res_p, g_mix, wqkv, cos_p, sin_p)
            qkv_s = _dil_qkv(res_s, g_mix, wqkv, cos_s, sin_s)
            bufs = (state_dil_kv0, state_dil_kv1, state_dil_kv2)
            o_list, l_list = [], []
            for g, (win, dil) in enumerate(DIL_GROUPS):
                kv3 = qkv_p[N_DIL + g].reshape(bsz, seq, 2 * DIL_GW)
                o, l = _dil_prompt(qkv_p[g], qkv_p[N_DIL + g], dil, bsz, seq)
                o_list.append(o)
                l_list.append(l)
                keep = min(win, seq)
                outs["dil%d_p" % g] = kv3[:, seq - keep:].reshape(bsz, keep, 2, DIL_HEADS, DIL_HD)
                new = qkv_s[N_DIL + g].reshape(dec, 1, 2, DIL_HEADS, DIL_HD)
                outs["dil%d_s" % g] = jnp.concatenate([bufs[g][:, 1:], new], axis=1)
            res_p = _dil_out(o_list, l_list, wo, res_p)
            picked = [b[:, ::dil].reshape(dec, win // dil, 2 * DIL_GW) for b, (win, dil) in zip(bufs, DIL_GROUPS)]
            o_s = _dil_sample(qkv_s[:N_DIL], qkv_s[N_DIL:], picked)
            res_s = _proj_res([o_s], [wo2], res_s, name="dil_out_sample")

        res_p = _xattn_prompt(res_p, norm_cross, xwq, mem_kv_b, xwo, i, bsz, seq)
        (q_s,) = _norm_proj(res_s, norm_cross[i], [xwq[i]], name="xattn_q_sample")
        a_s = _xattn_sample(q_s, cache_mem_kv, i)
        res_s = _proj_res([a_s], [xwo[i]], res_s, name="xattn_o_sample")

        final_g = norm_final if i == depth - 1 else None
        res_p = _ffn(res_p, norm_ffn2, w2gu, w2d, i, final_g)
        res_s = _ffn(res_s, norm_ffn2, w2gu, w2d, i, final_g)

    return (res_p.reshape(bsz, seq, d), res_s.reshape(dec, 1, d),
            outs["mla_p"], outs["mla_s"], outs["conv_p"], outs["conv_s"], outs["ssmc_p"], outs["ssmc_s"],
            outs["ssm_p"], outs["ssm_s"],
            outs["dil0_p"], outs["dil0_s"], outs["dil1_p"], outs["dil1_s"], outs["dil2_p"], outs["dil2_s"],
            mem_kv.reshape(depth, bsz, mem_len, 2, MEM_HEADS, MEM_HD))
```

```python
import functools
import math

import jax
import jax.numpy as jnp
from jax import lax
from jax.experimental import pallas as pl
from jax.experimental.pallas import tpu as pltpu

F32 = jnp.float32
BF16 = jnp.bfloat16

D_MODEL = 1024
D_FF = 2816
RMS_EPS = 1e-6
LN_EPS = 1e-5
ROPE_THETA = 10000.0

MLA_HEADS = 16
MLA_Q_LORA = 256
MLA_KV_LORA = 128
MLA_NOPE = 64
MLA_ROPE = 32
MLA_V_HD = 64
MLA_LATENT = MLA_KV_LORA + MLA_ROPE
MLA_SCALE = (MLA_NOPE + MLA_ROPE) ** -0.5

CONV_WIDTH = 31
CONV_HALO = 32

D_INNER = 2 * D_MODEL
SSM_HD = 64
SSM_HEADS = D_INNER // SSM_HD
SSM_GROUPS = 4
SSM_REP = SSM_HEADS // SSM_GROUPS
SSM_STATE = 128
SSM_CONV = 4
SSM_CHUNK = 128
SSM_GN = SSM_GROUPS * SSM_STATE
SSM_CONV_DIM = D_INNER + 2 * SSM_GN
SSM_HALO = 8

DIL_GROUPS = ((128, 1), (512, 4), (2048, 16))
N_DIL = 3
DIL_HEADS = 5
DIL_HD = 64
DIL_GW = DIL_HEADS * DIL_HD
DIL_WIDTH = N_DIL * DIL_GW
DIL_NK = 128

MEM_HEADS = 4
MEM_HD = D_MODEL // MEM_HEADS

LANE = 128
NEG_INF = float("-inf")
LOG2E = 1.4426950408889634


def _cparams(sem, vmem_mb=48):
    return pltpu.CompilerParams(dimension_semantics=sem, vmem_limit_bytes=vmem_mb * 1024 * 1024)


def _const_spec(shape):
    nd = len(shape)
    return pl.BlockSpec(shape, lambda *_: (0,) * nd, pipeline_mode=pl.Buffered(1))


def _dot(a, b):
    return jnp.dot(a.astype(BF16), b.astype(BF16), preferred_element_type=F32)


def _dot_nt(a, b):
    return lax.dot_general(a.astype(BF16), b.astype(BF16), (((1,), (1,)), ((), ())),
                           preferred_element_type=F32)


def _dot_exact(a, b):
    return jnp.dot(a, b, preferred_element_type=F32, precision=lax.Precision.HIGHEST)


def _dot_nt_exact(a, b):
    return lax.dot_general(a, b, (((1,), (1,)), ((), ())), preferred_element_type=F32,
                           precision=lax.Precision.HIGHEST)


def _rms(x, g):
    return x * lax.rsqrt(jnp.mean(x * x, axis=-1, keepdims=True) + RMS_EPS) * g


def _silu(x):
    return x * jax.nn.sigmoid(x)


def _rot_half(t, dim):
    half = dim // 2
    lane = lax.broadcasted_iota(jnp.int32, t.shape, 1)
    return jnp.where((lane & (dim - 1)) < half, pltpu.roll(t, LANE - half, 1), pltpu.roll(t, half, 1))


def _rope_tables(pos, dim):
    half = dim // 2
    inv_freq = jnp.exp(-math.log(ROPE_THETA) * (2.0 / dim) * jnp.arange(half, dtype=F32))
    ang = pos.astype(F32)[:, None] * inv_freq[None, :]
    cos, sin = jnp.cos(ang), jnp.sin(ang)
    reps = LANE // dim
    return (jnp.tile(jnp.concatenate([cos, cos], axis=1), (1, reps)),
            jnp.tile(jnp.concatenate([-sin, sin], axis=1), (1, reps)))


def _softmax_rows(s):
    m = jnp.max(s, axis=-1, keepdims=True)
    p = jnp.exp(s - m)
    return p / jnp.sum(p, axis=-1, keepdims=True)


def _cast_kernel(x_ref, o_ref):
    o_ref[...] = x_ref[...].astype(BF16)


def _cast_bf16(w, rows=256):
    shape = w.shape
    w2 = w.reshape(-1, shape[-1])
    m, n = w2.shape
    rows = min(rows, m)
    out = pl.pallas_call(
        _cast_kernel, grid=(pl.cdiv(m, rows),),
        in_specs=[pl.BlockSpec((rows, n), lambda i: (i, 0))],
        out_specs=pl.BlockSpec((rows, n), lambda i: (i, 0)),
        out_shape=jax.ShapeDtypeStruct((m, n), BF16),
        compiler_params=_cparams(("parallel",)), name="cast_bf16")(w2)
    return out.reshape(shape)


def _layer_spec(shape, layer):
    nd = len(shape)
    return pl.BlockSpec((1,) + tuple(shape[1:]), lambda *_: (layer,) + (0,) * (nd - 1), pipeline_mode=pl.Buffered(1))


def _ffn_kernel(has_final, x_ref, g_ref, wgu_ref, wd_ref, *rest):
    o_ref = rest[-1]
    x = x_ref[...]
    h = _dot(_rms(x, g_ref[0]), wgu_ref[0])
    act = _silu(h[:, :D_FF]) * h[:, D_FF:]
    y = x + 0.5 * _dot(act, wd_ref[0])
    if has_final:
        y = _rms(y, rest[0][...])
    o_ref[...] = y


def _ffn(x, g, wgu, wd, layer, final_g=None, tm=512):
    m, d = x.shape
    tm = min(tm, m)
    g = g.reshape(g.shape[0], 1, d)
    args = [x, g, wgu, wd]
    in_specs = [pl.BlockSpec((tm, d), lambda i: (i, 0)), _layer_spec(g.shape, layer),
                _layer_spec(wgu.shape, layer), _layer_spec(wd.shape, layer)]
    if final_g is not None:
        args.append(final_g.reshape(1, d))
        in_specs.append(_const_spec((1, d)))
    return pl.pallas_call(
        functools.partial(_ffn_kernel, final_g is not None),
        grid=(m // tm,), in_specs=in_specs,
        out_specs=pl.BlockSpec((tm, d), lambda i: (i, 0)),
        out_shape=jax.ShapeDtypeStruct((m, d), F32),
        compiler_params=_cparams(("parallel",), 56), name="ffn")(*args)


def _norm_proj_kernel(n, has_bias, x_ref, g_ref, *refs):
    xn = _rms(x_ref[...], g_ref[...]).astype(BF16)
    for i in range(n):
        y = jnp.dot(xn, refs[i][...], preferred_element_type=F32)
        if has_bias:
            y = y + refs[n + i][...]
        refs[-n + i][...] = y


def _norm_proj(x, g, ws, bs=None, tm=256, name="norm_proj"):
    m, d = x.shape
    tm = min(tm, m)
    n = len(ws)
    args = [x, g.reshape(1, d)] + list(ws)
    in_specs = [pl.BlockSpec((tm, d), lambda i: (i, 0)), _const_spec((1, d))] + [_const_spec(w.shape) for w in ws]
    if bs is not None:
        args += [b.reshape(1, -1) for b in bs]
        in_specs += [_const_spec((1, b.shape[-1])) for b in bs]
    outs = pl.pallas_call(
        functools.partial(_norm_proj_kernel, n, bs is not None),
        grid=(m // tm,), in_specs=in_specs,
        out_specs=[pl.BlockSpec((tm, w.shape[1]), lambda i: (i, 0)) for w in ws],
        out_shape=[jax.ShapeDtypeStruct((m, w.shape[1]), F32) for w in ws],
        compiler_params=_cparams(("parallel",)), name=name)(*args)
    return outs


def _proj_res_kernel(n, has_bias, *refs):
    a_refs, w_refs = refs[:n], refs[n:2 * n]
    res_ref, o_ref = refs[2 * n], refs[-1]
    y = res_ref[...]
    for a_ref, w_ref in zip(a_refs, w_refs):
        y = y + _dot(a_ref[...], w_ref[...])
    if has_bias:
        y = y + refs[2 * n + 1][...]
    o_ref[...] = y


def _proj_res(a_list, w_list, res, bias=None, tm=256, name="proj_res"):
    m, d = res.shape
    tm = min(tm, m)
    n = len(a_list)
    args = list(a_list) + list(w_list) + [res]
    in_specs = ([pl.BlockSpec((tm, a.shape[1]), lambda i: (i, 0)) for a in a_list]
                + [_const_spec(w.shape) for w in w_list] + [pl.BlockSpec((tm, d), lambda i: (i, 0))])
    if bias is not None:
        args.append(bias.reshape(1, d))
        in_specs.append(_const_spec((1, d)))
    return pl.pallas_call(
        functools.partial(_proj_res_kernel, n, bias is not None),
        grid=(m // tm,), in_specs=in_specs,
        out_specs=pl.BlockSpec((tm, d), lambda i: (i, 0)),
        out_shape=jax.ShapeDtypeStruct((m, d), F32),
        compiler_params=_cparams(("parallel",)), name=name)(*args)


def _memkv_kernel(x_ref, g_ref, w_ref, o_ref):
    o_ref[0] = _dot(_rms(x_ref[...], g_ref[0]), w_ref[0])


def _memkv(mem2d, norm_mem, w_kv, tm=256):
    m, d = mem2d.shape
    depth = w_kv.shape[0]
    n = w_kv.shape[2]
    return pl.pallas_call(
        _memkv_kernel, grid=(depth, m // tm),
        in_specs=[pl.BlockSpec((tm, d), lambda l, i: (i, 0)),
                  pl.BlockSpec((1, 1, d), lambda l, i: (l, 0, 0)),
                  pl.BlockSpec((1, d, n), lambda l, i: (l, 0, 0))],
        out_specs=pl.BlockSpec((1, tm, n), lambda l, i: (l, i, 0)),
        out_shape=jax.ShapeDtypeStruct((depth, m, n), F32),
        compiler_params=_cparams(("parallel", "parallel")), name="memkv")(mem2d, norm_mem.reshape(depth, 1, d), w_kv)


def _xattn_core(q, kv):
    kvb = kv.astype(BF16)
    outs = []
    for h in range(MEM_HEADS):
        kh = kvb[:, h * MEM_HD:(h + 1) * MEM_HD]
        vh = kvb[:, D_MODEL + h * MEM_HD:D_MODEL + (h + 1) * MEM_HD]
        s = _dot_nt(q[:, h * MEM_HD:(h + 1) * MEM_HD], kh) * MEM_HD ** -0.5
        outs.append(_dot(_softmax_rows(s), vh))
    return jnp.concatenate(outs, axis=1)


def _xattn_prompt_kernel(x_ref, g_ref, wq_ref, kv_ref, wo_ref, o_ref):
    x = x_ref[...]
    q = _dot(_rms(x, g_ref[0]), wq_ref[0])
    o_ref[...] = x + _dot(_xattn_core(q, kv_ref[0]), wo_ref[0])


def _xattn_prompt(x, g, wq, kv, wo, layer, bsz, seq, tq=512):
    m, d = x.shape
    nq = seq // tq
    _, ml, kvw = kv.shape
    g = g.reshape(g.shape[0], 1, d)
    return pl.pallas_call(
        _xattn_prompt_kernel, grid=(bsz, nq),
        in_specs=[pl.BlockSpec((tq, d), lambda b, i: (b * nq + i, 0)), _layer_spec(g.shape, layer),
                  _layer_spec(wq.shape, layer),
                  pl.BlockSpec((1, ml, kvw), lambda b, i: (layer * bsz + b, 0, 0)), _layer_spec(wo.shape, layer)],
        out_specs=pl.BlockSpec((tq, d), lambda b, i: (b * nq + i, 0)),
        out_shape=jax.ShapeDtypeStruct((m, d), F32),
        compiler_params=_cparams(("parallel", "parallel")), name="xattn_prompt")(x, g, wq, kv, wo)


def _xattn_sample_kernel(q_ref, kv_ref, o_ref):
    q = jnp.broadcast_to(q_ref[0], (8, D_MODEL))
    outs = []
    for h in range(MEM_HEADS):
        kh = kv_ref[0, 0, :, 0, h, :]
        vh = kv_ref[0, 0, :, 1, h, :]
        s = _dot_nt(q[:, h * MEM_HD:(h + 1) * MEM_HD], kh) * MEM_HD ** -0.5
        outs.append(_dot(_softmax_rows(s), vh))
    o_ref[0] = jnp.concatenate(outs, axis=1)[0:1]


def _xattn_sample(q, cache, layer):
    bsz, d = q.shape
    out = pl.pallas_call(
        _xattn_sample_kernel, grid=(bsz,),
        in_specs=[pl.BlockSpec((1, 1, d), lambda b: (b, 0, 0)),
                  pl.BlockSpec((1, 1) + cache.shape[2:], lambda b: (layer, b, 0, 0, 0, 0))],
        out_specs=pl.BlockSpec((1, 1, d), lambda b: (b, 0, 0)),
        out_shape=jax.ShapeDtypeStruct((bsz, 1, d), F32),
        compiler_params=_cparams(("parallel",)), name="xattn_sample")(q.reshape(bsz, 1, d), cache)
    return out.reshape(bsz, d)


def _mla_proj_kernel(x_ref, g_ref, win_ref, nq_ref, wuq_ref, nkv_ref, wuk_ref, cos_ref, sin_ref, q_ref, lat_ref):
    cos, sin = cos_ref[...], sin_ref[...]
    a = _dot(_rms(x_ref[...], g_ref[...]), win_ref[...])
    ckv = _rms(a[:, MLA_Q_LORA:MLA_Q_LORA + MLA_KV_LORA], nkv_ref[...])
    pe = a[:, MLA_Q_LORA + MLA_KV_LORA:]
    kpe = pe * cos + _rot_half(pe, MLA_ROPE) * sin
    lat_ref[:, :MLA_KV_LORA] = ckv
    lat_ref[:, MLA_KV_LORA:] = kpe[:, :MLA_ROPE]
    q = _dot(_rms(a[:, :MLA_Q_LORA], nq_ref[...]), wuq_ref[...])
    n_nope = MLA_HEADS * MLA_NOPE
    for p in range(MLA_HEADS // 2):
        ql = _dot(q[:, p * LANE:(p + 1) * LANE], wuk_ref[p])
        q_ref[2 * p, :, :MLA_KV_LORA] = ql[:, :MLA_KV_LORA]
        q_ref[2 * p + 1, :, :MLA_KV_LORA] = ql[:, MLA_KV_LORA:]
    per_tile = LANE // MLA_ROPE
    for t in range(MLA_HEADS // per_tile):
        tile = q[:, n_nope + t * LANE:n_nope + (t + 1) * LANE]
        qpe = tile * cos + _rot_half(tile, MLA_ROPE) * sin
        for j in range(per_tile):
            q_ref[t * per_tile + j, :, MLA_KV_LORA:] = qpe[:, j * MLA_ROPE:(j + 1) * MLA_ROPE]


def _mla_proj(x, g, w, cos, sin, tm=256):
    m, d = x.shape
    tm = min(tm, m)
    nt = cos.shape[0] // tm
    consts = [g.reshape(1, d), w["win"], w["nq"], w["wuq"], w["nkv"], w["wuk"]]
    return pl.pallas_call(
        _mla_proj_kernel, grid=(m // tm,),
        in_specs=([pl.BlockSpec((tm, d), lambda i: (i, 0))] + [_const_spec(c.shape) for c in consts]
                  + [pl.BlockSpec((tm, LANE), lambda i: (i % nt, 0))] * 2),
        out_specs=[pl.BlockSpec((MLA_HEADS, tm, MLA_LATENT), lambda i: (0, i, 0)),
                   pl.BlockSpec((tm, MLA_LATENT), lambda i: (i, 0))],
        out_shape=[jax.ShapeDtypeStruct((MLA_HEADS, m, MLA_LATENT), F32),
                   jax.ShapeDtypeStruct((m, MLA_LATENT), F32)],
        compiler_params=_cparams(("parallel",)), name="mla_proj")(x, *consts, cos, sin)


def _mla_attn_kernel(tq, tk, q_ref, lat_ref, wuv_ref, wo_ref, res_ref, o_ref):
    i = pl.program_id(1)
    rows = MLA_HEADS * tq
    q = (q_ref[...].reshape(rows, MLA_LATENT) * (MLA_SCALE * LOG2E)).astype(BF16)

    def chunk(c, carry, diagonal):
        m, l, acc = carry
        keys = lat_ref[pl.ds(pl.multiple_of(c * tk, tk), tk), :].astype(BF16)
        s = _dot_nt(q, keys)
        if diagonal:
            row = lax.broadcasted_iota(jnp.int32, (rows, tk), 0)
            col = lax.broadcasted_iota(jnp.int32, (rows, tk), 1)
            s = jnp.where(col - (row & (tq - 1)) <= i * tq - c * tk, s, NEG_INF)
        m_new = jnp.maximum(m, jnp.max(s, axis=-1, keepdims=True))
        alpha = jnp.exp2(m - m_new)
        p = jnp.exp2(s - m_new)
        l = alpha * l + jnp.sum(p, axis=-1, keepdims=True)
        acc = alpha * acc + _dot(p, keys[:, :MLA_KV_LORA])
        return m_new, l, acc

    init = (jnp.full((rows, 1), NEG_INF, F32), jnp.zeros((rows, 1), F32), jnp.zeros((rows, MLA_KV_LORA), F32))
    n_full = (i * tq) // tk
    carry = lax.fori_loop(0, n_full, lambda c, cr: chunk(c, cr, False), init)
    _, l, acc = chunk(n_full, carry, True)
    o = acc / l
    parts = []
    for p in range(MLA_HEADS // 2):
        pair = jnp.concatenate([o[(2 * p) * tq:(2 * p + 1) * tq], o[(2 * p + 1) * tq:(2 * p + 2) * tq]], axis=1)
        parts.append(_dot(pair, wuv_ref[p]))
    o_ref[...] = res_ref[...] + _dot(jnp.concatenate(parts, axis=1), wo_ref[...])


def _mla_attn_prompt(qabs, lat, wuv, wo, res, bsz, seq, tq=128, tk=512):
    m, d = res.shape
    nq = seq // tq
    assert tk % tq == 0
    return pl.pallas_call(
        functools.partial(_mla_attn_kernel, tq, tk), grid=(bsz, nq),
        in_specs=[pl.BlockSpec((MLA_HEADS, tq, MLA_LATENT), lambda b, i: (0, b * nq + i, 0)),
                  pl.BlockSpec((seq, MLA_LATENT), lambda b, i: (b, 0)), _const_spec(wuv.shape), _const_spec(wo.shape),
                  pl.BlockSpec((tq, d), lambda b, i: (b * nq + i, 0))],
        out_specs=pl.BlockSpec((tq, d), lambda b, i: (b * nq + i, 0)),
        out_shape=jax.ShapeDtypeStruct((m, d), F32),
        compiler_params=_cparams(("parallel", "parallel")), name="mla_attn")(qabs, lat, wuv, wo, res)


def _mla_decode_kernel(pages_per_chunk, n_pages, page, pt_ref, q_ref, new_ref, cache_ref, o_ref, buf, sem):
    b = pl.program_id(0)
    n_chunks = n_pages // pages_per_chunk

    def copies(seq_idx, c, slot):
        return [pltpu.make_async_copy(cache_ref.at[pt_ref[seq_idx, c * pages_per_chunk + j]],
                                      buf.at[slot, :, pl.ds(j * page, page)], sem.at[slot])
                for j in range(pages_per_chunk)]

    @pl.when(b == 0)
    def _():
        for cp in copies(0, 0, 0):
            cp.start()

    q = (q_ref[0] * (MLA_SCALE * LOG2E)).astype(BF16)
    m = jnp.full((MLA_HEADS, 1), NEG_INF, F32)
    l = jnp.zeros((MLA_HEADS, 1), F32)
    acc = jnp.zeros((MLA_HEADS, MLA_KV_LORA), F32)
    for c in range(n_chunks):
        slot = c % 2
        if c + 1 < n_chunks:
            for cp in copies(b, c + 1, 1 - slot):
                cp.start()
        else:
            @pl.when(b + 1 < pl.num_programs(0))
            def _():
                for cp in copies(b + 1, 0, 1 - slot):
                    cp.start()
        for cp in copies(b, c, slot):
            cp.wait()
        keys_t = buf[slot].astype(BF16)
        s = _dot(q, keys_t)
        m_new = jnp.maximum(m, jnp.max(s, axis=-1, keepdims=True))
        alpha = jnp.exp2(m - m_new)
        p = jnp.exp2(s - m_new)
        l = alpha * l + jnp.sum(p, axis=-1, keepdims=True)
        acc = alpha * acc + _dot_nt(p, keys_t[:MLA_KV_LORA, :])
        m = m_new
    new = new_ref[0].astype(BF16).astype(F32)
    s_new = jnp.sum(q.astype(F32) * new, axis=-1, keepdims=True)
    m_new = jnp.maximum(m, s_new)
    alpha = jnp.exp2(m - m_new)
    p_new = jnp.exp2(s_new - m_new)
    l = alpha * l + p_new
    acc = alpha * acc + p_new.astype(BF16).astype(F32) * new[:, :MLA_KV_LORA]
    o_ref[0] = acc / l


def _mla_attn_sample(q, new_rows, cache_t, page_table, pages_per_chunk=32):
    bsz = q.shape[0]
    n_pages = page_table.shape[1]
    page = cache_t.shape[2]
    pages_per_chunk = min(pages_per_chunk, n_pages // 2)
    assert n_pages % (2 * pages_per_chunk) == 0
    grid_spec = pltpu.PrefetchScalarGridSpec(
        num_scalar_prefetch=1, grid=(bsz,),
        in_specs=[pl.BlockSpec((1, MLA_HEADS, MLA_LATENT), lambda b, pt: (b, 0, 0)),
                  pl.BlockSpec((1, 1, MLA_LATENT), lambda b, pt: (b, 0, 0)),
                  pl.BlockSpec(memory_space=pl.ANY)],
        out_specs=pl.BlockSpec((1, MLA_HEADS, MLA_KV_LORA), lambda b, pt: (b, 0, 0)),
        scratch_shapes=[pltpu.VMEM((2, MLA_LATENT, pages_per_chunk * page), F32), pltpu.SemaphoreType.DMA((2,))])
    return pl.pallas_call(
        functools.partial(_mla_decode_kernel, pages_per_chunk, n_pages, page), grid_spec=grid_spec,
        out_shape=jax.ShapeDtypeStruct((bsz, MLA_HEADS, MLA_KV_LORA), F32),
        compiler_params=_cparams(("arbitrary",)), name="mla_decode")(page_table, q, new_rows, cache_t)


def _mla_out_kernel(o_ref, wuv_ref, wo_ref, res_ref, out_ref):
    parts = []
    for p in range(MLA_HEADS // 2):
        pair = jnp.concatenate([o_ref[2 * p], o_ref[2 * p + 1]], axis=1)
        parts.append(_dot(pair, wuv_ref[p]))
    out_ref[...] = res_ref[...] + _dot(jnp.concatenate(parts, axis=1), wo_ref[...])


def _mla_out(o_lat, wuv, wo, res, tm=256):
    m, d = res.shape
    tm = min(tm, m)
    return pl.pallas_call(
        _mla_out_kernel, grid=(m // tm,),
        in_specs=[pl.BlockSpec((MLA_HEADS, tm, MLA_KV_LORA), lambda i: (0, i, 0)), _const_spec(wuv.shape),
                  _const_spec(wo.shape), pl.BlockSpec((tm, d), lambda i: (i, 0))],
        out_specs=pl.BlockSpec((tm, d), lambda i: (i, 0)),
        out_shape=jax.ShapeDtypeStruct((m, d), F32),
        compiler_params=_cparams(("parallel",)), name="mla_out")(o_lat, wuv, wo, res)


def _conv_tail(y, lng, lnb, wout, bout):
    mu = jnp.mean(y, axis=-1, keepdims=True)
    var = jnp.mean(jnp.square(y - mu), axis=-1, keepdims=True)
    yn = (y - mu) * lax.rsqrt(var + LN_EPS) * lng + lnb
    return _dot(_silu(yn), wout) + bout


def _conv_prompt_kernel(ts, x_ref, g_ref, win_ref, bin_ref, wdw_ref, bdw_ref, lng_ref, lnb_ref, wout_ref, bout_ref,
                        o_ref, st_ref, ext):
    j = pl.program_id(1)

    @pl.when(j == 0)
    def _():
        ext[0:CONV_HALO, :] = jnp.zeros((CONV_HALO, D_MODEL), F32)

    x = x_ref[...]
    a = _dot(_rms(x, g_ref[...]), win_ref[...]) + bin_ref[...]
    ext[CONV_HALO:CONV_HALO + ts, :] = a[:, :D_MODEL] * jax.nn.sigmoid(a[:, D_MODEL:])
    base = CONV_HALO - (CONV_WIDTH - 1)
    sub = 8
    y = bdw_ref[...]
    for phase in range(sub):
        taps = [r for r in range(phase, base + CONV_WIDTH, sub) if r >= base]
        span = ts + (sub if phase else 0)
        part = None
        for r in taps:
            term = wdw_ref[r - base:r - base + 1, :] * ext[r - phase:r - phase + span, :]
            part = term if part is None else part + term
        y = y + part[phase:phase + ts]
    o_ref[...] = x + _conv_tail(y, lng_ref[...], lnb_ref[...], wout_ref[...], bout_ref[...])

    @pl.when(j == pl.num_programs(1) - 1)
    def _():
        st_ref[0] = ext[CONV_HALO + ts - (CONV_WIDTH - 1):CONV_HALO + ts, :]

    ext[0:CONV_HALO, :] = ext[ts:ts + CONV_HALO, :]


def _conv_prompt(x, g, w, bsz, seq, ts=256):
    m, d = x.shape
    ns = seq // ts
    consts = [g.reshape(1, d), w["win"], w["bin"], w["wdw"], w["bdw"], w["lng"], w["lnb"], w["wout"], w["bout"]]
    return pl.pallas_call(
        functools.partial(_conv_prompt_kernel, ts), grid=(bsz, ns),
        in_specs=[pl.BlockSpec((ts, d), lambda b, j: (b * ns + j, 0))] + [_const_spec(c.shape) for c in consts],
        out_specs=[pl.BlockSpec((ts, d), lambda b, j: (b * ns + j, 0)),
                   pl.BlockSpec((1, CONV_WIDTH - 1, d), lambda b, j: (b, 0, 0))],
        out_shape=[jax.ShapeDtypeStruct((m, d), F32), jax.ShapeDtypeStruct((bsz, CONV_WIDTH - 1, d), F32)],
        scratch_shapes=[pltpu.VMEM((CONV_HALO + ts, d), F32)],
        compiler_params=_cparams(("parallel", "arbitrary")), name="conv_prompt")(x, *consts)


def _conv_sample_kernel(x_ref, g_ref, win_ref, bin_ref, st_ref, wdw_ref, bdw_ref, lng_ref, lnb_ref, wout_ref,
                        bout_ref, o_ref, u_ref):
    x = x_ref[...]
    a = _dot(_rms(x, g_ref[...]), win_ref[...]) + bin_ref[...]
    u = a[:, :D_MODEL] * jax.nn.sigmoid(a[:, D_MODEL:])
    u_ref[...] = u
    y = jnp.zeros_like(u)
    for k in range(CONV_WIDTH - 1):
        y = y + wdw_ref[k:k + 1, :] * st_ref[k]
    y = y + wdw_ref[CONV_WIDTH - 1:CONV_WIDTH, :] * u + bdw_ref[...]
    o_ref[...] = x + _conv_tail(y, lng_ref[...], lnb_ref[...], wout_ref[...], bout_ref[...])


def _conv_sample(x, g, w, state_t):
    m, d = x.shape
    args = [x, g.reshape(1, d), w["win"], w["bin"], state_t, w["wdw"], w["bdw"], w["lng"], w["lnb"], w["wout"],
            w["bout"]]
    return pl.pallas_call(
        _conv_sample_kernel, grid=(1,),
        in_specs=[_const_spec(a.shape) for a in args],
        out_specs=[_const_spec((m, d)), _const_spec((m, d))],
        out_shape=[jax.ShapeDtypeStruct((m, d), F32), jax.ShapeDtypeStruct((m, d), F32)],
        compiler_params=_cparams(("arbitrary",)), name="conv_sample")(*args)


def _ssm_in_prompt_kernel(ts, x_ref, g_ref, wz_ref, wxbc_ref, wdt_ref, cw_ref, cb_ref, dtb_ref,
                          z_ref, act_ref, dt_ref, cst_ref, ext):
    j = pl.program_id(1)

    @pl.when(j == 0)
    def _():
        ext[0:SSM_HALO, :] = jnp.zeros((SSM_HALO, SSM_CONV_DIM), F32)

    xn = _rms(x_ref[...], g_ref[...]).astype(BF16)
    z_ref[...] = jnp.dot(xn, wz_ref[...], preferred_element_type=F32)
    dt_ref[...] = jax.nn.softplus(jnp.dot(xn, wdt_ref[...], preferred_element_type=F32) + dtb_ref[...])
    ext[SSM_HALO:SSM_HALO + ts, :] = jnp.dot(xn, wxbc_ref[...], preferred_element_type=F32)
    base = SSM_HALO - (SSM_CONV - 1)
    y = jnp.zeros((ts, SSM_CONV_DIM), F32)
    for k in range(SSM_CONV):
        y = y + cw_ref[k:k + 1, :] * ext[base + k:base + k + ts, :]
    act_ref[...] = _silu(y + cb_ref[...])

    @pl.when(j == pl.num_programs(1) - 1)
    def _():
        cst_ref[0] = ext[SSM_HALO + ts - (SSM_CONV - 1):SSM_HALO + ts, :]

    ext[0:SSM_HALO, :] = ext[ts:ts + SSM_HALO, :]


def _ssm_in_prompt(x, g, w, bsz, seq, ts=256):
    m, d = x.shape
    ns = seq // ts
    consts = [g.reshape(1, d), w["wz"], w["wxbc"], w["wdt"], w["cw"], w["cb"], w["dtb"]]
    row = lambda b, j: (b * ns + j, 0)
    return pl.pallas_call(
        functools.partial(_ssm_in_prompt_kernel, ts), grid=(bsz, ns),
        in_specs=[pl.BlockSpec((ts, d), row)] + [_const_spec(c.shape) for c in consts],
        out_specs=[pl.BlockSpec((ts, D_INNER), row), pl.BlockSpec((ts, SSM_CONV_DIM), row),
                   pl.BlockSpec((ts, SSM_HEADS), row),
                   pl.BlockSpec((1, SSM_CONV - 1, SSM_CONV_DIM), lambda b, j: (b, 0, 0))],
        out_shape=[jax.ShapeDtypeStruct((m, D_INNER), F32), jax.ShapeDtypeStruct((m, SSM_CONV_DIM), F32),
                   jax.ShapeDtypeStruct((m, SSM_HEADS), F32),
                   jax.ShapeDtypeStruct((bsz, SSM_CONV - 1, SSM_CONV_DIM), F32)],
        scratch_shapes=[pltpu.VMEM((SSM_HALO + ts, SSM_CONV_DIM), F32)],
        compiler_params=_cparams(("parallel", "arbitrary")), name="ssm_in_prompt")(x, *consts)


def _ssd_scan_kernel(act_ref, dt_ref, alog_ref, d_ref, y_ref, h_ref):
    c = pl.program_id(1)
    L = SSM_CHUNK

    @pl.when(c == 0)
    def _():
        h_ref[...] = jnp.zeros(h_ref.shape, F32)

    dt = dt_ref[...]
    da = dt * (-jnp.exp(alog_ref[...]))
    ri = lax.broadcasted_iota(jnp.int32, (L, L), 0)
    ci = lax.broadcasted_iota(jnp.int32, (L, L), 1)
    tril = ri >= ci
    cs = _dot_exact(tril.astype(F32), da)
    eye = (lax.broadcasted_iota(jnp.int32, (SSM_HEADS, SSM_HEADS), 0)
           == lax.broadcasted_iota(jnp.int32, (SSM_HEADS, SSM_HEADS), 1)).astype(F32)
    cs_t = _dot_nt_exact(eye, cs)
    dt_t = _dot_nt_exact(eye, dt)
    w_t = dt_t * jnp.exp(cs_t[:, L - 1:L] - cs_t)
    cdec = jnp.exp(cs[L - 1:L, :])
    dvec = d_ref[...]
    for g in range(SSM_GROUPS):
        bg = act_ref[:, D_INNER + g * SSM_STATE:D_INNER + (g + 1) * SSM_STATE]
        cg = act_ref[:, D_INNER + SSM_GN + g * SSM_STATE:D_INNER + SSM_GN + (g + 1) * SSM_STATE]
        cb = _dot_nt(cg, bg)
        bg_t = bg.T
        for r8 in range(SSM_REP):
            r = g * SSM_REP + r8
            xr = act_ref[:, r * SSM_HD:(r + 1) * SSM_HD]
            h_t = h_ref[0, r]
            csb = jnp.broadcast_to(cs[:, r:r + 1], (L, L))
            lmat = jnp.exp(jnp.where(tril, csb - cs_t[r:r + 1, :], NEG_INF)) * (cb * dt_t[r:r + 1, :])
            lhs = jnp.concatenate([lmat, cg * jnp.exp(csb)], axis=1)
            rhs = jnp.concatenate([xr, h_t], axis=0)
            y_ref[:, r * SSM_HD:(r + 1) * SSM_HD] = _dot(lhs, rhs) + xr * dvec[:, r:r + 1]
            h_ref[0, r] = cdec[:, r:r + 1] * h_t + _dot(bg_t * w_t[r:r + 1, :], xr)


def _ssd_scan(act, dt, alog, dvec, bsz, seq):
    m = act.shape[0]
    nc = seq // SSM_CHUNK
    row = lambda b, c: (b * nc + c, 0)
    return pl.pallas_call(
        _ssd_scan_kernel, grid=(bsz, nc),
        in_specs=[pl.BlockSpec((SSM_CHUNK, SSM_CONV_DIM), row), pl.BlockSpec((SSM_CHUNK, SSM_HEADS), row),
                  _const_spec((1, SSM_HEADS)), _const_spec((1, SSM_HEADS))],
        out_specs=[pl.BlockSpec((SSM_CHUNK, D_INNER), row),
                   pl.BlockSpec((1, SSM_HEADS, SSM_STATE, SSM_HD), lambda b, c: (b, 0, 0, 0))],
        out_shape=[jax.ShapeDtypeStruct((m, D_INNER), F32),
                   jax.ShapeDtypeStruct((bsz, SSM_HEADS, SSM_STATE, SSM_HD), F32)],
        compiler_params=_cparams(("parallel", "arbitrary")), name="ssd_scan")(
            act, dt, alog.reshape(1, SSM_HEADS), dvec.reshape(1, SSM_HEADS))


def _ssm_out_kernel(y_ref, z_ref, n_ref, w_ref, res_ref, o_ref):
    yz = y_ref[...] * _silu(z_ref[...])
    gw = D_INNER // SSM_GROUPS
    parts = []
    for g in range(SSM_GROUPS):
        seg = yz[:, g * gw:(g + 1) * gw]
        parts.append(seg * lax.rsqrt(jnp.mean(seg * seg, axis=-1, keepdims=True) + RMS_EPS))
    o_ref[...] = res_ref[...] + _dot(jnp.concatenate(parts, axis=1) * n_ref[...], w_ref[...])


def _ssm_out(y, z, norm, wout, res, tm=256):
    m, d = res.shape
    tm = min(tm, m)
    row = lambda i: (i, 0)
    return pl.pallas_call(
        _ssm_out_kernel, grid=(m // tm,),
        in_specs=[pl.BlockSpec((tm, D_INNER), row), pl.BlockSpec((tm, D_INNER), row), _const_spec((1, D_INNER)),
                  _const_spec(wout.shape), pl.BlockSpec((tm, d), row)],
        out_specs=pl.BlockSpec((tm, d), row),
        out_shape=jax.ShapeDtypeStruct((m, d), F32),
        compiler_params=_cparams(("parallel",)), name="ssm_out")(y, z, norm.reshape(1, D_INNER), wout, res)


def _ssm_in_sample_kernel(x_ref, g_ref, wz_ref, wxbc_ref, wdt_ref, cw_ref, cb_ref, dtb_ref, st_ref,
                          z_ref, xbc_ref, act_ref, dt_ref):
    xn = _rms(x_ref[...], g_ref[...]).astype(BF16)
    z_ref[...] = jnp.dot(xn, wz_ref[...], preferred_element_type=F32)
    dt_ref[...] = jax.nn.softplus(jnp.dot(xn, wdt_ref[...], preferred_element_type=F32) + dtb_ref[...])
    xbc = jnp.dot(xn, wxbc_ref[...], preferred_element_type=F32)
    xbc_ref[...] = xbc
    y = jnp.zeros_like(xbc)
    for k in range(SSM_CONV - 1):
        y = y + cw_ref[k:k + 1, :] * st_ref[k]
    y = y + cw_ref[SSM_CONV - 1:SSM_CONV, :] * xbc
    act_ref[...] = _silu(y + cb_ref[...])


def _ssm_in_sample(x, g, w, state_t):
    m, d = x.shape
    args = [x, g.reshape(1, d), w["wz"], w["wxbc"], w["wdt"], w["cw"], w["cb"], w["dtb"], state_t]
    shapes = [(m, D_INNER), (m, SSM_CONV_DIM), (m, SSM_CONV_DIM), (m, SSM_HEADS)]
    return pl.pallas_call(
        _ssm_in_sample_kernel, grid=(1,),
        in_specs=[_const_spec(a.shape) for a in args],
        out_specs=[_const_spec(s) for s in shapes],
        out_shape=[jax.ShapeDtypeStruct(s, F32) for s in shapes],
        compiler_params=_cparams(("arbitrary",)), name="ssm_in_sample")(*args)


def _ssm_step_kernel(act_ref, dt_ref, alog_ref, d_ref, h_ref, y_ref, hn_ref):
    act = act_ref[0]
    dt = dt_ref[0]
    decay = jnp.exp(dt * (-jnp.exp(alog_ref[...])))
    dvec = d_ref[...]
    eye = (lax.broadcasted_iota(jnp.int32, (SSM_HD, SSM_HD), 0)
           == lax.broadcasted_iota(jnp.int32, (SSM_HD, SSM_HD), 1))
    for g in range(SSM_GROUPS):
        bg = act[:, D_INNER + g * SSM_STATE:D_INNER + (g + 1) * SSM_STATE]
        cg = jnp.broadcast_to(act[:, D_INNER + SSM_GN + g * SSM_STATE:D_INNER + SSM_GN + (g + 1) * SSM_STATE],
                              (8, SSM_STATE))
        for r8 in range(SSM_REP):
            r = g * SSM_REP + r8
            xr = act[:, r * SSM_HD:(r + 1) * SSM_HD]
            xdt = xr * dt[:, r:r + 1]
            xcol = jnp.sum(jnp.where(eye, jnp.broadcast_to(xdt, (SSM_HD, SSM_HD)), 0.0), axis=1, keepdims=True)
            h_new = decay[:, r:r + 1] * h_ref[0, r] + xcol * bg
            hn_ref[0, r] = h_new
            y = _dot_nt(cg, h_new)[0:1] + xr * dvec[:, r:r + 1]
            y_ref[0, :, r * SSM_HD:(r + 1) * SSM_HD] = y


def _ssm_step(act, dt, alog, dvec, state):
    bsz = act.shape[0]
    blk3 = lambda b: (b, 0, 0)
    y, hn = pl.pallas_call(
        _ssm_step_kernel, grid=(bsz,),
        in_specs=[pl.BlockSpec((1, 1, SSM_CONV_DIM), blk3), pl.BlockSpec((1, 1, SSM_HEADS), blk3),
                  _const_spec((1, SSM_HEADS)), _const_spec((1, SSM_HEADS)),
                  pl.BlockSpec((1, SSM_HEADS, SSM_HD, SSM_STATE), lambda b: (b, 0, 0, 0))],
        out_specs=[pl.BlockSpec((1, 1, D_INNER), blk3),
                   pl.BlockSpec((1, SSM_HEADS, SSM_HD, SSM_STATE), lambda b: (b, 0, 0, 0))],
        out_shape=[jax.ShapeDtypeStruct((bsz, 1, D_INNER), F32),
                   jax.ShapeDtypeStruct((bsz, SSM_HEADS, SSM_HD, SSM_STATE), F32)],
        compiler_params=_cparams(("parallel",)), name="ssm_step")(
            act.reshape(bsz, 1, SSM_CONV_DIM), dt.reshape(bsz, 1, SSM_HEADS), alog.reshape(1, SSM_HEADS),
            dvec.reshape(1, SSM_HEADS), state)
    return y.reshape(bsz, D_INNER), hn


def _dil_qkv_kernel(x_ref, g_ref, w_ref, cos_ref, sin_ref, q0_ref, q1_ref, q2_ref, kv0_ref, kv1_ref, kv2_ref):
    cos, sin = cos_ref[...], sin_ref[...]
    h = _dot(_rms(x_ref[...], g_ref[...]), w_ref[...])
    tiles = []
    for t in range(2 * DIL_WIDTH // LANE):
        tile = h[:, t * LANE:(t + 1) * LANE]
        tiles.append(tile * cos + _rot_half(tile, DIL_HD) * sin)
    qk = jnp.concatenate(tiles, axis=1)
    for g, (q_ref, kv_ref) in enumerate(((q0_ref, kv0_ref), (q1_ref, kv1_ref), (q2_ref, kv2_ref))):
        q_ref[...] = qk[:, g * DIL_GW:(g + 1) * DIL_GW]
        kv_ref[:, :DIL_GW] = qk[:, DIL_WIDTH + g * DIL_GW:DIL_WIDTH + (g + 1) * DIL_GW]
        kv_ref[:, DIL_GW:] = h[:, 2 * DIL_WIDTH + g * DIL_GW:2 * DIL_WIDTH + (g + 1) * DIL_GW]


def _dil_qkv(x, g, w, cos, sin, tm=256):
    m, d = x.shape
    tm = min(tm, m)
    nt = cos.shape[0] // tm
    row = lambda i: (i, 0)
    widths = [DIL_GW] * N_DIL + [2 * DIL_GW] * N_DIL
    return pl.pallas_call(
        _dil_qkv_kernel, grid=(m // tm,),
        in_specs=[pl.BlockSpec((tm, d), row), _const_spec((1, d)), _const_spec(w.shape)]
        + [pl.BlockSpec((tm, LANE), lambda i: (i % nt, 0))] * 2,
        out_specs=[pl.BlockSpec((tm, wd), row) for wd in widths],
        out_shape=[jax.ShapeDtypeStruct((m, wd), F32) for wd in widths],
        compiler_params=_cparams(("parallel",)), name="dil_qkv")(x, g.reshape(1, d), w, cos, sin)


def _dil_band_kernel(sub_len, q_ref, kv_ref, o_ref, lse_ref):
    nk = DIL_NK
    qi = lax.broadcasted_iota(jnp.int32, (nk, 2 * nk), 0)
    ki = lax.broadcasted_iota(jnp.int32, (nk, 2 * nk), 1)
    band = (ki <= qi + nk) & (ki >= qi)
    causal = lax.broadcasted_iota(jnp.int32, (nk, nk), 1) <= lax.broadcasted_iota(jnp.int32, (nk, nk), 0)
    for blk in range(sub_len // nk):
        rows = pl.ds(blk * nk, nk)
        q = q_ref[0, rows, :]
        if blk == 0:
            kvb, mask = kv_ref[0, rows, :].astype(BF16), causal
        else:
            kvb, mask = kv_ref[0, pl.ds((blk - 1) * nk, 2 * nk), :].astype(BF16), band
        for h in range(DIL_HEADS):
            cols = slice(h * DIL_HD, (h + 1) * DIL_HD)
            s = _dot_nt(q[:, cols], kvb[:, cols]) * DIL_HD ** -0.5
            s = jnp.where(mask, s, NEG_INF)
            mx = jnp.max(s, axis=-1, keepdims=True)
            lse = mx + jnp.log(jnp.sum(jnp.exp(s - mx), axis=-1, keepdims=True))
            o_ref[0, rows, cols] = _dot(jnp.exp(s - lse), kvb[:, DIL_GW + h * DIL_HD:DIL_GW + (h + 1) * DIL_HD])
            lse_ref[0, rows, cols] = jnp.broadcast_to(lse, (nk, DIL_HD))


def _dil_prompt(q, kv, dil, bsz, seq):
    sub_len = seq // dil

    def split(a):
        w = a.shape[-1]
        return a.reshape(bsz, sub_len, dil, w).transpose(0, 2, 1, 3).reshape(bsz * dil, sub_len, w)

    def merge(a):
        return a.reshape(bsz, dil, sub_len, DIL_GW).transpose(0, 2, 1, 3).reshape(bsz * seq, DIL_GW)

    blk = lambda b: (b, 0, 0)
    o, lse = pl.pallas_call(
        functools.partial(_dil_band_kernel, sub_len), grid=(bsz * dil,),
        in_specs=[pl.BlockSpec((1, sub_len, DIL_GW), blk), pl.BlockSpec((1, sub_len, 2 * DIL_GW), blk)],
        out_specs=[pl.BlockSpec((1, sub_len, DIL_GW), blk)] * 2,
        out_shape=[jax.ShapeDtypeStruct((bsz * dil, sub_len, DIL_GW), F32)] * 2,
        compiler_params=_cparams(("parallel",)), name="dil_band_d%d" % dil)(split(q), split(kv))
    return merge(o), merge(lse)


def _dil_combine(outs, lses):
    mx = jnp.maximum(jnp.maximum(lses[0], lses[1]), lses[2])
    es = [jnp.exp(l - mx) for l in lses]
    den = es[0] + es[1] + es[2]
    return [o * (e / den) for o, e in zip(outs, es)]


def _dil_out_kernel(o0_ref, o1_ref, o2_ref, l0_ref, l1_ref, l2_ref, w_ref, res_ref, out_ref):
    parts = _dil_combine([o0_ref[...], o1_ref[...], o2_ref[...]], [l0_ref[...], l1_ref[...], l2_ref[...]])
    y = res_ref[...]
    for g in range(N_DIL):
        y = y + _dot(parts[g], w_ref[g])
    out_ref[...] = y


def _dil_out(outs, lses, w, res, tm=256):
    m, d = res.shape
    tm = min(tm, m)
    row = lambda i: (i, 0)
    return pl.pallas_call(
        _dil_out_kernel, grid=(m // tm,),
        in_specs=[pl.BlockSpec((tm, DIL_GW), row)] * 6 + [_const_spec(w.shape), pl.BlockSpec((tm, d), row)],
        out_specs=pl.BlockSpec((tm, d), row),
        out_shape=jax.ShapeDtypeStruct((m, d), F32),
        compiler_params=_cparams(("parallel",)), name="dil_out")(*outs, *lses, w, res)


def _dil_sample_kernel(q0_ref, q1_ref, q2_ref, n0_ref, n1_ref, n2_ref, b0_ref, b1_ref, b2_ref,
                       o_ref, nb0_ref, nb1_ref, nb2_ref):
    eye = (lax.broadcasted_iota(jnp.int32, (DIL_HD, DIL_HD), 0) == lax.broadcasted_iota(jnp.int32, (DIL_HD, DIL_HD), 1))

    def to_col(row):
        return jnp.sum(jnp.where(eye, jnp.broadcast_to(row, (DIL_HD, DIL_HD)), 0.0), axis=1, keepdims=True)

    def to_row(col):
        return jnp.sum(jnp.where(eye, jnp.broadcast_to(col, (DIL_HD, DIL_HD)), 0.0), axis=0, keepdims=True)

    scale = DIL_HD ** -0.5
    outs, lses = [], []
    for (win, dil), q_ref, n_ref, b_ref, nb_ref in zip(DIL_GROUPS, (q0_ref, q1_ref, q2_ref), (n0_ref, n1_ref, n2_ref),
                                                       (b0_ref, b1_ref, b2_ref), (nb0_ref, nb1_ref, nb2_ref)):
        rows = b_ref.shape[-1]
        lane = lax.broadcasted_iota(jnp.int32, (1, rows), 1)
        reads = (lane & (dil - 1)) == 0
        lane_full = lax.broadcasted_iota(jnp.int32, (DIL_HD, rows), 1)
        q = q_ref[0]
        new = n_ref[0]
        og, lg = [], []
        for h in range(DIL_HEADS):
            cols = slice(h * DIL_HD, (h + 1) * DIL_HD)
            qh, k_new, v_new = q[:, cols], new[:, cols], new[:, DIL_GW + h * DIL_HD:DIL_GW + (h + 1) * DIL_HD]
            k_t, v_t = b_ref[0, 0, h], b_ref[0, 1, h]
            s = jnp.where(reads, jnp.sum(k_t * to_col(qh), axis=0, keepdims=True) * scale, NEG_INF)
            s_new = jnp.sum(qh * k_new, axis=-1, keepdims=True) * scale
            mx = jnp.maximum(jnp.max(s, axis=-1, keepdims=True), s_new)
            lse = mx + jnp.log(jnp.sum(jnp.exp(s - mx), axis=-1, keepdims=True) + jnp.exp(s_new - mx))
            o_col = jnp.sum(v_t * jnp.exp(s - lse), axis=1, keepdims=True)
            og.append(to_row(o_col) + jnp.exp(s_new - lse) * v_new)
            lg.append(jnp.broadcast_to(lse, (1, DIL_HD)))
            last = lane_full == rows - 1
            nb_ref[0, 0, h] = jnp.where(last, to_col(k_new), pltpu.roll(k_t, rows - 1, 1))
            nb_ref[0, 1, h] = jnp.where(last, to_col(v_new), pltpu.roll(v_t, rows - 1, 1))
        outs.append(jnp.concatenate(og, axis=1))
        lses.append(jnp.concatenate(lg, axis=1))
    o_ref[0] = jnp.concatenate(_dil_combine(outs, lses), axis=1)


def _dil_sample(qs, news, bufs_t):
    bsz = qs[0].shape[0]
    blk = lambda b: (b, 0, 0)
    blk5 = lambda b: (b, 0, 0, 0, 0)
    small = [q.reshape(bsz, 1, DIL_GW) for q in qs] + [n.reshape(bsz, 1, 2 * DIL_GW) for n in news]
    buf_specs = [pl.BlockSpec((1,) + b.shape[1:], blk5) for b in bufs_t]
    res = pl.pallas_call(
        _dil_sample_kernel, grid=(bsz,),
        in_specs=[pl.BlockSpec((1,) + a.shape[1:], blk) for a in small] + buf_specs,
        out_specs=[pl.BlockSpec((1, 1, DIL_WIDTH), blk)] + buf_specs,
        out_shape=[jax.ShapeDtypeStruct((bsz, 1, DIL_WIDTH), F32)]
        + [jax.ShapeDtypeStruct(b.shape, F32) for b in bufs_t],
        compiler_params=_cparams(("parallel",)), name="dil_sample")(*small, *bufs_t)
    return res[0].reshape(bsz, DIL_WIDTH), res[1:]


def _block_diag_pairs(w):
    h, k, n = w.shape
    z = jnp.zeros((h // 2, k, n), w.dtype)
    top = jnp.concatenate([w[0::2], z], axis=2)
    bot = jnp.concatenate([z, w[1::2]], axis=2)
    return jnp.concatenate([top, bot], axis=1)


def _prep_mla(mla_w_in, mla_norm_q, mla_w_uq, mla_norm_kv, mla_w_uk, mla_w_uv, mla_w_o):
    pad = jnp.zeros((D_MODEL, LANE - MLA_ROPE), F32)
    wuq = mla_w_uq.reshape(MLA_Q_LORA, MLA_HEADS, MLA_NOPE + MLA_ROPE)
    wuq = jnp.concatenate([wuq[:, :, :MLA_NOPE].reshape(MLA_Q_LORA, -1), wuq[:, :, MLA_NOPE:].reshape(MLA_Q_LORA, -1)],
                          axis=1)
    return {
        "win": jnp.concatenate([mla_w_in, pad], axis=1).astype(BF16),
        "nq": mla_norm_q.reshape(1, -1), "nkv": mla_norm_kv.reshape(1, -1),
        "wuq": wuq.astype(BF16),
        "wuk": _block_diag_pairs(jnp.transpose(mla_w_uk, (1, 2, 0))).astype(BF16),
        "wuv": _block_diag_pairs(jnp.transpose(mla_w_uv, (1, 0, 2))).astype(BF16),
        "wo": _cast_bf16(mla_w_o),
    }


def _prep_conv(conv_w_in, conv_b_in, conv_w_dw, conv_b_dw, conv_ln_g, conv_ln_b, conv_w_out, conv_b_out):
    return {"win": _cast_bf16(conv_w_in), "bin": conv_b_in.reshape(1, -1), "wdw": conv_w_dw,
            "bdw": conv_b_dw.reshape(1, -1), "lng": conv_ln_g.reshape(1, -1), "lnb": conv_ln_b.reshape(1, -1),
            "wout": _cast_bf16(conv_w_out), "bout": conv_b_out.reshape(1, -1)}


def _prep_ssm(ssm_w_in, ssm_conv_w, ssm_conv_b, ssm_dt_bias):
    w = ssm_w_in.astype(BF16)
    return {"wz": w[:, :D_INNER], "wxbc": w[:, D_INNER:D_INNER + SSM_CONV_DIM], "wdt": w[:, D_INNER + SSM_CONV_DIM:],
            "cw": ssm_conv_w, "cb": ssm_conv_b.reshape(1, -1), "dtb": ssm_dt_bias.reshape(1, -1)}


def kernel(x_prompt, x_sample, cache_mla_latent, cache_mem_kv, state_conv, state_ssm_conv, state_ssm, state_dil_kv0, state_dil_kv1, state_dil_kv2, page_table, mem_prompt, norm_ffn1, ffn1_w_gu, ffn1_w_down, norm_mix, norm_cross, norm_mem, cross_w_q, cross_w_kv, cross_w_o, norm_ffn2, ffn2_w_gu, ffn2_w_down, norm_final, mla_w_in, mla_norm_q, mla_w_uq, mla_norm_kv, mla_w_uk, mla_w_uv, mla_w_o, conv_w_in, conv_b_in, conv_w_dw, conv_b_dw, conv_ln_g, conv_ln_b, conv_w_out, conv_b_out, ssm_w_in, ssm_conv_w, ssm_conv_b, ssm_dt_bias, ssm_a_log, ssm_d, ssm_norm, ssm_w_out, dil_w_qkv, dil_w_o):
    bsz, seq, d = x_prompt.shape
    dec = x_sample.shape[0]
    assert x_sample.shape[1] == 1 and d == D_MODEL
    depth = norm_ffn1.shape[0]
    past_len = page_table.shape[1] * cache_mla_latent.shape[1]
    pos_p = jnp.arange(seq, dtype=jnp.int32)
    pos_s = jnp.full((dec,), past_len, dtype=jnp.int32)

    res_p = x_prompt.reshape(bsz * seq, d)
    res_s = x_sample.reshape(dec, d)
    mem_len = mem_prompt.shape[1]
    w1gu, w1d = _cast_bf16(ffn1_w_gu), _cast_bf16(ffn1_w_down)
    w2gu, w2d = _cast_bf16(ffn2_w_gu), _cast_bf16(ffn2_w_down)
    xwq, xwo = _cast_bf16(cross_w_q), _cast_bf16(cross_w_o)
    mem_kv = _memkv(mem_prompt.reshape(bsz * mem_len, d), norm_mem, _cast_bf16(cross_w_kv))
    mem_kv_b = mem_kv.reshape(depth * bsz, mem_len, 2 * d)
    outs = {}

    for i in range(depth):
        kind = i % 4
        res_p = _ffn(res_p, norm_ffn1, w1gu, w1d, i)
        res_s = _ffn(res_s, norm_ffn1, w1gu, w1d, i)
        g_mix = norm_mix[i]
        if kind == 0:
            w = _prep_mla(mla_w_in, mla_norm_q, mla_w_uq, mla_norm_kv, mla_w_uk, mla_w_uv, mla_w_o)
            cos_p, sin_p = _rope_tables(pos_p, MLA_ROPE)
            cos_s, sin_s = _rope_tables(pos_s, MLA_ROPE)
            q_p, lat_p = _mla_proj(res_p, g_mix, w, cos_p, sin_p)
            res_p = _mla_attn_prompt(q_p, lat_p, w["wuv"], w["wo"], res_p, bsz, seq)
            q_s, lat_s = _mla_proj(res_s, g_mix, w, cos_s, sin_s)
            o_s = _mla_attn_sample(jnp.transpose(q_s, (1, 0, 2)), lat_s.reshape(dec, 1, MLA_LATENT),
                                   jnp.transpose(cache_mla_latent, (0, 2, 1)), page_table)
            res_s = _mla_out(jnp.transpose(o_s, (1, 0, 2)), w["wuv"], w["wo"], res_s)
            outs["mla_p"] = lat_p.reshape(bsz, seq, MLA_LATENT)
            outs["mla_s"] = lat_s.reshape(dec, 1, MLA_LATENT)
        elif kind == 1:
            w = _prep_conv(conv_w_in, conv_b_in, conv_w_dw, conv_b_dw, conv_ln_g, conv_ln_b, conv_w_out, conv_b_out)
            res_p, outs["conv_p"] = _conv_prompt(res_p, g_mix, w, bsz, seq)
            res_s, u_s = _conv_sample(res_s, g_mix, w, jnp.transpose(state_conv, (1, 0, 2)))
            outs["conv_s"] = jnp.concatenate([state_conv[:, 1:], u_s[:, None, :]], axis=1)
        elif kind == 2:
            w = _prep_ssm(ssm_w_in, ssm_conv_w, ssm_conv_b, ssm_dt_bias)
            wout = _cast_bf16(ssm_w_out)
            z_p, act_p, dt_p, outs["ssmc_p"] = _ssm_in_prompt(res_p, g_mix, w, bsz, seq)
            y_p, h_t = _ssd_scan(act_p, dt_p, ssm_a_log, ssm_d, bsz, seq)
            outs["ssm_p"] = jnp.swapaxes(h_t, 2, 3)
            res_p = _ssm_out(y_p, z_p, ssm_norm, wout, res_p)
            z_s, xbc_s, act_s, dt_s = _ssm_in_sample(res_s, g_mix, w, jnp.transpose(state_ssm_conv, (1, 0, 2)))
            outs["ssmc_s"] = jnp.concatenate([state_ssm_conv[:, 1:], xbc_s[:, None, :]], axis=1)
            y_s, outs["ssm_s"] = _ssm_step(act_s, dt_s, ssm_a_log, ssm_d, state_ssm)
            res_s = _ssm_out(y_s, z_s, ssm_norm, wout, res_s)
        else:
            wqkv = _cast_bf16(dil_w_qkv)
            wo2 = _cast_bf16(dil_w_o)
            wo = wo2.reshape(N_DIL, DIL_GW, d)
            cos_p, sin_p = _rope_tables(pos_p, DIL_HD)
            cos_s, sin_s = _rope_tables(pos_s, DIL_HD)
            qkv_p = _dil_qkv(res_p, g_mix, wqkv, cos_p, sin_p)
            qkv_s = _dil_qkv(res_s, g_mix, wqkv, cos_s, sin_s)
            bufs = (state_dil_kv0, state_dil_kv1, state_dil_kv2)
            o_list, l_list = [], []
            for g, (win, dil) in enumerate(DIL_GROUPS):
                kv3 = qkv_p[N_DIL + g].reshape(bsz, seq, 2 * DIL_GW)
                o, l = _dil_prompt(qkv_p[g], qkv_p[N_DIL + g], dil, bsz, seq)
                o_list.append(o)
                l_list.append(l)
                keep = min(win, seq)
                outs["dil%d_p" % g] = kv3[:, seq - keep:].reshape(bsz, keep, 2, DIL_HEADS, DIL_HD)
                assert bufs[g].shape[1] == win
            res_p = _dil_out(o_list, l_list, wo, res_p)
            o_s, new_bufs = _dil_sample(qkv_s[:N_DIL], qkv_s[N_DIL:], [jnp.transpose(b, (0, 2, 3, 4, 1)) for b in bufs])
            for g in range(N_DIL):
                outs["dil%d_s" % g] = jnp.transpose(new_bufs[g], (0, 4, 1, 2, 3))
            res_s = _proj_res([o_s], [wo2], res_s, name="dil_out_sample")

        res_p = _xattn_prompt(res_p, norm_cross, xwq, mem_kv_b, xwo, i, bsz, seq)
        (q_s,) = _norm_proj(res_s, norm_cross[i], [xwq[i]], name="xattn_q_sample")
        a_s = _xattn_sample(q_s, cache_mem_kv, i)
        res_s = _proj_res([a_s], [xwo[i]], res_s, name="xattn_o_sample")

        final_g = norm_final if i == depth - 1 else None
        res_p = _ffn(res_p, norm_ffn2, w2gu, w2d, i, final_g)
        res_s = _ffn(res_s, norm_ffn2, w2gu, w2d, i, final_g)

    return (res_p.reshape(bsz, seq, d), res_s.reshape(dec, 1, d),
            outs["mla_p"], outs["mla_s"], outs["conv_p"], outs["conv_s"], outs["ssmc_p"], outs["ssmc_s"],
            outs["ssm_p"], outs["ssm_s"],
            outs["dil0_p"], outs["dil0_s"], outs["dil1_p"], outs["dil1_s"], outs["dil2_p"], outs["dil2_s"],
            mem_kv.reshape(depth, bsz, mem_len, 2, MEM_HEADS, MEM_HD))
```

```python
import functools
import math

import jax
import jax.numpy as jnp
from jax import lax
from jax.experimental import pallas as pl
from jax.experimental.pallas import tpu as pltpu

F32 = jnp.float32
BF16 = jnp.bfloat16

D_MODEL = 1024
D_FF = 2816
RMS_EPS = 1e-6
LN_EPS = 1e-5
ROPE_THETA = 10000.0

MLA_HEADS = 16
MLA_Q_LORA = 256
MLA_KV_LORA = 128
MLA_NOPE = 64
MLA_ROPE = 32
MLA_V_HD = 64
MLA_LATENT = MLA_KV_LORA + MLA_ROPE
MLA_SCALE = (MLA_NOPE + MLA_ROPE) ** -0.5

CONV_WIDTH = 31
CONV_HALO = 32

D_INNER = 2 * D_MODEL
SSM_HD = 64
SSM_HEADS = D_INNER // SSM_HD
SSM_GROUPS = 4
SSM_REP = SSM_HEADS // SSM_GROUPS
SSM_STATE = 128
SSM_CONV = 4
SSM_CHUNK = 128
SSM_GN = SSM_GROUPS * SSM_STATE
SSM_CONV_DIM = D_INNER + 2 * SSM_GN
SSM_HALO = 8

DIL_GROUPS = ((128, 1), (512, 4), (2048, 16))
N_DIL = 3
DIL_HEADS = 5
DIL_HD = 64
DIL_GW = DIL_HEADS * DIL_HD
DIL_WIDTH = N_DIL * DIL_GW
DIL_NK = 128

MEM_HEADS = 4
MEM_HD = D_MODEL // MEM_HEADS

LANE = 128
NEG_INF = float("-inf")
LOG2E = 1.4426950408889634


def _cparams(sem, vmem_mb=48):
    return pltpu.CompilerParams(dimension_semantics=sem, vmem_limit_bytes=vmem_mb * 1024 * 1024)


def _const_spec(shape):
    nd = len(shape)
    return pl.BlockSpec(shape, lambda *_: (0,) * nd, pipeline_mode=pl.Buffered(1))


def _dot(a, b):
    return jnp.dot(a.astype(BF16), b.astype(BF16), preferred_element_type=F32)


def _dot_nt(a, b):
    return lax.dot_general(a.astype(BF16), b.astype(BF16), (((1,), (1,)), ((), ())),
                           preferred_element_type=F32)


def _dot_exact(a, b):
    return jnp.dot(a, b, preferred_element_type=F32, precision=lax.Precision.HIGHEST)


def _dot_nt_exact(a, b):
    return lax.dot_general(a, b, (((1,), (1,)), ((), ())), preferred_element_type=F32,
                           precision=lax.Precision.HIGHEST)


def _rms(x, g):
    return x * lax.rsqrt(jnp.mean(x * x, axis=-1, keepdims=True) + RMS_EPS) * g


def _silu(x):
    return x * jax.nn.sigmoid(x)


def _rot_half(t, dim):
    half = dim // 2
    lane = lax.broadcasted_iota(jnp.int32, t.shape, 1)
    return jnp.where((lane & (dim - 1)) < half, pltpu.roll(t, LANE - half, 1), pltpu.roll(t, half, 1))


def _rope_tables(pos, dim):
    half = dim // 2
    inv_freq = jnp.exp(-math.log(ROPE_THETA) * (2.0 / dim) * jnp.arange(half, dtype=F32))
    ang = pos.astype(F32)[:, None] * inv_freq[None, :]
    cos, sin = jnp.cos(ang), jnp.sin(ang)
    reps = LANE // dim
    return (jnp.tile(jnp.concatenate([cos, cos], axis=1), (1, reps)),
            jnp.tile(jnp.concatenate([-sin, sin], axis=1), (1, reps)))


def _softmax_rows(s):
    m = jnp.max(s, axis=-1, keepdims=True)
    p = jnp.exp(s - m)
    return p / jnp.sum(p, axis=-1, keepdims=True)


def _cast_kernel(x_ref, o_ref):
    o_ref[...] = x_ref[...].astype(BF16)


def _cast_bf16(w, rows=256):
    shape = w.shape
    w2 = w.reshape(-1, shape[-1])
    m, n = w2.shape
    rows = min(rows, m)
    out = pl.pallas_call(
        _cast_kernel, grid=(pl.cdiv(m, rows),),
        in_specs=[pl.BlockSpec((rows, n), lambda i: (i, 0))],
        out_specs=pl.BlockSpec((rows, n), lambda i: (i, 0)),
        out_shape=jax.ShapeDtypeStruct((m, n), BF16),
        compiler_params=_cparams(("parallel",)), name="cast_bf16")(w2)
    return out.reshape(shape)


def _layer_spec(shape, layer):
    nd = len(shape)
    return pl.BlockSpec((1,) + tuple(shape[1:]), lambda *_: (layer,) + (0,) * (nd - 1), pipeline_mode=pl.Buffered(1))


def _ffn_kernel(has_final, x_ref, g_ref, wgu_ref, wd_ref, *rest):
    o_ref = rest[-1]
    x = x_ref[...]
    h = _dot(_rms(x, g_ref[0]), wgu_ref[0])
    act = _silu(h[:, :D_FF]) * h[:, D_FF:]
    y = x + 0.5 * _dot(act, wd_ref[0])
    if has_final:
        y = _rms(y, rest[0][...])
    o_ref[...] = y


def _ffn(x, g, wgu, wd, layer, final_g=None, tm=512):
    m, d = x.shape
    tm = min(tm, m)
    g = g.reshape(g.shape[0], 1, d)
    args = [x, g, wgu, wd]
    in_specs = [pl.BlockSpec((tm, d), lambda i: (i, 0)), _layer_spec(g.shape, layer),
                _layer_spec(wgu.shape, layer), _layer_spec(wd.shape, layer)]
    if final_g is not None:
        args.append(final_g.reshape(1, d))
        in_specs.append(_const_spec((1, d)))
    return pl.pallas_call(
        functools.partial(_ffn_kernel, final_g is not None),
        grid=(m // tm,), in_specs=in_specs,
        out_specs=pl.BlockSpec((tm, d), lambda i: (i, 0)),
        out_shape=jax.ShapeDtypeStruct((m, d), F32),
        compiler_params=_cparams(("parallel",), 56), name="ffn")(*args)


def _norm_proj_kernel(n, has_bias, x_ref, g_ref, *refs):
    xn = _rms(x_ref[...], g_ref[...]).astype(BF16)
    for i in range(n):
        y = jnp.dot(xn, refs[i][...], preferred_element_type=F32)
        if has_bias:
            y = y + refs[n + i][...]
        refs[-n + i][...] = y


def _norm_proj(x, g, ws, bs=None, tm=256, name="norm_proj"):
    m, d = x.shape
    tm = min(tm, m)
    n = len(ws)
    args = [x, g.reshape(1, d)] + list(ws)
    in_specs = [pl.BlockSpec((tm, d), lambda i: (i, 0)), _const_spec((1, d))] + [_const_spec(w.shape) for w in ws]
    if bs is not None:
        args += [b.reshape(1, -1) for b in bs]
        in_specs += [_const_spec((1, b.shape[-1])) for b in bs]
    outs = pl.pallas_call(
        functools.partial(_norm_proj_kernel, n, bs is not None),
        grid=(m // tm,), in_specs=in_specs,
        out_specs=[pl.BlockSpec((tm, w.shape[1]), lambda i: (i, 0)) for w in ws],
        out_shape=[jax.ShapeDtypeStruct((m, w.shape[1]), F32) for w in ws],
        compiler_params=_cparams(("parallel",)), name=name)(*args)
    return outs


def _proj_res_kernel(n, has_bias, *refs):
    a_refs, w_refs = refs[:n], refs[n:2 * n]
    res_ref, o_ref = refs[2 * n], refs[-1]
    y = res_ref[...]
    for a_ref, w_ref in zip(a_refs, w_refs):
        y = y + _dot(a_ref[...], w_ref[...])
    if has_bias:
        y = y + refs[2 * n + 1][...]
    o_ref[...] = y


def _proj_res(a_list, w_list, res, bias=None, tm=256, name="proj_res"):
    m, d = res.shape
    tm = min(tm, m)
    n = len(a_list)
    args = list(a_list) + list(w_list) + [res]
    in_specs = ([pl.BlockSpec((tm, a.shape[1]), lambda i: (i, 0)) for a in a_list]
                + [_const_spec(w.shape) for w in w_list] + [pl.BlockSpec((tm, d), lambda i: (i, 0))])
    if bias is not None:
        args.append(bias.reshape(1, d))
        in_specs.append(_const_spec((1, d)))
    return pl.pallas_call(
        functools.partial(_proj_res_kernel, n, bias is not None),
        grid=(m // tm,), in_specs=in_specs,
        out_specs=pl.BlockSpec((tm, d), lambda i: (i, 0)),
        out_shape=jax.ShapeDtypeStruct((m, d), F32),
        compiler_params=_cparams(("parallel",)), name=name)(*args)


def _memkv_kernel(x_ref, g_ref, w_ref, o_ref):
    o_ref[0] = _dot(_rms(x_ref[...], g_ref[0]), w_ref[0])


def _memkv(mem2d, norm_mem, w_kv, tm=256):
    m, d = mem2d.shape
    depth = w_kv.shape[0]
    n = w_kv.shape[2]
    return pl.pallas_call(
        _memkv_kernel, grid=(depth, m // tm),
        in_specs=[pl.BlockSpec((tm, d), lambda l, i: (i, 0)),
                  pl.BlockSpec((1, 1, d), lambda l, i: (l, 0, 0)),
                  pl.BlockSpec((1, d, n), lambda l, i: (l, 0, 0))],
        out_specs=pl.BlockSpec((1, tm, n), lambda l, i: (l, i, 0)),
        out_shape=jax.ShapeDtypeStruct((depth, m, n), F32),
        compiler_params=_cparams(("parallel", "parallel")), name="memkv")(mem2d, norm_mem.reshape(depth, 1, d), w_kv)


def _xattn_core(q, kv):
    kvb = kv.astype(BF16)
    outs = []
    for h in range(MEM_HEADS):
        kh = kvb[:, h * MEM_HD:(h + 1) * MEM_HD]
        vh = kvb[:, D_MODEL + h * MEM_HD:D_MODEL + (h + 1) * MEM_HD]
        s = _dot_nt(q[:, h * MEM_HD:(h + 1) * MEM_HD], kh) * MEM_HD ** -0.5
        outs.append(_dot(_softmax_rows(s), vh))
    return jnp.concatenate(outs, axis=1)


def _xattn_prompt_kernel(x_ref, g_ref, wq_ref, kv_ref, wo_ref, o_ref):
    x = x_ref[...]
    q = _dot(_rms(x, g_ref[0]), wq_ref[0])
    o_ref[...] = x + _dot(_xattn_core(q, kv_ref[0]), wo_ref[0])


def _xattn_prompt(x, g, wq, kv, wo, layer, bsz, seq, tq=512):
    m, d = x.shape
    nq = seq // tq
    _, ml, kvw = kv.shape
    g = g.reshape(g.shape[0], 1, d)
    return pl.pallas_call(
        _xattn_prompt_kernel, grid=(bsz, nq),
        in_specs=[pl.BlockSpec((tq, d), lambda b, i: (b * nq + i, 0)), _layer_spec(g.shape, layer),
                  _layer_spec(wq.shape, layer),
                  pl.BlockSpec((1, ml, kvw), lambda b, i: (layer * bsz + b, 0, 0)), _layer_spec(wo.shape, layer)],
        out_specs=pl.BlockSpec((tq, d), lambda b, i: (b * nq + i, 0)),
        out_shape=jax.ShapeDtypeStruct((m, d), F32),
        compiler_params=_cparams(("parallel", "parallel")), name="xattn_prompt")(x, g, wq, kv, wo)


def _xattn_sample_kernel(q_ref, kv_ref, o_ref):
    q = q_ref[0]
    k = kv_ref[0, 0, :, 0]
    v = kv_ref[0, 0, :, 1]
    s = jnp.sum(k * q[None], axis=-1, keepdims=True) * MEM_HD ** -0.5
    p = jnp.exp(s - jnp.max(s, axis=0, keepdims=True))
    p = p / jnp.sum(p, axis=0, keepdims=True)
    o_ref[0] = jnp.sum(p * v, axis=0)


def _xattn_sample(q, cache, layer):
    bsz, d = q.shape
    out = pl.pallas_call(
        _xattn_sample_kernel, grid=(bsz,),
        in_specs=[pl.BlockSpec((1, MEM_HEADS, MEM_HD), lambda b: (b, 0, 0)),
                  pl.BlockSpec((1, 1) + cache.shape[2:], lambda b: (layer, b, 0, 0, 0, 0))],
        out_specs=pl.BlockSpec((1, MEM_HEADS, MEM_HD), lambda b: (b, 0, 0)),
        out_shape=jax.ShapeDtypeStruct((bsz, MEM_HEADS, MEM_HD), F32),
        compiler_params=_cparams(("parallel",)), name="xattn_sample")(q.reshape(bsz, MEM_HEADS, MEM_HD), cache)
    return out.reshape(bsz, d)


def _mla_proj_kernel(x_ref, g_ref, win_ref, nq_ref, wuq_ref, nkv_ref, wuk_ref, cos_ref, sin_ref, q_ref, lat_ref):
    cos, sin = cos_ref[...], sin_ref[...]
    a = _dot(_rms(x_ref[...], g_ref[...]), win_ref[...])
    ckv = _rms(a[:, MLA_Q_LORA:MLA_Q_LORA + MLA_KV_LORA], nkv_ref[...])
    pe = a[:, MLA_Q_LORA + MLA_KV_LORA:]
    kpe = pe * cos + _rot_half(pe, MLA_ROPE) * sin
    lat_ref[:, :MLA_KV_LORA] = ckv
    lat_ref[:, MLA_KV_LORA:] = kpe[:, :MLA_ROPE]
    q = _dot(_rms(a[:, :MLA_Q_LORA], nq_ref[...]), wuq_ref[...])
    n_nope = MLA_HEADS * MLA_NOPE
    for p in range(MLA_HEADS // 2):
        ql = _dot(q[:, p * LANE:(p + 1) * LANE], wuk_ref[p])
        q_ref[2 * p, :, :MLA_KV_LORA] = ql[:, :MLA_KV_LORA]
        q_ref[2 * p + 1, :, :MLA_KV_LORA] = ql[:, MLA_KV_LORA:]
    per_tile = LANE // MLA_ROPE
    for t in range(MLA_HEADS // per_tile):
        tile = q[:, n_nope + t * LANE:n_nope + (t + 1) * LANE]
        qpe = tile * cos + _rot_half(tile, MLA_ROPE) * sin
        for j in range(per_tile):
            q_ref[t * per_tile + j, :, MLA_KV_LORA:] = qpe[:, j * MLA_ROPE:(j + 1) * MLA_ROPE]


def _mla_proj(x, g, w, cos, sin, tm=256):
    m, d = x.shape
    tm = min(tm, m)
    nt = cos.shape[0] // tm
    consts = [g.reshape(1, d), w["win"], w["nq"], w["wuq"], w["nkv"], w["wuk"]]
    return pl.pallas_call(
        _mla_proj_kernel, grid=(m // tm,),
        in_specs=([pl.BlockSpec((tm, d), lambda i: (i, 0))] + [_const_spec(c.shape) for c in consts]
                  + [pl.BlockSpec((tm, LANE), lambda i: (i % nt, 0))] * 2),
        out_specs=[pl.BlockSpec((MLA_HEADS, tm, MLA_LATENT), lambda i: (0, i, 0)),
                   pl.BlockSpec((tm, MLA_LATENT), lambda i: (i, 0))],
        out_shape=[jax.ShapeDtypeStruct((MLA_HEADS, m, MLA_LATENT), F32),
                   jax.ShapeDtypeStruct((m, MLA_LATENT), F32)],
        compiler_params=_cparams(("parallel",)), name="mla_proj")(x, *consts, cos, sin)


def _mla_attn_kernel(tq, tk, q_ref, lat_ref, wuv_ref, wo_ref, res_ref, o_ref):
    i = pl.program_id(1)
    rows = MLA_HEADS * tq
    q = (q_ref[...].reshape(rows, MLA_LATENT) * (MLA_SCALE * LOG2E)).astype(BF16)

    def chunk(c, carry, diagonal):
        m, l, acc = carry
        keys = lat_ref[pl.ds(pl.multiple_of(c * tk, tk), tk), :].astype(BF16)
        s = _dot_nt(q, keys)
        if diagonal:
            row = lax.broadcasted_iota(jnp.int32, (rows, tk), 0)
            col = lax.broadcasted_iota(jnp.int32, (rows, tk), 1)
            s = jnp.where(col - (row & (tq - 1)) <= i * tq - c * tk, s, NEG_INF)
        m_new = jnp.maximum(m, jnp.max(s, axis=-1, keepdims=True))
        alpha = jnp.exp2(m - m_new)
        p = jnp.exp2(s - m_new)
        l = alpha * l + jnp.sum(p, axis=-1, keepdims=True)
        acc = alpha * acc + _dot(p, keys[:, :MLA_KV_LORA])
        return m_new, l, acc

    init = (jnp.full((rows, 1), NEG_INF, F32), jnp.zeros((rows, 1), F32), jnp.zeros((rows, MLA_KV_LORA), F32))
    n_full = (i * tq) // tk
    carry = lax.fori_loop(0, n_full, lambda c, cr: chunk(c, cr, False), init)
    _, l, acc = chunk(n_full, carry, True)
    o = acc / l
    parts = []
    for p in range(MLA_HEADS // 2):
        pair = jnp.concatenate([o[(2 * p) * tq:(2 * p + 1) * tq], o[(2 * p + 1) * tq:(2 * p + 2) * tq]], axis=1)
        parts.append(_dot(pair, wuv_ref[p]))
    o_ref[...] = res_ref[...] + _dot(jnp.concatenate(parts, axis=1), wo_ref[...])


def _mla_attn_prompt(qabs, lat, wuv, wo, res, bsz, seq, tq=128, tk=512):
    m, d = res.shape
    nq = seq // tq
    assert tk % tq == 0
    return pl.pallas_call(
        functools.partial(_mla_attn_kernel, tq, tk), grid=(bsz, nq),
        in_specs=[pl.BlockSpec((MLA_HEADS, tq, MLA_LATENT), lambda b, i: (0, b * nq + i, 0)),
                  pl.BlockSpec((seq, MLA_LATENT), lambda b, i: (b, 0)), _const_spec(wuv.shape), _const_spec(wo.shape),
                  pl.BlockSpec((tq, d), lambda b, i: (b * nq + i, 0))],
        out_specs=pl.BlockSpec((tq, d), lambda b, i: (b * nq + i, 0)),
        out_shape=jax.ShapeDtypeStruct((m, d), F32),
        compiler_params=_cparams(("parallel", "parallel")), name="mla_attn")(qabs, lat, wuv, wo, res)


def _mla_decode_kernel(pages_per_chunk, n_pages, page, pt_ref, q_ref, new_ref, cache_ref, o_ref, buf, sem):
    b = pl.program_id(0)
    n_chunks = n_pages // pages_per_chunk

    def copies(seq_idx, c, slot):
        return [pltpu.make_async_copy(cache_ref.at[pt_ref[seq_idx, c * pages_per_chunk + j]],
                                      buf.at[slot, :, pl.ds(j * page, page)], sem.at[slot])
                for j in range(pages_per_chunk)]

    @pl.when(b == 0)
    def _():
        for cp in copies(0, 0, 0):
            cp.start()

    q = (q_ref[0] * (MLA_SCALE * LOG2E)).astype(BF16)
    m = jnp.full((MLA_HEADS, 1), NEG_INF, F32)
    l = jnp.zeros((MLA_HEADS, 1), F32)
    acc = jnp.zeros((MLA_HEADS, MLA_KV_LORA), F32)
    for c in range(n_chunks):
        slot = c % 2
        if c + 1 < n_chunks:
            for cp in copies(b, c + 1, 1 - slot):
                cp.start()
        else:
            @pl.when(b + 1 < pl.num_programs(0))
            def _():
                for cp in copies(b + 1, 0, 1 - slot):
                    cp.start()
        for cp in copies(b, c, slot):
            cp.wait()
        keys_t = buf[slot].astype(BF16)
        s = _dot(q, keys_t)
        m_new = jnp.maximum(m, jnp.max(s, axis=-1, keepdims=True))
        alpha = jnp.exp2(m - m_new)
        p = jnp.exp2(s - m_new)
        l = alpha * l + jnp.sum(p, axis=-1, keepdims=True)
        acc = alpha * acc + _dot_nt(p, keys_t[:MLA_KV_LORA, :])
        m = m_new
    new = new_ref[0].astype(BF16).astype(F32)
    s_new = jnp.sum(q.astype(F32) * new, axis=-1, keepdims=True)
    m_new = jnp.maximum(m, s_new)
    alpha = jnp.exp2(m - m_new)
    p_new = jnp.exp2(s_new - m_new)
    l = alpha * l + p_new
    acc = alpha * acc + p_new.astype(BF16).astype(F32) * new[:, :MLA_KV_LORA]
    o_ref[0] = acc / l


def _mla_attn_sample(q, new_rows, cache_t, page_table, pages_per_chunk=32):
    bsz = q.shape[0]
    n_pages = page_table.shape[1]
    page = cache_t.shape[2]
    pages_per_chunk = min(pages_per_chunk, n_pages // 2)
    assert n_pages % (2 * pages_per_chunk) == 0
    grid_spec = pltpu.PrefetchScalarGridSpec(
        num_scalar_prefetch=1, grid=(bsz,),
        in_specs=[pl.BlockSpec((1, MLA_HEADS, MLA_LATENT), lambda b, pt: (b, 0, 0)),
                  pl.BlockSpec((1, 1, MLA_LATENT), lambda b, pt: (b, 0, 0)),
                  pl.BlockSpec(memory_space=pl.ANY)],
        out_specs=pl.BlockSpec((1, MLA_HEADS, MLA_KV_LORA), lambda b, pt: (b, 0, 0)),
        scratch_shapes=[pltpu.VMEM((2, MLA_LATENT, pages_per_chunk * page), F32), pltpu.SemaphoreType.DMA((2,))])
    return pl.pallas_call(
        functools.partial(_mla_decode_kernel, pages_per_chunk, n_pages, page), grid_spec=grid_spec,
        out_shape=jax.ShapeDtypeStruct((bsz, MLA_HEADS, MLA_KV_LORA), F32),
        compiler_params=_cparams(("arbitrary",)), name="mla_decode")(page_table, q, new_rows, cache_t)


def _mla_out_kernel(o_ref, wuv_ref, wo_ref, res_ref, out_ref):
    parts = []
    for p in range(MLA_HEADS // 2):
        pair = jnp.concatenate([o_ref[2 * p], o_ref[2 * p + 1]], axis=1)
        parts.append(_dot(pair, wuv_ref[p]))
    out_ref[...] = res_ref[...] + _dot(jnp.concatenate(parts, axis=1), wo_ref[...])


def _mla_out(o_lat, wuv, wo, res, tm=256):
    m, d = res.shape
    tm = min(tm, m)
    return pl.pallas_call(
        _mla_out_kernel, grid=(m // tm,),
        in_specs=[pl.BlockSpec((MLA_HEADS, tm, MLA_KV_LORA), lambda i: (0, i, 0)), _const_spec(wuv.shape),
                  _const_spec(wo.shape), pl.BlockSpec((tm, d), lambda i: (i, 0))],
        out_specs=pl.BlockSpec((tm, d), lambda i: (i, 0)),
        out_shape=jax.ShapeDtypeStruct((m, d), F32),
        compiler_params=_cparams(("parallel",)), name="mla_out")(o_lat, wuv, wo, res)


def _conv_tail(y, lng, lnb, wout, bout):
    mu = jnp.mean(y, axis=-1, keepdims=True)
    var = jnp.mean(jnp.square(y - mu), axis=-1, keepdims=True)
    yn = (y - mu) * lax.rsqrt(var + LN_EPS) * lng + lnb
    return _dot(_silu(yn), wout) + bout


def _conv_prompt_kernel(ts, x_ref, g_ref, win_ref, bin_ref, wdw_ref, bdw_ref, lng_ref, lnb_ref, wout_ref, bout_ref,
                        o_ref, st_ref, ext):
    j = pl.program_id(1)

    @pl.when(j == 0)
    def _():
        ext[0:CONV_HALO, :] = jnp.zeros((CONV_HALO, D_MODEL), F32)

    x = x_ref[...]
    a = _dot(_rms(x, g_ref[...]), win_ref[...]) + bin_ref[...]
    ext[CONV_HALO:CONV_HALO + ts, :] = a[:, :D_MODEL] * jax.nn.sigmoid(a[:, D_MODEL:])
    base = CONV_HALO - (CONV_WIDTH - 1)
    sub = 8
    y = bdw_ref[...]
    for phase in range(sub):
        taps = [r for r in range(phase, base + CONV_WIDTH, sub) if r >= base]
        span = ts + (sub if phase else 0)
        part = None
        for r in taps:
            term = wdw_ref[r - base:r - base + 1, :] * ext[r - phase:r - phase + span, :]
            part = term if part is None else part + term
        y = y + part[phase:phase + ts]
    o_ref[...] = x + _conv_tail(y, lng_ref[...], lnb_ref[...], wout_ref[...], bout_ref[...])

    @pl.when(j == pl.num_programs(1) - 1)
    def _():
        st_ref[0] = ext[CONV_HALO + ts - (CONV_WIDTH - 1):CONV_HALO + ts, :]

    ext[0:CONV_HALO, :] = ext[ts:ts + CONV_HALO, :]


def _conv_prompt(x, g, w, bsz, seq, ts=256):
    m, d = x.shape
    ns = seq // ts
    consts = [g.reshape(1, d), w["win"], w["bin"], w["wdw"], w["bdw"], w["lng"], w["lnb"], w["wout"], w["bout"]]
    return pl.pallas_call(
        functools.partial(_conv_prompt_kernel, ts), grid=(bsz, ns),
        in_specs=[pl.BlockSpec((ts, d), lambda b, j: (b * ns + j, 0))] + [_const_spec(c.shape) for c in consts],
        out_specs=[pl.BlockSpec((ts, d), lambda b, j: (b * ns + j, 0)),
                   pl.BlockSpec((1, CONV_WIDTH - 1, d), lambda b, j: (b, 0, 0))],
        out_shape=[jax.ShapeDtypeStruct((m, d), F32), jax.ShapeDtypeStruct((bsz, CONV_WIDTH - 1, d), F32)],
        scratch_shapes=[pltpu.VMEM((CONV_HALO + ts, d), F32)],
        compiler_params=_cparams(("parallel", "arbitrary")), name="conv_prompt")(x, *consts)


def _conv_sample_kernel(x_ref, g_ref, win_ref, bin_ref, st_ref, wdw_ref, bdw_ref, lng_ref, lnb_ref, wout_ref,
                        bout_ref, o_ref, u_ref):
    x = x_ref[...]
    a = _dot(_rms(x, g_ref[...]), win_ref[...]) + bin_ref[...]
    u = a[:, :D_MODEL] * jax.nn.sigmoid(a[:, D_MODEL:])
    u_ref[...] = u
    y = jnp.zeros_like(u)
    for k in range(CONV_WIDTH - 1):
        y = y + wdw_ref[k:k + 1, :] * st_ref[k]
    y = y + wdw_ref[CONV_WIDTH - 1:CONV_WIDTH, :] * u + bdw_ref[...]
    o_ref[...] = x + _conv_tail(y, lng_ref[...], lnb_ref[...], wout_ref[...], bout_ref[...])


def _conv_sample(x, g, w, state_t):
    m, d = x.shape
    args = [x, g.reshape(1, d), w["win"], w["bin"], state_t, w["wdw"], w["bdw"], w["lng"], w["lnb"], w["wout"],
            w["bout"]]
    return pl.pallas_call(
        _conv_sample_kernel, grid=(1,),
        in_specs=[_const_spec(a.shape) for a in args],
        out_specs=[_const_spec((m, d)), _const_spec((m, d))],
        out_shape=[jax.ShapeDtypeStruct((m, d), F32), jax.ShapeDtypeStruct((m, d), F32)],
        compiler_params=_cparams(("arbitrary",)), name="conv_sample")(*args)


def _ssm_in_prompt_kernel(ts, x_ref, g_ref, wz_ref, wxbc_ref, wdt_ref, cw_ref, cb_ref, dtb_ref,
                          z_ref, act_ref, dt_ref, cst_ref, ext):
    j = pl.program_id(1)

    @pl.when(j == 0)
    def _():
        ext[0:SSM_HALO, :] = jnp.zeros((SSM_HALO, SSM_CONV_DIM), F32)

    xn = _rms(x_ref[...], g_ref[...]).astype(BF16)
    z_ref[...] = jnp.dot(xn, wz_ref[...], preferred_element_type=F32)
    dt_ref[...] = jax.nn.softplus(jnp.dot(xn, wdt_ref[...], preferred_element_type=F32) + dtb_ref[...])
    ext[SSM_HALO:SSM_HALO + ts, :] = jnp.dot(xn, wxbc_ref[...], preferred_element_type=F32)
    base = SSM_HALO - (SSM_CONV - 1)
    y = jnp.zeros((ts, SSM_CONV_DIM), F32)
    for k in range(SSM_CONV):
        y = y + cw_ref[k:k + 1, :] * ext[base + k:base + k + ts, :]
    act_ref[...] = _silu(y + cb_ref[...])

    @pl.when(j == pl.num_programs(1) - 1)
    def _():
        cst_ref[0] = ext[SSM_HALO + ts - (SSM_CONV - 1):SSM_HALO + ts, :]

    ext[0:SSM_HALO, :] = ext[ts:ts + SSM_HALO, :]


def _ssm_in_prompt(x, g, w, bsz, seq, ts=256):
    m, d = x.shape
    ns = seq // ts
    consts = [g.reshape(1, d), w["wz"], w["wxbc"], w["wdt"], w["cw"], w["cb"], w["dtb"]]
    row = lambda b, j: (b * ns + j, 0)
    return pl.pallas_call(
        functools.partial(_ssm_in_prompt_kernel, ts), grid=(bsz, ns),
        in_specs=[pl.BlockSpec((ts, d), row)] + [_const_spec(c.shape) for c in consts],
        out_specs=[pl.BlockSpec((ts, D_INNER), row), pl.BlockSpec((ts, SSM_CONV_DIM), row),
                   pl.BlockSpec((ts, SSM_HEADS), row),
                   pl.BlockSpec((1, SSM_CONV - 1, SSM_CONV_DIM), lambda b, j: (b, 0, 0))],
        out_shape=[jax.ShapeDtypeStruct((m, D_INNER), F32), jax.ShapeDtypeStruct((m, SSM_CONV_DIM), F32),
                   jax.ShapeDtypeStruct((m, SSM_HEADS), F32),
                   jax.ShapeDtypeStruct((bsz, SSM_CONV - 1, SSM_CONV_DIM), F32)],
        scratch_shapes=[pltpu.VMEM((SSM_HALO + ts, SSM_CONV_DIM), F32)],
        compiler_params=_cparams(("parallel", "arbitrary")), name="ssm_in_prompt")(x, *consts)


def _ssd_scan_kernel(act_ref, dt_ref, alog_ref, d_ref, y_ref, h_ref):
    c = pl.program_id(1)
    L = SSM_CHUNK

    @pl.when(c == 0)
    def _():
        h_ref[...] = jnp.zeros(h_ref.shape, F32)

    dt = dt_ref[...]
    da = dt * (-jnp.exp(alog_ref[...]))
    ri = lax.broadcasted_iota(jnp.int32, (L, L), 0)
    ci = lax.broadcasted_iota(jnp.int32, (L, L), 1)
    tril = ri >= ci
    cs = _dot_exact(tril.astype(F32), da)
    eye = (lax.broadcasted_iota(jnp.int32, (SSM_HEADS, SSM_HEADS), 0)
           == lax.broadcasted_iota(jnp.int32, (SSM_HEADS, SSM_HEADS), 1)).astype(F32)
    cs_t = _dot_nt_exact(eye, cs)
    dt_t = _dot_nt_exact(eye, dt)
    w_t = dt_t * jnp.exp(cs_t[:, L - 1:L] - cs_t)
    cdec = jnp.exp(cs[L - 1:L, :])
    dvec = d_ref[...]
    for g in range(SSM_GROUPS):
        bg = act_ref[:, D_INNER + g * SSM_STATE:D_INNER + (g + 1) * SSM_STATE]
        cg = act_ref[:, D_INNER + SSM_GN + g * SSM_STATE:D_INNER + SSM_GN + (g + 1) * SSM_STATE]
        cb = _dot_nt(cg, bg)
        bg_t = bg.T
        for r8 in range(SSM_REP):
            r = g * SSM_REP + r8
            xr = act_ref[:, r * SSM_HD:(r + 1) * SSM_HD]
            h_t = h_ref[0, r]
            csb = jnp.broadcast_to(cs[:, r:r + 1], (L, L))
            lmat = jnp.exp(jnp.where(tril, csb - cs_t[r:r + 1, :], NEG_INF)) * (cb * dt_t[r:r + 1, :])
            lhs = jnp.concatenate([lmat, cg * jnp.exp(csb)], axis=1)
            rhs = jnp.concatenate([xr, h_t], axis=0)
            y_ref[:, r * SSM_HD:(r + 1) * SSM_HD] = _dot(lhs, rhs) + xr * dvec[:, r:r + 1]
            h_ref[0, r] = cdec[:, r:r + 1] * h_t + _dot(bg_t * w_t[r:r + 1, :], xr)


def _ssd_scan(act, dt, alog, dvec, bsz, seq):
    m = act.shape[0]
    nc = seq // SSM_CHUNK
    row = lambda b, c: (b * nc + c, 0)
    return pl.pallas_call(
        _ssd_scan_kernel, grid=(bsz, nc),
        in_specs=[pl.BlockSpec((SSM_CHUNK, SSM_CONV_DIM), row), pl.BlockSpec((SSM_CHUNK, SSM_HEADS), row),
                  _const_spec((1, SSM_HEADS)), _const_spec((1, SSM_HEADS))],
        out_specs=[pl.BlockSpec((SSM_CHUNK, D_INNER), row),
                   pl.BlockSpec((1, SSM_HEADS, SSM_STATE, SSM_HD), lambda b, c: (b, 0, 0, 0))],
        out_shape=[jax.ShapeDtypeStruct((m, D_INNER), F32),
                   jax.ShapeDtypeStruct((bsz, SSM_HEADS, SSM_STATE, SSM_HD), F32)],
        compiler_params=_cparams(("parallel", "arbitrary")), name="ssd_scan")(
            act, dt, alog.reshape(1, SSM_HEADS), dvec.reshape(1, SSM_HEADS))


def _ssm_out_kernel(y_ref, z_ref, n_ref, w_ref, res_ref, o_ref):
    yz = y_ref[...] * _silu(z_ref[...])
    gw = D_INNER // SSM_GROUPS
    parts = []
    for g in range(SSM_GROUPS):
        seg = yz[:, g * gw:(g + 1) * gw]
        parts.append(seg * lax.rsqrt(jnp.mean(seg * seg, axis=-1, keepdims=True) + RMS_EPS))
    o_ref[...] = res_ref[...] + _dot(jnp.concatenate(parts, axis=1) * n_ref[...], w_ref[...])


def _ssm_out(y, z, norm, wout, res, tm=256):
    m, d = res.shape
    tm = min(tm, m)
    row = lambda i: (i, 0)
    return pl.pallas_call(
        _ssm_out_kernel, grid=(m // tm,),
        in_specs=[pl.BlockSpec((tm, D_INNER), row), pl.BlockSpec((tm, D_INNER), row), _const_spec((1, D_INNER)),
                  _const_spec(wout.shape), pl.BlockSpec((tm, d), row)],
        out_specs=pl.BlockSpec((tm, d), row),
        out_shape=jax.ShapeDtypeStruct((m, d), F32),
        compiler_params=_cparams(("parallel",)), name="ssm_out")(y, z, norm.reshape(1, D_INNER), wout, res)


def _ssm_in_sample_kernel(x_ref, g_ref, wz_ref, wxbc_ref, wdt_ref, cw_ref, cb_ref, dtb_ref, st_ref,
                          z_ref, xbc_ref, act_ref, dt_ref):
    xn = _rms(x_ref[...], g_ref[...]).astype(BF16)
    z_ref[...] = jnp.dot(xn, wz_ref[...], preferred_element_type=F32)
    dt_ref[...] = jax.nn.softplus(jnp.dot(xn, wdt_ref[...], preferred_element_type=F32) + dtb_ref[...])
    xbc = jnp.dot(xn, wxbc_ref[...], preferred_element_type=F32)
    xbc_ref[...] = xbc
    y = jnp.zeros_like(xbc)
    for k in range(SSM_CONV - 1):
        y = y + cw_ref[k:k + 1, :] * st_ref[k]
    y = y + cw_ref[SSM_CONV - 1:SSM_CONV, :] * xbc
    act_ref[...] = _silu(y + cb_ref[...])


def _ssm_in_sample(x, g, w, state_t):
    m, d = x.shape
    args = [x, g.reshape(1, d), w["wz"], w["wxbc"], w["wdt"], w["cw"], w["cb"], w["dtb"], state_t]
    shapes = [(m, D_INNER), (m, SSM_CONV_DIM), (m, SSM_CONV_DIM), (m, SSM_HEADS)]
    return pl.pallas_call(
        _ssm_in_sample_kernel, grid=(1,),
        in_specs=[_const_spec(a.shape) for a in args],
        out_specs=[_const_spec(s) for s in shapes],
        out_shape=[jax.ShapeDtypeStruct(s, F32) for s in shapes],
        compiler_params=_cparams(("arbitrary",)), name="ssm_in_sample")(*args)


def _ssm_step_kernel(act_ref, dt_ref, alog_ref, d_ref, h_ref, y_ref, hn_ref):
    act = act_ref[0]
    dt = dt_ref[0]
    decay = jnp.exp(dt * (-jnp.exp(alog_ref[...])))
    dvec = d_ref[...]
    eye = (lax.broadcasted_iota(jnp.int32, (SSM_HD, SSM_HD), 0)
           == lax.broadcasted_iota(jnp.int32, (SSM_HD, SSM_HD), 1))
    for g in range(SSM_GROUPS):
        bg = act[:, D_INNER + g * SSM_STATE:D_INNER + (g + 1) * SSM_STATE]
        cg = jnp.broadcast_to(act[:, D_INNER + SSM_GN + g * SSM_STATE:D_INNER + SSM_GN + (g + 1) * SSM_STATE],
                              (8, SSM_STATE))
        for r8 in range(SSM_REP):
            r = g * SSM_REP + r8
            xr = act[:, r * SSM_HD:(r + 1) * SSM_HD]
            xdt = xr * dt[:, r:r + 1]
            xcol = jnp.sum(jnp.where(eye, jnp.broadcast_to(xdt, (SSM_HD, SSM_HD)), 0.0), axis=1, keepdims=True)
            h_new = decay[:, r:r + 1] * h_ref[0, r] + xcol * bg
            hn_ref[0, r] = h_new
            y = _dot_nt(cg, h_new)[0:1] + xr * dvec[:, r:r + 1]
            y_ref[0, :, r * SSM_HD:(r + 1) * SSM_HD] = y


def _ssm_step(act, dt, alog, dvec, state):
    bsz = act.shape[0]
    blk3 = lambda b: (b, 0, 0)
    y, hn = pl.pallas_call(
        _ssm_step_kernel, grid=(bsz,),
        in_specs=[pl.BlockSpec((1, 1, SSM_CONV_DIM), blk3), pl.BlockSpec((1, 1, SSM_HEADS), blk3),
                  _const_spec((1, SSM_HEADS)), _const_spec((1, SSM_HEADS)),
                  pl.BlockSpec((1, SSM_HEADS, SSM_HD, SSM_STATE), lambda b: (b, 0, 0, 0))],
        out_specs=[pl.BlockSpec((1, 1, D_INNER), blk3),
                   pl.BlockSpec((1, SSM_HEADS, SSM_HD, SSM_STATE), lambda b: (b, 0, 0, 0))],
        out_shape=[jax.ShapeDtypeStruct((bsz, 1, D_INNER), F32),
                   jax.ShapeDtypeStruct((bsz, SSM_HEADS, SSM_HD, SSM_STATE), F32)],
        compiler_params=_cparams(("parallel",)), name="ssm_step")(
            act.reshape(bsz, 1, SSM_CONV_DIM), dt.reshape(bsz, 1, SSM_HEADS), alog.reshape(1, SSM_HEADS),
            dvec.reshape(1, SSM_HEADS), state)
    return y.reshape(bsz, D_INNER), hn


def _dil_qkv_kernel(x_ref, g_ref, w_ref, cos_ref, sin_ref, q0_ref, q1_ref, q2_ref, kv0_ref, kv1_ref, kv2_ref):
    cos, sin = cos_ref[...], sin_ref[...]
    h = _dot(_rms(x_ref[...], g_ref[...]), w_ref[...])
    tiles = []
    for t in range(2 * DIL_WIDTH // LANE):
        tile = h[:, t * LANE:(t + 1) * LANE]
        tiles.append(tile * cos + _rot_half(tile, DIL_HD) * sin)
    qk = jnp.concatenate(tiles, axis=1)
    for g, (q_ref, kv_ref) in enumerate(((q0_ref, kv0_ref), (q1_ref, kv1_ref), (q2_ref, kv2_ref))):
        q_ref[...] = qk[:, g * DIL_GW:(g + 1) * DIL_GW]
        kv_ref[:, :DIL_GW] = qk[:, DIL_WIDTH + g * DIL_GW:DIL_WIDTH + (g + 1) * DIL_GW]
        kv_ref[:, DIL_GW:] = h[:, 2 * DIL_WIDTH + g * DIL_GW:2 * DIL_WIDTH + (g + 1) * DIL_GW]


def _dil_qkv(x, g, w, cos, sin, tm=256):
    m, d = x.shape
    tm = min(tm, m)
    nt = cos.shape[0] // tm
    row = lambda i: (i, 0)
    widths = [DIL_GW] * N_DIL + [2 * DIL_GW] * N_DIL
    return pl.pallas_call(
        _dil_qkv_kernel, grid=(m // tm,),
        in_specs=[pl.BlockSpec((tm, d), row), _const_spec((1, d)), _const_spec(w.shape)]
        + [pl.BlockSpec((tm, LANE), lambda i: (i % nt, 0))] * 2,
        out_specs=[pl.BlockSpec((tm, wd), row) for wd in widths],
        out_shape=[jax.ShapeDtypeStruct((m, wd), F32) for wd in widths],
        compiler_params=_cparams(("parallel",)), name="dil_qkv")(x, g.reshape(1, d), w, cos, sin)


def _dil_band_kernel(n_sub, sub_len, q_ref, kv_ref, o_ref, lse_ref):
    nk = DIL_NK
    qi = lax.broadcasted_iota(jnp.int32, (nk, 2 * nk), 0)
    ki = lax.broadcasted_iota(jnp.int32, (nk, 2 * nk), 1)
    band = (ki <= qi + nk) & (ki >= qi)
    causal = lax.broadcasted_iota(jnp.int32, (nk, nk), 1) <= lax.broadcasted_iota(jnp.int32, (nk, nk), 0)
    for u in range(n_sub):
        for blk in range(sub_len // nk):
            rows = pl.ds(blk * nk, nk)
            q = q_ref[u, rows, :]
            if blk == 0:
                kvb, mask = kv_ref[u, rows, :].astype(BF16), causal
            else:
                kvb, mask = kv_ref[u, pl.ds((blk - 1) * nk, 2 * nk), :].astype(BF16), band
            heads = range(DIL_HEADS)
            s = jnp.concatenate([_dot_nt(q[:, h * DIL_HD:(h + 1) * DIL_HD], kvb[:, h * DIL_HD:(h + 1) * DIL_HD])
                                 for h in heads], axis=0) * DIL_HD ** -0.5
            s = jnp.where(jnp.concatenate([mask] * DIL_HEADS, axis=0), s, NEG_INF)
            mx = jnp.max(s, axis=-1, keepdims=True)
            e = jnp.exp(s - mx)
            l = jnp.sum(e, axis=-1, keepdims=True)
            p = e * (1.0 / l)
            lse = mx + jnp.log(l)
            for h in heads:
                cols = slice(h * DIL_HD, (h + 1) * DIL_HD)
                o_ref[u, rows, cols] = _dot(p[h * nk:(h + 1) * nk], kvb[:, DIL_GW + h * DIL_HD:DIL_GW + (h + 1) * DIL_HD])
                lse_ref[u, rows, cols] = jnp.broadcast_to(lse[h * nk:(h + 1) * nk], (nk, DIL_HD))


def _dil_prompt(q, kv, dil, bsz, seq):
    sub_len = seq // dil

    def split(a):
        w = a.shape[-1]
        return a.reshape(bsz, sub_len, dil, w).transpose(0, 2, 1, 3).reshape(bsz * dil, sub_len, w)

    def merge(a):
        return a.reshape(bsz, dil, sub_len, DIL_GW).transpose(0, 2, 1, 3).reshape(bsz * seq, DIL_GW)

    blk = lambda b: (b, 0, 0)
    n_sub = max(1, min(dil, 16 * DIL_NK // sub_len))
    o, lse = pl.pallas_call(
        functools.partial(_dil_band_kernel, n_sub, sub_len), grid=(bsz * dil // n_sub,),
        in_specs=[pl.BlockSpec((n_sub, sub_len, DIL_GW), blk), pl.BlockSpec((n_sub, sub_len, 2 * DIL_GW), blk)],
        out_specs=[pl.BlockSpec((n_sub, sub_len, DIL_GW), blk)] * 2,
        out_shape=[jax.ShapeDtypeStruct((bsz * dil, sub_len, DIL_GW), F32)] * 2,
        compiler_params=_cparams(("parallel",)), name="dil_band_d%d" % dil)(split(q), split(kv))
    return merge(o), merge(lse)


def _dil_combine(outs, lses):
    mx = jnp.maximum(jnp.maximum(lses[0], lses[1]), lses[2])
    es = [jnp.exp(l - mx) for l in lses]
    den = es[0] + es[1] + es[2]
    return [o * (e / den) for o, e in zip(outs, es)]


def _dil_out_kernel(o0_ref, o1_ref, o2_ref, l0_ref, l1_ref, l2_ref, w_ref, res_ref, out_ref):
    parts = _dil_combine([o0_ref[...], o1_ref[...], o2_ref[...]], [l0_ref[...], l1_ref[...], l2_ref[...]])
    y = res_ref[...]
    for g in range(N_DIL):
        y = y + _dot(parts[g], w_ref[g])
    out_ref[...] = y


def _dil_out(outs, lses, w, res, tm=256):
    m, d = res.shape
    tm = min(tm, m)
    row = lambda i: (i, 0)
    return pl.pallas_call(
        _dil_out_kernel, grid=(m // tm,),
        in_specs=[pl.BlockSpec((tm, DIL_GW), row)] * 6 + [_const_spec(w.shape), pl.BlockSpec((tm, d), row)],
        out_specs=pl.BlockSpec((tm, d), row),
        out_shape=jax.ShapeDtypeStruct((m, d), F32),
        compiler_params=_cparams(("parallel",)), name="dil_out")(*outs, *lses, w, res)


def _dil_sample_kernel(q0_ref, q1_ref, q2_ref, n0_ref, n1_ref, n2_ref, b0_ref, b1_ref, b2_ref,
                       o_ref, nb0_ref, nb1_ref, nb2_ref):
    eye = (lax.broadcasted_iota(jnp.int32, (DIL_HD, DIL_HD), 0) == lax.broadcasted_iota(jnp.int32, (DIL_HD, DIL_HD), 1))

    def to_col(row):
        return jnp.sum(jnp.where(eye, jnp.broadcast_to(row, (DIL_HD, DIL_HD)), 0.0), axis=1, keepdims=True)

    def to_row(col):
        return jnp.sum(jnp.where(eye, jnp.broadcast_to(col, (DIL_HD, DIL_HD)), 0.0), axis=0, keepdims=True)

    scale = DIL_HD ** -0.5
    heads = range(DIL_HEADS)
    outs, lses = [], []
    for (win, dil), q_ref, n_ref, b_ref, nb_ref in zip(DIL_GROUPS, (q0_ref, q1_ref, q2_ref), (n0_ref, n1_ref, n2_ref),
                                                       (b0_ref, b1_ref, b2_ref), (nb0_ref, nb1_ref, nb2_ref)):
        rows = b_ref.shape[-1]
        reads = (lax.broadcasted_iota(jnp.int32, (1, rows), 1) & (dil - 1)) == 0
        last = lax.broadcasted_iota(jnp.int32, (DIL_HD, rows), 1) == rows - 1
        q = q_ref[0]
        new = n_ref[0]
        qh = [q[:, h * DIL_HD:(h + 1) * DIL_HD] for h in heads]
        k_new = [new[:, h * DIL_HD:(h + 1) * DIL_HD] for h in heads]
        v_new = [new[:, DIL_GW + h * DIL_HD:DIL_GW + (h + 1) * DIL_HD] for h in heads]
        q_col = [to_col(x) for x in qh]
        s = jnp.concatenate([jnp.sum(b_ref[0, 0, h] * q_col[h], axis=0, keepdims=True) for h in heads], axis=0)
        s = jnp.where(reads, s * scale, NEG_INF)
        s_new = jnp.concatenate([jnp.sum(qh[h] * k_new[h], axis=-1, keepdims=True) for h in heads], axis=0) * scale
        mx = jnp.maximum(jnp.max(s, axis=-1, keepdims=True), s_new)
        lse = mx + jnp.log(jnp.sum(jnp.exp(s - mx), axis=-1, keepdims=True) + jnp.exp(s_new - mx))
        p = jnp.exp(s - lse)
        p_new = jnp.exp(s_new - lse)
        outs.append([to_row(jnp.sum(b_ref[0, 1, h] * p[h:h + 1], axis=1, keepdims=True)) + p_new[h:h + 1] * v_new[h]
                     for h in heads])
        lses.append(lse)
        for h in heads:
            nb_ref[0, 0, h] = jnp.where(last, to_col(k_new[h]), pltpu.roll(b_ref[0, 0, h], rows - 1, 1))
            nb_ref[0, 1, h] = jnp.where(last, to_col(v_new[h]), pltpu.roll(b_ref[0, 1, h], rows - 1, 1))
    top = jnp.maximum(jnp.maximum(lses[0], lses[1]), lses[2])
    es = [jnp.exp(x - top) for x in lses]
    den = es[0] + es[1] + es[2]
    o_ref[0] = jnp.concatenate([outs[g][h] * (es[g][h:h + 1] / den[h:h + 1]) for g in range(N_DIL) for h in heads],
                               axis=1)


def _dil_sample(qs, news, bufs_t):
    bsz = qs[0].shape[0]
    blk = lambda b: (b, 0, 0)
    blk5 = lambda b: (b, 0, 0, 0, 0)
    small = [q.reshape(bsz, 1, DIL_GW) for q in qs] + [n.reshape(bsz, 1, 2 * DIL_GW) for n in news]
    buf_specs = [pl.BlockSpec((1,) + b.shape[1:], blk5) for b in bufs_t]
    res = pl.pallas_call(
        _dil_sample_kernel, grid=(bsz,),
        in_specs=[pl.BlockSpec((1,) + a.shape[1:], blk) for a in small] + buf_specs,
        out_specs=[pl.BlockSpec((1, 1, DIL_WIDTH), blk)] + buf_specs,
        out_shape=[jax.ShapeDtypeStruct((bsz, 1, DIL_WIDTH), F32)]
        + [jax.ShapeDtypeStruct(b.shape, F32) for b in bufs_t],
        compiler_params=_cparams(("parallel",)), name="dil_sample")(*small, *bufs_t)
    return res[0].reshape(bsz, DIL_WIDTH), res[1:]


def _block_diag_pairs(w):
    h, k, n = w.shape
    z = jnp.zeros((h // 2, k, n), w.dtype)
    top = jnp.concatenate([w[0::2], z], axis=2)
    bot = jnp.concatenate([z, w[1::2]], axis=2)
    return jnp.concatenate([top, bot], axis=1)


def _prep_mla(mla_w_in, mla_norm_q, mla_w_uq, mla_norm_kv, mla_w_uk, mla_w_uv, mla_w_o):
    pad = jnp.zeros((D_MODEL, LANE - MLA_ROPE), F32)
    wuq = mla_w_uq.reshape(MLA_Q_LORA, MLA_HEADS, MLA_NOPE + MLA_ROPE)
    wuq = jnp.concatenate([wuq[:, :, :MLA_NOPE].reshape(MLA_Q_LORA, -1), wuq[:, :, MLA_NOPE:].reshape(MLA_Q_LORA, -1)],
                          axis=1)
    return {
        "win": jnp.concatenate([mla_w_in, pad], axis=1).astype(BF16),
        "nq": mla_norm_q.reshape(1, -1), "nkv": mla_norm_kv.reshape(1, -1),
        "wuq": wuq.astype(BF16),
        "wuk": _block_diag_pairs(jnp.transpose(mla_w_uk, (1, 2, 0))).astype(BF16),
        "wuv": _block_diag_pairs(jnp.transpose(mla_w_uv, (1, 0, 2))).astype(BF16),
        "wo": _cast_bf16(mla_w_o),
    }


def _prep_conv(conv_w_in, conv_b_in, conv_w_dw, conv_b_dw, conv_ln_g, conv_ln_b, conv_w_out, conv_b_out):
    return {"win": _cast_bf16(conv_w_in), "bin": conv_b_in.reshape(1, -1), "wdw": conv_w_dw,
            "bdw": conv_b_dw.reshape(1, -1), "lng": conv_ln_g.reshape(1, -1), "lnb": conv_ln_b.reshape(1, -1),
            "wout": _cast_bf16(conv_w_out), "bout": conv_b_out.reshape(1, -1)}


def _prep_ssm(ssm_w_in, ssm_conv_w, ssm_conv_b, ssm_dt_bias):
    w = ssm_w_in.astype(BF16)
    return {"wz": w[:, :D_INNER], "wxbc": w[:, D_INNER:D_INNER + SSM_CONV_DIM], "wdt": w[:, D_INNER + SSM_CONV_DIM:],
            "cw": ssm_conv_w, "cb": ssm_conv_b.reshape(1, -1), "dtb": ssm_dt_bias.reshape(1, -1)}


def kernel(x_prompt, x_sample, cache_mla_latent, cache_mem_kv, state_conv, state_ssm_conv, state_ssm, state_dil_kv0, state_dil_kv1, state_dil_kv2, page_table, mem_prompt, norm_ffn1, ffn1_w_gu, ffn1_w_down, norm_mix, norm_cross, norm_mem, cross_w_q, cross_w_kv, cross_w_o, norm_ffn2, ffn2_w_gu, ffn2_w_down, norm_final, mla_w_in, mla_norm_q, mla_w_uq, mla_norm_kv, mla_w_uk, mla_w_uv, mla_w_o, conv_w_in, conv_b_in, conv_w_dw, conv_b_dw, conv_ln_g, conv_ln_b, conv_w_out, conv_b_out, ssm_w_in, ssm_conv_w, ssm_conv_b, ssm_dt_bias, ssm_a_log, ssm_d, ssm_norm, ssm_w_out, dil_w_qkv, dil_w_o):
    bsz, seq, d = x_prompt.shape
    dec = x_sample.shape[0]
    assert x_sample.shape[1] == 1 and d == D_MODEL
    depth = norm_ffn1.shape[0]
    past_len = page_table.shape[1] * cache_mla_latent.shape[1]
    pos_p = jnp.arange(seq, dtype=jnp.int32)
    pos_s = jnp.full((dec,), past_len, dtype=jnp.int32)

    res_p = x_prompt.reshape(bsz * seq, d)
    res_s = x_sample.reshape(dec, d)
    mem_len = mem_prompt.shape[1]
    w1gu, w1d = _cast_bf16(ffn1_w_gu), _cast_bf16(ffn1_w_down)
    w2gu, w2d = _cast_bf16(ffn2_w_gu), _cast_bf16(ffn2_w_down)
    xwq, xwo = _cast_bf16(cross_w_q), _cast_bf16(cross_w_o)
    mem_kv = _memkv(mem_prompt.reshape(bsz * mem_len, d), norm_mem, _cast_bf16(cross_w_kv))
    mem_kv_b = mem_kv.reshape(depth * bsz, mem_len, 2 * d)
    outs = {}

    for i in range(depth):
        kind = i % 4
        res_p = _ffn(res_p, norm_ffn1, w1gu, w1d, i)
        res_s = _ffn(res_s, norm_ffn1, w1gu, w1d, i)
        g_mix = norm_mix[i]
        if kind == 0:
            w = _prep_mla(mla_w_in, mla_norm_q, mla_w_uq, mla_norm_kv, mla_w_uk, mla_w_uv, mla_w_o)
            cos_p, sin_p = _rope_tables(pos_p, MLA_ROPE)
            cos_s, sin_s = _rope_tables(pos_s, MLA_ROPE)
            q_p, lat_p = _mla_proj(res_p, g_mix, w, cos_p, sin_p)
            res_p = _mla_attn_prompt(q_p, lat_p, w["wuv"], w["wo"], res_p, bsz, seq)
            q_s, lat_s = _mla_proj(res_s, g_mix, w, cos_s, sin_s)
            o_s = _mla_attn_sample(jnp.transpose(q_s, (1, 0, 2)), lat_s.reshape(dec, 1, MLA_LATENT),
                                   jnp.transpose(cache_mla_latent, (0, 2, 1)), page_table)
            res_s = _mla_out(jnp.transpose(o_s, (1, 0, 2)), w["wuv"], w["wo"], res_s)
            outs["mla_p"] = lat_p.reshape(bsz, seq, MLA_LATENT)
            outs["mla_s"] = lat_s.reshape(dec, 1, MLA_LATENT)
        elif kind == 1:
            w = _prep_conv(conv_w_in, conv_b_in, conv_w_dw, conv_b_dw, conv_ln_g, conv_ln_b, conv_w_out, conv_b_out)
            res_p, outs["conv_p"] = _conv_prompt(res_p, g_mix, w, bsz, seq)
            res_s, u_s = _conv_sample(res_s, g_mix, w, jnp.transpose(state_conv, (1, 0, 2)))
            outs["conv_s"] = jnp.concatenate([state_conv[:, 1:], u_s[:, None, :]], axis=1)
        elif kind == 2:
            w = _prep_ssm(ssm_w_in, ssm_conv_w, ssm_conv_b, ssm_dt_bias)
            wout = _cast_bf16(ssm_w_out)
            z_p, act_p, dt_p, outs["ssmc_p"] = _ssm_in_prompt(res_p, g_mix, w, bsz, seq)
            y_p, h_t = _ssd_scan(act_p, dt_p, ssm_a_log, ssm_d, bsz, seq)
            outs["ssm_p"] = jnp.swapaxes(h_t, 2, 3)
            res_p = _ssm_out(y_p, z_p, ssm_norm, wout, res_p)
            z_s, xbc_s, act_s, dt_s = _ssm_in_sample(res_s, g_mix, w, jnp.transpose(state_ssm_conv, (1, 0, 2)))
            outs["ssmc_s"] = jnp.concatenate([state_ssm_conv[:, 1:], xbc_s[:, None, :]], axis=1)
            y_s, outs["ssm_s"] = _ssm_step(act_s, dt_s, ssm_a_log, ssm_d, state_ssm)
            res_s = _ssm_out(y_s, z_s, ssm_norm, wout, res_s)
        else:
            wqkv = _cast_bf16(dil_w_qkv)
            wo2 = _cast_bf16(dil_w_o)
            wo = wo2.reshape(N_DIL, DIL_GW, d)
            cos_p, sin_p = _rope_tables(pos_p, DIL_HD)
            cos_s, sin_s = _rope_tables(pos_s, DIL_HD)
            qkv_p = _dil_qkv(res_p, g_mix, wqkv, cos_p, sin_p)
            qkv_s = _dil_qkv(res_s, g_mix, wqkv, cos_s, sin_s)
            bufs = (state_dil_kv0, state_dil_kv1, state_dil_kv2)
            o_list, l_list = [], []
            for g, (win, dil) in enumerate(DIL_GROUPS):
                kv3 = qkv_p[N_DIL + g].reshape(bsz, seq, 2 * DIL_GW)
                o, l = _dil_prompt(qkv_p[g], qkv_p[N_DIL + g], dil, bsz, seq)
                o_list.append(o)
                l_list.append(l)
                keep = min(win, seq)
                outs["dil%d_p" % g] = kv3[:, seq - keep:].reshape(bsz, keep, 2, DIL_HEADS, DIL_HD)
                assert bufs[g].shape[1] == win
            res_p = _dil_out(o_list, l_list, wo, res_p)
            o_s, new_bufs = _dil_sample(qkv_s[:N_DIL], qkv_s[N_DIL:], [jnp.transpose(b, (0, 2, 3, 4, 1)) for b in bufs])
            for g in range(N_DIL):
                outs["dil%d_s" % g] = jnp.transpose(new_bufs[g], (0, 4, 1, 2, 3))
            res_s = _proj_res([o_s], [wo2], res_s, name="dil_out_sample")

        res_p = _xattn_prompt(res_p, norm_cross, xwq, mem_kv_b, xwo, i, bsz, seq)
        (q_s,) = _norm_proj(res_s, norm_cross[i], [xwq[i]], name="xattn_q_sample")
        a_s = _xattn_sample(q_s, cache_mem_kv, i)
        res_s = _proj_res([a_s], [xwo[i]], res_s, name="xattn_o_sample")

        final_g = norm_final if i == depth - 1 else None
        res_p = _ffn(res_p, norm_ffn2, w2gu, w2d, i, final_g)
        res_s = _ffn(res_s, norm_ffn2, w2gu, w2d, i, final_g)

    return (res_p.reshape(bsz, seq, d), res_s.reshape(dec, 1, d),
            outs["mla_p"], outs["mla_s"], outs["conv_p"], outs["conv_s"], outs["ssmc_p"], outs["ssmc_s"],
            outs["ssm_p"], outs["ssm_s"],
            outs["dil0_p"], outs["dil0_s"], outs["dil1_p"], outs["dil1_s"], outs["dil2_p"], outs["dil2_s"],
            mem_kv.reshape(depth, bsz, mem_len, 2, MEM_HEADS, MEM_HD))
```

```python
import functools
import math

import jax
import jax.numpy as jnp
from jax import lax
from jax.experimental import pallas as pl
from jax.experimental.pallas import tpu as pltpu

F32 = jnp.float32
BF16 = jnp.bfloat16

D_MODEL = 1024
D_FF = 2816
RMS_EPS = 1e-6
LN_EPS = 1e-5
ROPE_THETA = 10000.0

MLA_HEADS = 16
MLA_Q_LORA = 256
MLA_KV_LORA = 128
MLA_NOPE = 64
MLA_ROPE = 32
MLA_V_HD = 64
MLA_LATENT = MLA_KV_LORA + MLA_ROPE
MLA_SCALE = (MLA_NOPE + MLA_ROPE) ** -0.5

CONV_WIDTH = 31
CONV_HALO = 32

D_INNER = 2 * D_MODEL
SSM_HD = 64
SSM_HEADS = D_INNER // SSM_HD
SSM_GROUPS = 4
SSM_REP = SSM_HEADS // SSM_GROUPS
SSM_STATE = 128
SSM_CONV = 4
SSM_CHUNK = 128
SSM_GN = SSM_GROUPS * SSM_STATE
SSM_CONV_DIM = D_INNER + 2 * SSM_GN
SSM_HALO = 8

DIL_GROUPS = ((128, 1), (512, 4), (2048, 16))
N_DIL = 3
DIL_HEADS = 5
DIL_HD = 64
DIL_GW = DIL_HEADS * DIL_HD
DIL_WIDTH = N_DIL * DIL_GW
DIL_NK = 128
DIL_OW = 384

MEM_HEADS = 4
MEM_HD = D_MODEL // MEM_HEADS

LANE = 128
NEG_INF = float("-inf")
LOG2E = 1.4426950408889634


def _cparams(sem, vmem_mb=48):
    return pltpu.CompilerParams(dimension_semantics=sem, vmem_limit_bytes=vmem_mb * 1024 * 1024)


def _const_spec(shape):
    nd = len(shape)
    return pl.BlockSpec(shape, lambda *_: (0,) * nd, pipeline_mode=pl.Buffered(1))


def _dot(a, b):
    return jnp.dot(a.astype(BF16), b.astype(BF16), preferred_element_type=F32)


def _dot_nt(a, b):
    return lax.dot_general(a.astype(BF16), b.astype(BF16), (((1,), (1,)), ((), ())),
                           preferred_element_type=F32)


def _dot_exact(a, b):
    return jnp.dot(a, b, preferred_element_type=F32, precision=lax.Precision.HIGHEST)


def _dot_nt_exact(a, b):
    return lax.dot_general(a, b, (((1,), (1,)), ((), ())), preferred_element_type=F32,
                           precision=lax.Precision.HIGHEST)


def _rms(x, g):
    return x * lax.rsqrt(jnp.mean(x * x, axis=-1, keepdims=True) + RMS_EPS) * g


def _silu(x):
    return x * jax.nn.sigmoid(x)


def _rot_half(t, dim):
    half = dim // 2
    lane = lax.broadcasted_iota(jnp.int32, t.shape, 1)
    return jnp.where((lane & (dim - 1)) < half, pltpu.roll(t, LANE - half, 1), pltpu.roll(t, half, 1))


def _rope_tables(pos, dim):
    half = dim // 2
    inv_freq = jnp.exp(-math.log(ROPE_THETA) * (2.0 / dim) * jnp.arange(half, dtype=F32))
    ang = pos.astype(F32)[:, None] * inv_freq[None, :]
    cos, sin = jnp.cos(ang), jnp.sin(ang)
    reps = LANE // dim
    return (jnp.tile(jnp.concatenate([cos, cos], axis=1), (1, reps)),
            jnp.tile(jnp.concatenate([-sin, sin], axis=1), (1, reps)))


def _softmax_rows(s):
    m = jnp.max(s, axis=-1, keepdims=True)
    p = jnp.exp(s - m)
    return p / jnp.sum(p, axis=-1, keepdims=True)


def _cast_kernel(x_ref, o_ref):
    o_ref[...] = x_ref[...].astype(BF16)


def _cast_bf16(w, rows=256):
    shape = w.shape
    w2 = w.reshape(-1, shape[-1])
    m, n = w2.shape
    rows = min(rows, m)
    out = pl.pallas_call(
        _cast_kernel, grid=(pl.cdiv(m, rows),),
        in_specs=[pl.BlockSpec((rows, n), lambda i: (i, 0))],
        out_specs=pl.BlockSpec((rows, n), lambda i: (i, 0)),
        out_shape=jax.ShapeDtypeStruct((m, n), BF16),
        compiler_params=_cparams(("parallel",)), name="cast_bf16")(w2)
    return out.reshape(shape)


def _layer_spec(shape, layer):
    nd = len(shape)
    return pl.BlockSpec((1,) + tuple(shape[1:]), lambda *_: (layer,) + (0,) * (nd - 1), pipeline_mode=pl.Buffered(1))


def _ffn_kernel(has_final, x_ref, g_ref, wgu_ref, wd_ref, *rest):
    o_ref = rest[-1]
    x = x_ref[...]
    h = _dot(_rms(x, g_ref[0]), wgu_ref[0])
    act = _silu(h[:, :D_FF]) * h[:, D_FF:]
    y = x + 0.5 * _dot(act, wd_ref[0])
    if has_final:
        y = _rms(y, rest[0][...])
    o_ref[...] = y


def _ffn(x, g, wgu, wd, layer, final_g=None, tm=512):
    m, d = x.shape
    tm = min(tm, m)
    g = g.reshape(g.shape[0], 1, d)
    args = [x, g, wgu, wd]
    in_specs = [pl.BlockSpec((tm, d), lambda i: (i, 0)), _layer_spec(g.shape, layer),
                _layer_spec(wgu.shape, layer), _layer_spec(wd.shape, layer)]
    if final_g is not None:
        args.append(final_g.reshape(1, d))
        in_specs.append(_const_spec((1, d)))
    return pl.pallas_call(
        functools.partial(_ffn_kernel, final_g is not None),
        grid=(m // tm,), in_specs=in_specs,
        out_specs=pl.BlockSpec((tm, d), lambda i: (i, 0)),
        out_shape=jax.ShapeDtypeStruct((m, d), F32),
        compiler_params=_cparams(("parallel",), 56), name="ffn")(*args)


def _norm_proj_kernel(n, has_bias, x_ref, g_ref, *refs):
    xn = _rms(x_ref[...], g_ref[...]).astype(BF16)
    for i in range(n):
        y = jnp.dot(xn, refs[i][...], preferred_element_type=F32)
        if has_bias:
            y = y + refs[n + i][...]
        refs[-n + i][...] = y


def _norm_proj(x, g, ws, bs=None, tm=256, name="norm_proj"):
    m, d = x.shape
    tm = min(tm, m)
    n = len(ws)
    args = [x, g.reshape(1, d)] + list(ws)
    in_specs = [pl.BlockSpec((tm, d), lambda i: (i, 0)), _const_spec((1, d))] + [_const_spec(w.shape) for w in ws]
    if bs is not None:
        args += [b.reshape(1, -1) for b in bs]
        in_specs += [_const_spec((1, b.shape[-1])) for b in bs]
    outs = pl.pallas_call(
        functools.partial(_norm_proj_kernel, n, bs is not None),
        grid=(m // tm,), in_specs=in_specs,
        out_specs=[pl.BlockSpec((tm, w.shape[1]), lambda i: (i, 0)) for w in ws],
        out_shape=[jax.ShapeDtypeStruct((m, w.shape[1]), F32) for w in ws],
        compiler_params=_cparams(("parallel",)), name=name)(*args)
    return outs


def _proj_res_kernel(n, has_bias, *refs):
    a_refs, w_refs = refs[:n], refs[n:2 * n]
    res_ref, o_ref = refs[2 * n], refs[-1]
    y = res_ref[...]
    for a_ref, w_ref in zip(a_refs, w_refs):
        y = y + _dot(a_ref[...], w_ref[...])
    if has_bias:
        y = y + refs[2 * n + 1][...]
    o_ref[...] = y


def _proj_res(a_list, w_list, res, bias=None, tm=256, name="proj_res"):
    m, d = res.shape
    tm = min(tm, m)
    n = len(a_list)
    args = list(a_list) + list(w_list) + [res]
    in_specs = ([pl.BlockSpec((tm, a.shape[1]), lambda i: (i, 0)) for a in a_list]
                + [_const_spec(w.shape) for w in w_list] + [pl.BlockSpec((tm, d), lambda i: (i, 0))])
    if bias is not None:
        args.append(bias.reshape(1, d))
        in_specs.append(_const_spec((1, d)))
    return pl.pallas_call(
        functools.partial(_proj_res_kernel, n, bias is not None),
        grid=(m // tm,), in_specs=in_specs,
        out_specs=pl.BlockSpec((tm, d), lambda i: (i, 0)),
        out_shape=jax.ShapeDtypeStruct((m, d), F32),
        compiler_params=_cparams(("parallel",)), name=name)(*args)


def _memkv_kernel(x_ref, g_ref, w_ref, o_ref):
    o_ref[0] = _dot(_rms(x_ref[...], g_ref[0]), w_ref[0])


def _memkv(mem2d, norm_mem, w_kv, tm=256):
    m, d = mem2d.shape
    depth = w_kv.shape[0]
    n = w_kv.shape[2]
    return pl.pallas_call(
        _memkv_kernel, grid=(depth, m // tm),
        in_specs=[pl.BlockSpec((tm, d), lambda l, i: (i, 0)),
                  pl.BlockSpec((1, 1, d), lambda l, i: (l, 0, 0)),
                  pl.BlockSpec((1, d, n), lambda l, i: (l, 0, 0))],
        out_specs=pl.BlockSpec((1, tm, n), lambda l, i: (l, i, 0)),
        out_shape=jax.ShapeDtypeStruct((depth, m, n), F32),
        compiler_params=_cparams(("parallel", "parallel")), name="memkv")(mem2d, norm_mem.reshape(depth, 1, d), w_kv)


def _xattn_core(q, kv):
    kvb = kv.astype(BF16)
    outs = []
    for h in range(MEM_HEADS):
        kh = kvb[:, h * MEM_HD:(h + 1) * MEM_HD]
        vh = kvb[:, D_MODEL + h * MEM_HD:D_MODEL + (h + 1) * MEM_HD]
        s = _dot_nt(q[:, h * MEM_HD:(h + 1) * MEM_HD], kh) * MEM_HD ** -0.5
        outs.append(_dot(_softmax_rows(s), vh))
    return jnp.concatenate(outs, axis=1)


def _xattn_prompt_kernel(x_ref, g_ref, wq_ref, kv_ref, wo_ref, o_ref):
    x = x_ref[...]
    q = _dot(_rms(x, g_ref[0]), wq_ref[0])
    o_ref[...] = x + _dot(_xattn_core(q, kv_ref[0]), wo_ref[0])


def _xattn_prompt(x, g, wq, kv, wo, layer, bsz, seq, tq=512):
    m, d = x.shape
    nq = seq // tq
    _, ml, kvw = kv.shape
    g = g.reshape(g.shape[0], 1, d)
    return pl.pallas_call(
        _xattn_prompt_kernel, grid=(bsz, nq),
        in_specs=[pl.BlockSpec((tq, d), lambda b, i: (b * nq + i, 0)), _layer_spec(g.shape, layer),
                  _layer_spec(wq.shape, layer),
                  pl.BlockSpec((1, ml, kvw), lambda b, i: (layer * bsz + b, 0, 0)), _layer_spec(wo.shape, layer)],
        out_specs=pl.BlockSpec((tq, d), lambda b, i: (b * nq + i, 0)),
        out_shape=jax.ShapeDtypeStruct((m, d), F32),
        compiler_params=_cparams(("parallel", "parallel")), name="xattn_prompt")(x, g, wq, kv, wo)


def _xattn_sample_kernel(q_ref, kv_ref, o_ref):
    q = q_ref[0]
    k = kv_ref[0, 0, :, 0]
    v = kv_ref[0, 0, :, 1]
    s = jnp.sum(k * q[None], axis=-1, keepdims=True) * MEM_HD ** -0.5
    p = jnp.exp(s - jnp.max(s, axis=0, keepdims=True))
    p = p / jnp.sum(p, axis=0, keepdims=True)
    o_ref[0] = jnp.sum(p * v, axis=0)


def _xattn_sample(q, cache, layer):
    bsz, d = q.shape
    out = pl.pallas_call(
        _xattn_sample_kernel, grid=(bsz,),
        in_specs=[pl.BlockSpec((1, MEM_HEADS, MEM_HD), lambda b: (b, 0, 0)),
                  pl.BlockSpec((1, 1) + cache.shape[2:], lambda b: (layer, b, 0, 0, 0, 0))],
        out_specs=pl.BlockSpec((1, MEM_HEADS, MEM_HD), lambda b: (b, 0, 0)),
        out_shape=jax.ShapeDtypeStruct((bsz, MEM_HEADS, MEM_HD), F32),
        compiler_params=_cparams(("parallel",)), name="xattn_sample")(q.reshape(bsz, MEM_HEADS, MEM_HD), cache)
    return out.reshape(bsz, d)


def _mla_proj_kernel(x_ref, g_ref, win_ref, nq_ref, wuq_ref, nkv_ref, wuk_ref, cos_ref, sin_ref, q_ref, lat_ref):
    cos, sin = cos_ref[...], sin_ref[...]
    a = _dot(_rms(x_ref[...], g_ref[...]), win_ref[...])
    ckv = _rms(a[:, MLA_Q_LORA:MLA_Q_LORA + MLA_KV_LORA], nkv_ref[...])
    pe = a[:, MLA_Q_LORA + MLA_KV_LORA:]
    kpe = pe * cos + _rot_half(pe, MLA_ROPE) * sin
    lat_ref[:, :MLA_KV_LORA] = ckv
    lat_ref[:, MLA_KV_LORA:] = kpe[:, :MLA_ROPE]
    q = _dot(_rms(a[:, :MLA_Q_LORA], nq_ref[...]), wuq_ref[...])
    n_nope = MLA_HEADS * MLA_NOPE
    for p in range(MLA_HEADS // 2):
        ql = _dot(q[:, p * LANE:(p + 1) * LANE], wuk_ref[p])
        q_ref[2 * p, :, :MLA_KV_LORA] = ql[:, :MLA_KV_LORA]
        q_ref[2 * p + 1, :, :MLA_KV_LORA] = ql[:, MLA_KV_LORA:]
    per_tile = LANE // MLA_ROPE
    for t in range(MLA_HEADS // per_tile):
        tile = q[:, n_nope + t * LANE:n_nope + (t + 1) * LANE]
        qpe = tile * cos + _rot_half(tile, MLA_ROPE) * sin
        for j in range(per_tile):
            q_ref[t * per_tile + j, :, MLA_KV_LORA:] = qpe[:, j * MLA_ROPE:(j + 1) * MLA_ROPE]


def _mla_proj(x, g, w, cos, sin, tm=256):
    m, d = x.shape
    tm = min(tm, m)
    nt = cos.shape[0] // tm
    consts = [g.reshape(1, d), w["win"], w["nq"], w["wuq"], w["nkv"], w["wuk"]]
    return pl.pallas_call(
        _mla_proj_kernel, grid=(m // tm,),
        in_specs=([pl.BlockSpec((tm, d), lambda i: (i, 0))] + [_const_spec(c.shape) for c in consts]
                  + [pl.BlockSpec((tm, LANE), lambda i: (i % nt, 0))] * 2),
        out_specs=[pl.BlockSpec((MLA_HEADS, tm, MLA_LATENT), lambda i: (0, i, 0)),
                   pl.BlockSpec((tm, MLA_LATENT), lambda i: (i, 0))],
        out_shape=[jax.ShapeDtypeStruct((MLA_HEADS, m, MLA_LATENT), F32),
                   jax.ShapeDtypeStruct((m, MLA_LATENT), F32)],
        compiler_params=_cparams(("parallel",)), name="mla_proj")(x, *consts, cos, sin)


def _mla_attn_kernel(tq, tk, q_ref, lat_ref, wuv_ref, wo_ref, res_ref, o_ref):
    i = pl.program_id(1)
    rows = MLA_HEADS * tq
    q = (q_ref[...].reshape(rows, MLA_LATENT) * (MLA_SCALE * LOG2E)).astype(BF16)

    def chunk(c, carry, diagonal):
        m, l, acc = carry
        keys = lat_ref[pl.ds(pl.multiple_of(c * tk, tk), tk), :].astype(BF16)
        s = _dot_nt(q, keys)
        if diagonal:
            row = lax.broadcasted_iota(jnp.int32, (rows, tk), 0)
            col = lax.broadcasted_iota(jnp.int32, (rows, tk), 1)
            s = jnp.where(col - (row & (tq - 1)) <= i * tq - c * tk, s, NEG_INF)
        m_new = jnp.maximum(m, jnp.max(s, axis=-1, keepdims=True))
        alpha = jnp.exp2(m - m_new)
        p = jnp.exp2(s - m_new)
        l = alpha * l + jnp.sum(p, axis=-1, keepdims=True)
        acc = alpha * acc + _dot(p, keys[:, :MLA_KV_LORA])
        return m_new, l, acc

    init = (jnp.full((rows, 1), NEG_INF, F32), jnp.zeros((rows, 1), F32), jnp.zeros((rows, MLA_KV_LORA), F32))
    n_full = (i * tq) // tk
    carry = lax.fori_loop(0, n_full, lambda c, cr: chunk(c, cr, False), init)
    _, l, acc = chunk(n_full, carry, True)
    o = acc / l
    parts = []
    for p in range(MLA_HEADS // 2):
        pair = jnp.concatenate([o[(2 * p) * tq:(2 * p + 1) * tq], o[(2 * p + 1) * tq:(2 * p + 2) * tq]], axis=1)
        parts.append(_dot(pair, wuv_ref[p]))
    o_ref[...] = res_ref[...] + _dot(jnp.concatenate(parts, axis=1), wo_ref[...])


def _mla_attn_prompt(qabs, lat, wuv, wo, res, bsz, seq, tq=128, tk=512):
    m, d = res.shape
    nq = seq // tq
    assert tk % tq == 0
    return pl.pallas_call(
        functools.partial(_mla_attn_kernel, tq, tk), grid=(bsz, nq),
        in_specs=[pl.BlockSpec((MLA_HEADS, tq, MLA_LATENT), lambda b, i: (0, b * nq + i, 0)),
                  pl.BlockSpec((seq, MLA_LATENT), lambda b, i: (b, 0)), _const_spec(wuv.shape), _const_spec(wo.shape),
                  pl.BlockSpec((tq, d), lambda b, i: (b * nq + i, 0))],
        out_specs=pl.BlockSpec((tq, d), lambda b, i: (b * nq + i, 0)),
        out_shape=jax.ShapeDtypeStruct((m, d), F32),
        compiler_params=_cparams(("parallel", "parallel")), name="mla_attn")(qabs, lat, wuv, wo, res)


def _mla_decode_kernel(pages_per_chunk, n_pages, page, pt_ref, q_ref, new_ref, cache_ref, o_ref, buf, sem):
    b = pl.program_id(0)
    n_chunks = n_pages // pages_per_chunk

    def copies(seq_idx, c, slot):
        return [pltpu.make_async_copy(cache_ref.at[pt_ref[seq_idx, c * pages_per_chunk + j]],
                                      buf.at[slot, :, pl.ds(j * page, page)], sem.at[slot])
                for j in range(pages_per_chunk)]

    @pl.when(b == 0)
    def _():
        for cp in copies(0, 0, 0):
            cp.start()

    q = (q_ref[0] * (MLA_SCALE * LOG2E)).astype(BF16)
    m = jnp.full((MLA_HEADS, 1), NEG_INF, F32)
    l = jnp.zeros((MLA_HEADS, 1), F32)
    acc = jnp.zeros((MLA_HEADS, MLA_KV_LORA), F32)
    for c in range(n_chunks):
        slot = c % 2
        if c + 1 < n_chunks:
            for cp in copies(b, c + 1, 1 - slot):
                cp.start()
        else:
            @pl.when(b + 1 < pl.num_programs(0))
            def _():
                for cp in copies(b + 1, 0, 1 - slot):
                    cp.start()
        for cp in copies(b, c, slot):
            cp.wait()
        keys_t = buf[slot].astype(BF16)
        s = _dot(q, keys_t)
        m_new = jnp.maximum(m, jnp.max(s, axis=-1, keepdims=True))
        alpha = jnp.exp2(m - m_new)
        p = jnp.exp2(s - m_new)
        l = alpha * l + jnp.sum(p, axis=-1, keepdims=True)
        acc = alpha * acc + _dot_nt(p, keys_t[:MLA_KV_LORA, :])
        m = m_new
    new = new_ref[0].astype(BF16).astype(F32)
    s_new = jnp.sum(q.astype(F32) * new, axis=-1, keepdims=True)
    m_new = jnp.maximum(m, s_new)
    alpha = jnp.exp2(m - m_new)
    p_new = jnp.exp2(s_new - m_new)
    l = alpha * l + p_new
    acc = alpha * acc + p_new.astype(BF16).astype(F32) * new[:, :MLA_KV_LORA]
    o_ref[0] = acc / l


def _mla_attn_sample(q, new_rows, cache_t, page_table, pages_per_chunk=32):
    bsz = q.shape[0]
    n_pages = page_table.shape[1]
    page = cache_t.shape[2]
    pages_per_chunk = min(pages_per_chunk, n_pages // 2)
    assert n_pages % (2 * pages_per_chunk) == 0
    grid_spec = pltpu.PrefetchScalarGridSpec(
        num_scalar_prefetch=1, grid=(bsz,),
        in_specs=[pl.BlockSpec((1, MLA_HEADS, MLA_LATENT), lambda b, pt: (b, 0, 0)),
                  pl.BlockSpec((1, 1, MLA_LATENT), lambda b, pt: (b, 0, 0)),
                  pl.BlockSpec(memory_space=pl.ANY)],
        out_specs=pl.BlockSpec((1, MLA_HEADS, MLA_KV_LORA), lambda b, pt: (b, 0, 0)),
        scratch_shapes=[pltpu.VMEM((2, MLA_LATENT, pages_per_chunk * page), F32), pltpu.SemaphoreType.DMA((2,))])
    return pl.pallas_call(
        functools.partial(_mla_decode_kernel, pages_per_chunk, n_pages, page), grid_spec=grid_spec,
        out_shape=jax.ShapeDtypeStruct((bsz, MLA_HEADS, MLA_KV_LORA), F32),
        compiler_params=_cparams(("arbitrary",)), name="mla_decode")(page_table, q, new_rows, cache_t)


def _mla_out_kernel(o_ref, wuv_ref, wo_ref, res_ref, out_ref):
    parts = []
    for p in range(MLA_HEADS // 2):
        pair = jnp.concatenate([o_ref[2 * p], o_ref[2 * p + 1]], axis=1)
        parts.append(_dot(pair, wuv_ref[p]))
    out_ref[...] = res_ref[...] + _dot(jnp.concatenate(parts, axis=1), wo_ref[...])


def _mla_out(o_lat, wuv, wo, res, tm=256):
    m, d = res.shape
    tm = min(tm, m)
    return pl.pallas_call(
        _mla_out_kernel, grid=(m // tm,),
        in_specs=[pl.BlockSpec((MLA_HEADS, tm, MLA_KV_LORA), lambda i: (0, i, 0)), _const_spec(wuv.shape),
                  _const_spec(wo.shape), pl.BlockSpec((tm, d), lambda i: (i, 0))],
        out_specs=pl.BlockSpec((tm, d), lambda i: (i, 0)),
        out_shape=jax.ShapeDtypeStruct((m, d), F32),
        compiler_params=_cparams(("parallel",)), name="mla_out")(o_lat, wuv, wo, res)


def _conv_tail(y, lng, lnb, wout, bout):
    mu = jnp.mean(y, axis=-1, keepdims=True)
    var = jnp.mean(jnp.square(y - mu), axis=-1, keepdims=True)
    yn = (y - mu) * lax.rsqrt(var + LN_EPS) * lng + lnb
    return _dot(_silu(yn), wout) + bout


def _conv_prompt_kernel(ts, x_ref, g_ref, win_ref, bin_ref, wdw_ref, bdw_ref, lng_ref, lnb_ref, wout_ref, bout_ref,
                        o_ref, st_ref, ext):
    j = pl.program_id(1)

    @pl.when(j == 0)
    def _():
        ext[0:CONV_HALO, :] = jnp.zeros((CONV_HALO, D_MODEL), F32)

    x = x_ref[...]
    a = _dot(_rms(x, g_ref[...]), win_ref[...]) + bin_ref[...]
    ext[CONV_HALO:CONV_HALO + ts, :] = a[:, :D_MODEL] * jax.nn.sigmoid(a[:, D_MODEL:])
    base = CONV_HALO - (CONV_WIDTH - 1)
    sub = 8
    y = bdw_ref[...]
    for phase in range(sub):
        taps = [r for r in range(phase, base + CONV_WIDTH, sub) if r >= base]
        span = ts + (sub if phase else 0)
        part = None
        for r in taps:
            term = wdw_ref[r - base:r - base + 1, :] * ext[r - phase:r - phase + span, :]
            part = term if part is None else part + term
        y = y + part[phase:phase + ts]
    o_ref[...] = x + _conv_tail(y, lng_ref[...], lnb_ref[...], wout_ref[...], bout_ref[...])

    @pl.when(j == pl.num_programs(1) - 1)
    def _():
        st_ref[0] = ext[CONV_HALO + ts - (CONV_WIDTH - 1):CONV_HALO + ts, :]

    ext[0:CONV_HALO, :] = ext[ts:ts + CONV_HALO, :]


def _conv_prompt(x, g, w, bsz, seq, ts=256):
    m, d = x.shape
    ns = seq // ts
    consts = [g.reshape(1, d), w["win"], w["bin"], w["wdw"], w["bdw"], w["lng"], w["lnb"], w["wout"], w["bout"]]
    return pl.pallas_call(
        functools.partial(_conv_prompt_kernel, ts), grid=(bsz, ns),
        in_specs=[pl.BlockSpec((ts, d), lambda b, j: (b * ns + j, 0))] + [_const_spec(c.shape) for c in consts],
        out_specs=[pl.BlockSpec((ts, d), lambda b, j: (b * ns + j, 0)),
                   pl.BlockSpec((1, CONV_WIDTH - 1, d), lambda b, j: (b, 0, 0))],
        out_shape=[jax.ShapeDtypeStruct((m, d), F32), jax.ShapeDtypeStruct((bsz, CONV_WIDTH - 1, d), F32)],
        scratch_shapes=[pltpu.VMEM((CONV_HALO + ts, d), F32)],
        compiler_params=_cparams(("parallel", "arbitrary")), name="conv_prompt")(x, *consts)


def _conv_sample_kernel(x_ref, g_ref, win_ref, bin_ref, st_ref, wdw_ref, bdw_ref, lng_ref, lnb_ref, wout_ref,
                        bout_ref, o_ref, u_ref):
    x = x_ref[...]
    a = _dot(_rms(x, g_ref[...]), win_ref[...]) + bin_ref[...]
    u = a[:, :D_MODEL] * jax.nn.sigmoid(a[:, D_MODEL:])
    u_ref[...] = u
    y = jnp.zeros_like(u)
    for k in range(CONV_WIDTH - 1):
        y = y + wdw_ref[k:k + 1, :] * st_ref[k]
    y = y + wdw_ref[CONV_WIDTH - 1:CONV_WIDTH, :] * u + bdw_ref[...]
    o_ref[...] = x + _conv_tail(y, lng_ref[...], lnb_ref[...], wout_ref[...], bout_ref[...])


def _conv_sample(x, g, w, state_t):
    m, d = x.shape
    args = [x, g.reshape(1, d), w["win"], w["bin"], state_t, w["wdw"], w["bdw"], w["lng"], w["lnb"], w["wout"],
            w["bout"]]
    return pl.pallas_call(
        _conv_sample_kernel, grid=(1,),
        in_specs=[_const_spec(a.shape) for a in args],
        out_specs=[_const_spec((m, d)), _const_spec((m, d))],
        out_shape=[jax.ShapeDtypeStruct((m, d), F32), jax.ShapeDtypeStruct((m, d), F32)],
        compiler_params=_cparams(("arbitrary",)), name="conv_sample")(*args)


def _ssm_in_prompt_kernel(ts, x_ref, g_ref, wz_ref, wxbc_ref, wdt_ref, cw_ref, cb_ref, dtb_ref,
                          z_ref, act_ref, dt_ref, cst_ref, ext):
    j = pl.program_id(1)

    @pl.when(j == 0)
    def _():
        ext[0:SSM_HALO, :] = jnp.zeros((SSM_HALO, SSM_CONV_DIM), F32)

    xn = _rms(x_ref[...], g_ref[...]).astype(BF16)
    z_ref[...] = jnp.dot(xn, wz_ref[...], preferred_element_type=F32)
    dt_ref[...] = jax.nn.softplus(jnp.dot(xn, wdt_ref[...], preferred_element_type=F32) + dtb_ref[...])
    ext[SSM_HALO:SSM_HALO + ts, :] = jnp.dot(xn, wxbc_ref[...], preferred_element_type=F32)
    base = SSM_HALO - (SSM_CONV - 1)
    y = jnp.zeros((ts, SSM_CONV_DIM), F32)
    for k in range(SSM_CONV):
        y = y + cw_ref[k:k + 1, :] * ext[base + k:base + k + ts, :]
    act_ref[...] = _silu(y + cb_ref[...])

    @pl.when(j == pl.num_programs(1) - 1)
    def _():
        cst_ref[0] = ext[SSM_HALO + ts - (SSM_CONV - 1):SSM_HALO + ts, :]

    ext[0:SSM_HALO, :] = ext[ts:ts + SSM_HALO, :]


def _ssm_in_prompt(x, g, w, bsz, seq, ts=256):
    m, d = x.shape
    ns = seq // ts
    consts = [g.reshape(1, d), w["wz"], w["wxbc"], w["wdt"], w["cw"], w["cb"], w["dtb"]]
    row = lambda b, j: (b * ns + j, 0)
    return pl.pallas_call(
        functools.partial(_ssm_in_prompt_kernel, ts), grid=(bsz, ns),
        in_specs=[pl.BlockSpec((ts, d), row)] + [_const_spec(c.shape) for c in consts],
        out_specs=[pl.BlockSpec((ts, D_INNER), row), pl.BlockSpec((ts, SSM_CONV_DIM), row),
                   pl.BlockSpec((ts, SSM_HEADS), row),
                   pl.BlockSpec((1, SSM_CONV - 1, SSM_CONV_DIM), lambda b, j: (b, 0, 0))],
        out_shape=[jax.ShapeDtypeStruct((m, D_INNER), F32), jax.ShapeDtypeStruct((m, SSM_CONV_DIM), F32),
                   jax.ShapeDtypeStruct((m, SSM_HEADS), F32),
                   jax.ShapeDtypeStruct((bsz, SSM_CONV - 1, SSM_CONV_DIM), F32)],
        scratch_shapes=[pltpu.VMEM((SSM_HALO + ts, SSM_CONV_DIM), F32)],
        compiler_params=_cparams(("parallel", "arbitrary")), name="ssm_in_prompt")(x, *consts)


def _ssd_scan_kernel(act_ref, dt_ref, alog_ref, d_ref, y_ref, h_ref):
    c = pl.program_id(1)
    L = SSM_CHUNK

    @pl.when(c == 0)
    def _():
        h_ref[...] = jnp.zeros(h_ref.shape, F32)

    dt = dt_ref[...]
    da = dt * (-jnp.exp(alog_ref[...]))
    ri = lax.broadcasted_iota(jnp.int32, (L, L), 0)
    ci = lax.broadcasted_iota(jnp.int32, (L, L), 1)
    tril = ri >= ci
    cs = _dot_exact(tril.astype(F32), da)
    eye = (lax.broadcasted_iota(jnp.int32, (SSM_HEADS, SSM_HEADS), 0)
           == lax.broadcasted_iota(jnp.int32, (SSM_HEADS, SSM_HEADS), 1)).astype(F32)
    cs_t = _dot_nt_exact(eye, cs)
    dt_t = _dot_nt_exact(eye, dt)
    w_t = dt_t * jnp.exp(cs_t[:, L - 1:L] - cs_t)
    cdec = jnp.exp(cs[L - 1:L, :])
    dvec = d_ref[...]
    for g in range(SSM_GROUPS):
        bg = act_ref[:, D_INNER + g * SSM_STATE:D_INNER + (g + 1) * SSM_STATE]
        cg = act_ref[:, D_INNER + SSM_GN + g * SSM_STATE:D_INNER + SSM_GN + (g + 1) * SSM_STATE]
        cb = _dot_nt(cg, bg)
        bg_t = bg.T
        for r8 in range(SSM_REP):
            r = g * SSM_REP + r8
            xr = act_ref[:, r * SSM_HD:(r + 1) * SSM_HD]
            h_t = h_ref[0, r]
            csb = jnp.broadcast_to(cs[:, r:r + 1], (L, L))
            lmat = jnp.exp(jnp.where(tril, csb - cs_t[r:r + 1, :], NEG_INF)) * (cb * dt_t[r:r + 1, :])
            lhs = jnp.concatenate([lmat, cg * jnp.exp(csb)], axis=1)
            rhs = jnp.concatenate([xr, h_t], axis=0)
            y_ref[:, r * SSM_HD:(r + 1) * SSM_HD] = _dot(lhs, rhs) + xr * dvec[:, r:r + 1]
            h_ref[0, r] = cdec[:, r:r + 1] * h_t + _dot(bg_t * w_t[r:r + 1, :], xr)


def _ssd_scan(act, dt, alog, dvec, bsz, seq):
    m = act.shape[0]
    nc = seq // SSM_CHUNK
    row = lambda b, c: (b * nc + c, 0)
    return pl.pallas_call(
        _ssd_scan_kernel, grid=(bsz, nc),
        in_specs=[pl.BlockSpec((SSM_CHUNK, SSM_CONV_DIM), row), pl.BlockSpec((SSM_CHUNK, SSM_HEADS), row),
                  _const_spec((1, SSM_HEADS)), _const_spec((1, SSM_HEADS))],
        out_specs=[pl.BlockSpec((SSM_CHUNK, D_INNER), row),
                   pl.BlockSpec((1, SSM_HEADS, SSM_STATE, SSM_HD), lambda b, c: (b, 0, 0, 0))],
        out_shape=[jax.ShapeDtypeStruct((m, D_INNER), F32),
                   jax.ShapeDtypeStruct((bsz, SSM_HEADS, SSM_STATE, SSM_HD), F32)],
        compiler_params=_cparams(("parallel", "arbitrary")), name="ssd_scan")(
            act, dt, alog.reshape(1, SSM_HEADS), dvec.reshape(1, SSM_HEADS))


def _ssm_out_kernel(y_ref, z_ref, n_ref, w_ref, res_ref, o_ref):
    yz = y_ref[...] * _silu(z_ref[...])
    gw = D_INNER // SSM_GROUPS
    parts = []
    for g in range(SSM_GROUPS):
        seg = yz[:, g * gw:(g + 1) * gw]
        parts.append(seg * lax.rsqrt(jnp.mean(seg * seg, axis=-1, keepdims=True) + RMS_EPS))
    o_ref[...] = res_ref[...] + _dot(jnp.concatenate(parts, axis=1) * n_ref[...], w_ref[...])


def _ssm_out(y, z, norm, wout, res, tm=256):
    m, d = res.shape
    tm = min(tm, m)
    row = lambda i: (i, 0)
    return pl.pallas_call(
        _ssm_out_kernel, grid=(m // tm,),
        in_specs=[pl.BlockSpec((tm, D_INNER), row), pl.BlockSpec((tm, D_INNER), row), _const_spec((1, D_INNER)),
                  _const_spec(wout.shape), pl.BlockSpec((tm, d), row)],
        out_specs=pl.BlockSpec((tm, d), row),
        out_shape=jax.ShapeDtypeStruct((m, d), F32),
        compiler_params=_cparams(("parallel",)), name="ssm_out")(y, z, norm.reshape(1, D_INNER), wout, res)


def _ssm_in_sample_kernel(x_ref, g_ref, wz_ref, wxbc_ref, wdt_ref, cw_ref, cb_ref, dtb_ref, st_ref,
                          z_ref, xbc_ref, act_ref, dt_ref):
    xn = _rms(x_ref[...], g_ref[...]).astype(BF16)
    z_ref[...] = jnp.dot(xn, wz_ref[...], preferred_element_type=F32)
    dt_ref[...] = jax.nn.softplus(jnp.dot(xn, wdt_ref[...], preferred_element_type=F32) + dtb_ref[...])
    xbc = jnp.dot(xn, wxbc_ref[...], preferred_element_type=F32)
    xbc_ref[...] = xbc
    y = jnp.zeros_like(xbc)
    for k in range(SSM_CONV - 1):
        y = y + cw_ref[k:k + 1, :] * st_ref[k]
    y = y + cw_ref[SSM_CONV - 1:SSM_CONV, :] * xbc
    act_ref[...] = _silu(y + cb_ref[...])


def _ssm_in_sample(x, g, w, state_t):
    m, d = x.shape
    args = [x, g.reshape(1, d), w["wz"], w["wxbc"], w["wdt"], w["cw"], w["cb"], w["dtb"], state_t]
    shapes = [(m, D_INNER), (m, SSM_CONV_DIM), (m, SSM_CONV_DIM), (m, SSM_HEADS)]
    return pl.pallas_call(
        _ssm_in_sample_kernel, grid=(1,),
        in_specs=[_const_spec(a.shape) for a in args],
        out_specs=[_const_spec(s) for s in shapes],
        out_shape=[jax.ShapeDtypeStruct(s, F32) for s in shapes],
        compiler_params=_cparams(("arbitrary",)), name="ssm_in_sample")(*args)


def _ssm_step_kernel(act_ref, dt_ref, alog_ref, d_ref, h_ref, y_ref, hn_ref):
    act = act_ref[0]
    dt = dt_ref[0]
    decay = jnp.exp(dt * (-jnp.exp(alog_ref[...])))
    dvec = d_ref[...]
    eye = (lax.broadcasted_iota(jnp.int32, (SSM_HD, SSM_HD), 0)
           == lax.broadcasted_iota(jnp.int32, (SSM_HD, SSM_HD), 1))
    for g in range(SSM_GROUPS):
        bg = act[:, D_INNER + g * SSM_STATE:D_INNER + (g + 1) * SSM_STATE]
        cg = jnp.broadcast_to(act[:, D_INNER + SSM_GN + g * SSM_STATE:D_INNER + SSM_GN + (g + 1) * SSM_STATE],
                              (8, SSM_STATE))
        for r8 in range(SSM_REP):
            r = g * SSM_REP + r8
            xr = act[:, r * SSM_HD:(r + 1) * SSM_HD]
            xdt = xr * dt[:, r:r + 1]
            xcol = jnp.sum(jnp.where(eye, jnp.broadcast_to(xdt, (SSM_HD, SSM_HD)), 0.0), axis=1, keepdims=True)
            h_new = decay[:, r:r + 1] * h_ref[0, r] + xcol * bg
            hn_ref[0, r] = h_new
            y = _dot_nt(cg, h_new)[0:1] + xr * dvec[:, r:r + 1]
            y_ref[0, :, r * SSM_HD:(r + 1) * SSM_HD] = y


def _ssm_step(act, dt, alog, dvec, state):
    bsz = act.shape[0]
    blk3 = lambda b: (b, 0, 0)
    y, hn = pl.pallas_call(
        _ssm_step_kernel, grid=(bsz,),
        in_specs=[pl.BlockSpec((1, 1, SSM_CONV_DIM), blk3), pl.BlockSpec((1, 1, SSM_HEADS), blk3),
                  _const_spec((1, SSM_HEADS)), _const_spec((1, SSM_HEADS)),
                  pl.BlockSpec((1, SSM_HEADS, SSM_HD, SSM_STATE), lambda b: (b, 0, 0, 0))],
        out_specs=[pl.BlockSpec((1, 1, D_INNER), blk3),
                   pl.BlockSpec((1, SSM_HEADS, SSM_HD, SSM_STATE), lambda b: (b, 0, 0, 0))],
        out_shape=[jax.ShapeDtypeStruct((bsz, 1, D_INNER), F32),
                   jax.ShapeDtypeStruct((bsz, SSM_HEADS, SSM_HD, SSM_STATE), F32)],
        compiler_params=_cparams(("parallel",)), name="ssm_step")(
            act.reshape(bsz, 1, SSM_CONV_DIM), dt.reshape(bsz, 1, SSM_HEADS), alog.reshape(1, SSM_HEADS),
            dvec.reshape(1, SSM_HEADS), state)
    return y.reshape(bsz, D_INNER), hn


def _dil_rope_qk(h, cos, sin):
    tiles = []
    for t in range(2 * DIL_WIDTH // LANE):
        tile = h[:, t * LANE:(t + 1) * LANE]
        tiles.append(tile * cos + _rot_half(tile, DIL_HD) * sin)
    return jnp.concatenate(tiles, axis=1)


def _dil_qkv_kernel(x_ref, g_ref, w_ref, cos_ref, sin_ref, q0_ref, q1_ref, q2_ref, kv0_ref, kv1_ref, kv2_ref):
    h = _dot(_rms(x_ref[...], g_ref[...]), w_ref[...])
    qk = _dil_rope_qk(h, cos_ref[...], sin_ref[...])
    for g, (q_ref, kv_ref) in enumerate(((q0_ref, kv0_ref), (q1_ref, kv1_ref), (q2_ref, kv2_ref))):
        q_ref[...] = qk[:, g * DIL_GW:(g + 1) * DIL_GW]
        kv_ref[:, :DIL_GW] = qk[:, DIL_WIDTH + g * DIL_GW:DIL_WIDTH + (g + 1) * DIL_GW]
        kv_ref[:, DIL_GW:] = h[:, 2 * DIL_WIDTH + g * DIL_GW:2 * DIL_WIDTH + (g + 1) * DIL_GW]


def _lane_chunks(width):
    return [(c, min(LANE, width - c)) for c in range(0, width, LANE)]


def _store_by_residue(val, dil, perm, out_ref):
    tm, w = val.shape
    if dil == 1:
        out_ref[0, 0] = val
        return
    for i, (c, wc) in enumerate(_lane_chunks(w)):
        perm[i, :, :wc] = val[:, c:c + wc]
    for r in range(dil):
        for i, (c, wc) in enumerate(_lane_chunks(w)):
            out_ref[0, r, :, c:c + wc] = perm[i, pl.ds(r, tm // dil, stride=dil), :][:, :wc]


def _load_by_position(in_ref, dil, perm):
    _, _, sub, w = in_ref.shape
    if dil == 1:
        return in_ref[0, 0]
    for r in range(dil):
        for i, (c, wc) in enumerate(_lane_chunks(w)):
            perm[i, pl.ds(r, sub, stride=dil), :] = in_ref[0, r, :, c:c + wc]
    return jnp.concatenate([perm[i] for i in range(len(_lane_chunks(w)))], axis=1)


def _dil_qkv_prompt_kernel(x_ref, g_ref, w_ref, cos_ref, sin_ref, q0_ref, q1_ref, q2_ref, kv0_ref, kv1_ref, kv2_ref,
                           pos1_ref, pos2_ref, perm):
    h = _dot(_rms(x_ref[...], g_ref[...]), w_ref[...])
    qk = _dil_rope_qk(h, cos_ref[...], sin_ref[...])
    pos_refs = (None, pos1_ref, pos2_ref)
    for g, (q_ref, kv_ref) in enumerate(((q0_ref, kv0_ref), (q1_ref, kv1_ref), (q2_ref, kv2_ref))):
        dil = DIL_GROUPS[g][1]
        kv = jnp.concatenate([qk[:, DIL_WIDTH + g * DIL_GW:DIL_WIDTH + (g + 1) * DIL_GW],
                              h[:, 2 * DIL_WIDTH + g * DIL_GW:2 * DIL_WIDTH + (g + 1) * DIL_GW]], axis=1)
        if pos_refs[g] is not None:
            pos_refs[g][...] = kv
        _store_by_residue(qk[:, g * DIL_GW:(g + 1) * DIL_GW], dil, perm, q_ref)
        _store_by_residue(kv, dil, perm, kv_ref)


def _dil_qkv_prompt(x, g, w, cos, sin, bsz, seq, tm=256):
    m, d = x.shape
    nj = seq // tm
    row = lambda i: (i, 0)
    res_blk = lambda i: (i // nj, 0, i % nj, 0)
    dils = [dl for _, dl in DIL_GROUPS]
    shapes = ([(bsz, dl, seq // dl, DIL_GW) for dl in dils] + [(bsz, dl, seq // dl, 2 * DIL_GW) for dl in dils])
    out_specs = ([pl.BlockSpec((1, dl, tm // dl, DIL_GW), res_blk) for dl in dils]
                 + [pl.BlockSpec((1, dl, tm // dl, 2 * DIL_GW), res_blk) for dl in dils]
                 + [pl.BlockSpec((tm, 2 * DIL_GW), row)] * 2)
    return pl.pallas_call(
        _dil_qkv_prompt_kernel, grid=(m // tm,),
        in_specs=[pl.BlockSpec((tm, d), row), _const_spec((1, d)), _const_spec(w.shape)]
        + [pl.BlockSpec((tm, LANE), lambda i: (i % nj, 0))] * 2,
        out_specs=out_specs,
        out_shape=[jax.ShapeDtypeStruct(sh, F32) for sh in shapes] + [jax.ShapeDtypeStruct((m, 2 * DIL_GW), F32)] * 2,
        scratch_shapes=[pltpu.VMEM((2 * DIL_GW // LANE, tm, LANE), F32)],
        compiler_params=_cparams(("parallel",)), name="dil_qkv_prompt")(x, g.reshape(1, d), w, cos, sin)


def _dil_qkv(x, g, w, cos, sin, tm=256):
    m, d = x.shape
    tm = min(tm, m)
    nt = cos.shape[0] // tm
    row = lambda i: (i, 0)
    widths = [DIL_GW] * N_DIL + [2 * DIL_GW] * N_DIL
    return pl.pallas_call(
        _dil_qkv_kernel, grid=(m // tm,),
        in_specs=[pl.BlockSpec((tm, d), row), _const_spec((1, d)), _const_spec(w.shape)]
        + [pl.BlockSpec((tm, LANE), lambda i: (i % nt, 0))] * 2,
        out_specs=[pl.BlockSpec((tm, wd), row) for wd in widths],
        out_shape=[jax.ShapeDtypeStruct((m, wd), F32) for wd in widths],
        compiler_params=_cparams(("parallel",)), name="dil_qkv")(x, g.reshape(1, d), w, cos, sin)


def _dil_band_kernel(n_sub, sub_len, q_ref, kv_ref, o_ref):
    nk = DIL_NK
    qi = lax.broadcasted_iota(jnp.int32, (nk, 2 * nk), 0)
    ki = lax.broadcasted_iota(jnp.int32, (nk, 2 * nk), 1)
    band = (ki <= qi + nk) & (ki >= qi)
    causal = lax.broadcasted_iota(jnp.int32, (nk, nk), 1) <= lax.broadcasted_iota(jnp.int32, (nk, nk), 0)
    for u in range(n_sub):
        for blk in range(sub_len // nk):
            rows = pl.ds(blk * nk, nk)
            q = q_ref[u, rows, :]
            if blk == 0:
                kvb, mask = kv_ref[u, rows, :].astype(BF16), causal
            else:
                kvb, mask = kv_ref[u, pl.ds((blk - 1) * nk, 2 * nk), :].astype(BF16), band
            heads = range(DIL_HEADS)
            s = jnp.concatenate([_dot_nt(q[:, h * DIL_HD:(h + 1) * DIL_HD], kvb[:, h * DIL_HD:(h + 1) * DIL_HD])
                                 for h in heads], axis=0) * DIL_HD ** -0.5
            s = jnp.where(jnp.concatenate([mask] * DIL_HEADS, axis=0), s, NEG_INF)
            mx = jnp.max(s, axis=-1, keepdims=True)
            e = jnp.exp(s - mx)
            l = jnp.sum(e, axis=-1, keepdims=True)
            p = e * (1.0 / l)
            lse = mx + jnp.log(l)
            o_ref[u, rows, DIL_GW:] = jnp.zeros((nk, DIL_OW - DIL_GW), F32)
            for h in heads:
                cols = slice(h * DIL_HD, (h + 1) * DIL_HD)
                o_ref[u, rows, cols] = _dot(p[h * nk:(h + 1) * nk], kvb[:, DIL_GW + h * DIL_HD:DIL_GW + (h + 1) * DIL_HD])
                o_ref[u, rows, DIL_GW + h:DIL_GW + h + 1] = lse[h * nk:(h + 1) * nk]


def _dil_prompt(q, kv, dil):
    bsz, _, sub_len, _ = q.shape
    blk = lambda b: (b, 0, 0)
    n_sub = max(1, min(dil, 16 * DIL_NK // sub_len))
    o = pl.pallas_call(
        functools.partial(_dil_band_kernel, n_sub, sub_len), grid=(bsz * dil // n_sub,),
        in_specs=[pl.BlockSpec((n_sub, sub_len, DIL_GW), blk), pl.BlockSpec((n_sub, sub_len, 2 * DIL_GW), blk)],
        out_specs=pl.BlockSpec((n_sub, sub_len, DIL_OW), blk),
        out_shape=jax.ShapeDtypeStruct((bsz * dil, sub_len, DIL_OW), F32),
        compiler_params=_cparams(("parallel",)), name="dil_band_d%d" % dil)(
            q.reshape(bsz * dil, sub_len, DIL_GW), kv.reshape(bsz * dil, sub_len, 2 * DIL_GW))
    return o.reshape(bsz, dil, sub_len, DIL_OW)


def _dil_out_kernel(o0_ref, o1_ref, o2_ref, w_ref, res_ref, out_ref, perm):
    o = [_load_by_position(r, dl, perm) for r, (_, dl) in zip((o0_ref, o1_ref, o2_ref), DIL_GROUPS)]
    tails = [x[:, DIL_OW - LANE:] for x in o]
    off = DIL_GW - (DIL_OW - LANE)
    top = jnp.maximum(jnp.maximum(tails[0], tails[1]), tails[2])
    es = [jnp.exp(t - top) for t in tails]
    den = es[0] + es[1] + es[2]
    y = res_ref[...]
    for g in range(N_DIL):
        share = es[g] / den
        part = jnp.concatenate([o[g][:, h * DIL_HD:(h + 1) * DIL_HD] * share[:, off + h:off + h + 1]
                                for h in range(DIL_HEADS)], axis=1)
        y = y + _dot(part, w_ref[g])
    out_ref[...] = y


def _dil_out(outs, w, res, seq, tm=256):
    m, d = res.shape
    nj = seq // tm
    row = lambda i: (i, 0)
    res_blk = lambda i: (i // nj, 0, i % nj, 0)
    return pl.pallas_call(
        _dil_out_kernel, grid=(m // tm,),
        in_specs=[pl.BlockSpec((1, dl, tm // dl, DIL_OW), res_blk) for _, dl in DIL_GROUPS]
        + [_const_spec(w.shape), pl.BlockSpec((tm, d), row)],
        out_specs=pl.BlockSpec((tm, d), row),
        out_shape=jax.ShapeDtypeStruct((m, d), F32),
        scratch_shapes=[pltpu.VMEM((DIL_OW // LANE, tm, LANE), F32)],
        compiler_params=_cparams(("parallel",)), name="dil_out")(*outs, w, res)


def _dil_sample_kernel(q0_ref, q1_ref, q2_ref, n0_ref, n1_ref, n2_ref, b0_ref, b1_ref, b2_ref,
                       o_ref, nb0_ref, nb1_ref, nb2_ref):
    eye = (lax.broadcasted_iota(jnp.int32, (DIL_HD, DIL_HD), 0) == lax.broadcasted_iota(jnp.int32, (DIL_HD, DIL_HD), 1))

    def to_col(row):
        return jnp.sum(jnp.where(eye, jnp.broadcast_to(row, (DIL_HD, DIL_HD)), 0.0), axis=1, keepdims=True)

    def to_row(col):
        return jnp.sum(jnp.where(eye, jnp.broadcast_to(col, (DIL_HD, DIL_HD)), 0.0), axis=0, keepdims=True)

    scale = DIL_HD ** -0.5
    heads = range(DIL_HEADS)
    outs, lses = [], []
    for (win, dil), q_ref, n_ref, b_ref, nb_ref in zip(DIL_GROUPS, (q0_ref, q1_ref, q2_ref), (n0_ref, n1_ref, n2_ref),
                                                       (b0_ref, b1_ref, b2_ref), (nb0_ref, nb1_ref, nb2_ref)):
        rows = b_ref.shape[-1]
        reads = (lax.broadcasted_iota(jnp.int32, (1, rows), 1) & (dil - 1)) == 0
        last = lax.broadcasted_iota(jnp.int32, (DIL_HD, rows), 1) == rows - 1
        q = q_ref[0]
        new = n_ref[0]
        qh = [q[:, h * DIL_HD:(h + 1) * DIL_HD] for h in heads]
        k_new = [new[:, h * DIL_HD:(h + 1) * DIL_HD] for h in heads]
        v_new = [new[:, DIL_GW + h * DIL_HD:DIL_GW + (h + 1) * DIL_HD] for h in heads]
        q_col = [to_col(x) for x in qh]
        s = jnp.concatenate([jnp.sum(b_ref[0, 0, h] * q_col[h], axis=0, keepdims=True) for h in heads], axis=0)
        s = jnp.where(reads, s * scale, NEG_INF)
        s_new = jnp.concatenate([jnp.sum(qh[h] * k_new[h], axis=-1, keepdims=True) for h in heads], axis=0) * scale
        mx = jnp.maximum(jnp.max(s, axis=-1, keepdims=True), s_new)
        lse = mx + jnp.log(jnp.sum(jnp.exp(s - mx), axis=-1, keepdims=True) + jnp.exp(s_new - mx))
        p = jnp.exp(s - lse)
        p_new = jnp.exp(s_new - lse)
        outs.append([to_row(jnp.sum(b_ref[0, 1, h] * p[h:h + 1], axis=1, keepdims=True)) + p_new[h:h + 1] * v_new[h]
                     for h in heads])
        lses.append(lse)
        for h in heads:
            nb_ref[0, 0, h] = jnp.where(last, to_col(k_new[h]), pltpu.roll(b_ref[0, 0, h], rows - 1, 1))
            nb_ref[0, 1, h] = jnp.where(last, to_col(v_new[h]), pltpu.roll(b_ref[0, 1, h], rows - 1, 1))
    top = jnp.maximum(jnp.maximum(lses[0], lses[1]), lses[2])
    es = [jnp.exp(x - top) for x in lses]
    den = es[0] + es[1] + es[2]
    o_ref[0] = jnp.concatenate([outs[g][h] * (es[g][h:h + 1] / den[h:h + 1]) for g in range(N_DIL) for h in heads],
                               axis=1)


def _dil_sample(qs, news, bufs_t):
    bsz = qs[0].shape[0]
    blk = lambda b: (b, 0, 0)
    blk5 = lambda b: (b, 0, 0, 0, 0)
    small = [q.reshape(bsz, 1, DIL_GW) for q in qs] + [n.reshape(bsz, 1, 2 * DIL_GW) for n in news]
    buf_specs = [pl.BlockSpec((1,) + b.shape[1:], blk5) for b in bufs_t]
    res = pl.pallas_call(
        _dil_sample_kernel, grid=(bsz,),
        in_specs=[pl.BlockSpec((1,) + a.shape[1:], blk) for a in small] + buf_specs,
        out_specs=[pl.BlockSpec((1, 1, DIL_WIDTH), blk)] + buf_specs,
        out_shape=[jax.ShapeDtypeStruct((bsz, 1, DIL_WIDTH), F32)]
        + [jax.ShapeDtypeStruct(b.shape, F32) for b in bufs_t],
        compiler_params=_cparams(("parallel",)), name="dil_sample")(*small, *bufs_t)
    return res[0].reshape(bsz, DIL_WIDTH), res[1:]


def _block_diag_pairs(w):
    h, k, n = w.shape
    z = jnp.zeros((h // 2, k, n), w.dtype)
    top = jnp.concatenate([w[0::2], z], axis=2)
    bot = jnp.concatenate([z, w[1::2]], axis=2)
    return jnp.concatenate([top, bot], axis=1)


def _prep_mla(mla_w_in, mla_norm_q, mla_w_uq, mla_norm_kv, mla_w_uk, mla_w_uv, mla_w_o):
    pad = jnp.zeros((D_MODEL, LANE - MLA_ROPE), F32)
    wuq = mla_w_uq.reshape(MLA_Q_LORA, MLA_HEADS, MLA_NOPE + MLA_ROPE)
    wuq = jnp.concatenate([wuq[:, :, :MLA_NOPE].reshape(MLA_Q_LORA, -1), wuq[:, :, MLA_NOPE:].reshape(MLA_Q_LORA, -1)],
                          axis=1)
    return {
        "win": jnp.concatenate([mla_w_in, pad], axis=1).astype(BF16),
        "nq": mla_norm_q.reshape(1, -1), "nkv": mla_norm_kv.reshape(1, -1),
        "wuq": wuq.astype(BF16),
        "wuk": _block_diag_pairs(jnp.transpose(mla_w_uk, (1, 2, 0))).astype(BF16),
        "wuv": _block_diag_pairs(jnp.transpose(mla_w_uv, (1, 0, 2))).astype(BF16),
        "wo": _cast_bf16(mla_w_o),
    }


def _prep_conv(conv_w_in, conv_b_in, conv_w_dw, conv_b_dw, conv_ln_g, conv_ln_b, conv_w_out, conv_b_out):
    return {"win": _cast_bf16(conv_w_in), "bin": conv_b_in.reshape(1, -1), "wdw": conv_w_dw,
            "bdw": conv_b_dw.reshape(1, -1), "lng": conv_ln_g.reshape(1, -1), "lnb": conv_ln_b.reshape(1, -1),
            "wout": _cast_bf16(conv_w_out), "bout": conv_b_out.reshape(1, -1)}


def _prep_ssm(ssm_w_in, ssm_conv_w, ssm_conv_b, ssm_dt_bias):
    w = ssm_w_in.astype(BF16)
    return {"wz": w[:, :D_INNER], "wxbc": w[:, D_INNER:D_INNER + SSM_CONV_DIM], "wdt": w[:, D_INNER + SSM_CONV_DIM:],
            "cw": ssm_conv_w, "cb": ssm_conv_b.reshape(1, -1), "dtb": ssm_dt_bias.reshape(1, -1)}


def kernel(x_prompt, x_sample, cache_mla_latent, cache_mem_kv, state_conv, state_ssm_conv, state_ssm, state_dil_kv0, state_dil_kv1, state_dil_kv2, page_table, mem_prompt, norm_ffn1, ffn1_w_gu, ffn1_w_down, norm_mix, norm_cross, norm_mem, cross_w_q, cross_w_kv, cross_w_o, norm_ffn2, ffn2_w_gu, ffn2_w_down, norm_final, mla_w_in, mla_norm_q, mla_w_uq, mla_norm_kv, mla_w_uk, mla_w_uv, mla_w_o, conv_w_in, conv_b_in, conv_w_dw, conv_b_dw, conv_ln_g, conv_ln_b, conv_w_out, conv_b_out, ssm_w_in, ssm_conv_w, ssm_conv_b, ssm_dt_bias, ssm_a_log, ssm_d, ssm_norm, ssm_w_out, dil_w_qkv, dil_w_o):
    bsz, seq, d = x_prompt.shape
    dec = x_sample.shape[0]
    assert x_sample.shape[1] == 1 and d == D_MODEL
    depth = norm_ffn1.shape[0]
    past_len = page_table.shape[1] * cache_mla_latent.shape[1]
    pos_p = jnp.arange(seq, dtype=jnp.int32)
    pos_s = jnp.full((dec,), past_len, dtype=jnp.int32)

    res_p = x_prompt.reshape(bsz * seq, d)
    res_s = x_sample.reshape(dec, d)
    mem_len = mem_prompt.shape[1]
    w1gu, w1d = _cast_bf16(ffn1_w_gu), _cast_bf16(ffn1_w_down)
    w2gu, w2d = _cast_bf16(ffn2_w_gu), _cast_bf16(ffn2_w_down)
    xwq, xwo = _cast_bf16(cross_w_q), _cast_bf16(cross_w_o)
    mem_kv = _memkv(mem_prompt.reshape(bsz * mem_len, d), norm_mem, _cast_bf16(cross_w_kv))
    mem_kv_b = mem_kv.reshape(depth * bsz, mem_len, 2 * d)
    outs = {}

    for i in range(depth):
        kind = i % 4
        res_p = _ffn(res_p, norm_ffn1, w1gu, w1d, i)
        res_s = _ffn(res_s, norm_ffn1, w1gu, w1d, i)
        g_mix = norm_mix[i]
        if kind == 0:
            w = _prep_mla(mla_w_in, mla_norm_q, mla_w_uq, mla_norm_kv, mla_w_uk, mla_w_uv, mla_w_o)
            cos_p, sin_p = _rope_tables(pos_p, MLA_ROPE)
            cos_s, sin_s = _rope_tables(pos_s, MLA_ROPE)
            q_p, lat_p = _mla_proj(res_p, g_mix, w, cos_p, sin_p)
            res_p = _mla_attn_prompt(q_p, lat_p, w["wuv"], w["wo"], res_p, bsz, seq)
            q_s, lat_s = _mla_proj(res_s, g_mix, w, cos_s, sin_s)
            o_s = _mla_attn_sample(jnp.transpose(q_s, (1, 0, 2)), lat_s.reshape(dec, 1, MLA_LATENT),
                                   jnp.transpose(cache_mla_latent, (0, 2, 1)), page_table)
            res_s = _mla_out(jnp.transpose(o_s, (1, 0, 2)), w["wuv"], w["wo"], res_s)
            outs["mla_p"] = lat_p.reshape(bsz, seq, MLA_LATENT)
            outs["mla_s"] = lat_s.reshape(dec, 1, MLA_LATENT)
        elif kind == 1:
            w = _prep_conv(conv_w_in, conv_b_in, conv_w_dw, conv_b_dw, conv_ln_g, conv_ln_b, conv_w_out, conv_b_out)
            res_p, outs["conv_p"] = _conv_prompt(res_p, g_mix, w, bsz, seq)
            res_s, u_s = _conv_sample(res_s, g_mix, w, jnp.transpose(state_conv, (1, 0, 2)))
            outs["conv_s"] = jnp.concatenate([state_conv[:, 1:], u_s[:, None, :]], axis=1)
        elif kind == 2:
            w = _prep_ssm(ssm_w_in, ssm_conv_w, ssm_conv_b, ssm_dt_bias)
            wout = _cast_bf16(ssm_w_out)
            z_p, act_p, dt_p, outs["ssmc_p"] = _ssm_in_prompt(res_p, g_mix, w, bsz, seq)
            y_p, h_t = _ssd_scan(act_p, dt_p, ssm_a_log, ssm_d, bsz, seq)
            outs["ssm_p"] = jnp.swapaxes(h_t, 2, 3)
            res_p = _ssm_out(y_p, z_p, ssm_norm, wout, res_p)
            z_s, xbc_s, act_s, dt_s = _ssm_in_sample(res_s, g_mix, w, jnp.transpose(state_ssm_conv, (1, 0, 2)))
            outs["ssmc_s"] = jnp.concatenate([state_ssm_conv[:, 1:], xbc_s[:, None, :]], axis=1)
            y_s, outs["ssm_s"] = _ssm_step(act_s, dt_s, ssm_a_log, ssm_d, state_ssm)
            res_s = _ssm_out(y_s, z_s, ssm_norm, wout, res_s)
        else:
            wqkv = _cast_bf16(dil_w_qkv)
            wo2 = _cast_bf16(dil_w_o)
            wo = wo2.reshape(N_DIL, DIL_GW, d)
            cos_p, sin_p = _rope_tables(pos_p, DIL_HD)
            cos_s, sin_s = _rope_tables(pos_s, DIL_HD)
            qkv_p = _dil_qkv_prompt(res_p, g_mix, wqkv, cos_p, sin_p, bsz, seq)
            qkv_s = _dil_qkv(res_s, g_mix, wqkv, cos_s, sin_s)
            bufs = (state_dil_kv0, state_dil_kv1, state_dil_kv2)
            kv_pos = (qkv_p[N_DIL], qkv_p[2 * N_DIL], qkv_p[2 * N_DIL + 1])
            o_list = []
            for g, (win, dil) in enumerate(DIL_GROUPS):
                o_list.append(_dil_prompt(qkv_p[g], qkv_p[N_DIL + g], dil))
                keep = min(win, seq)
                kv3 = kv_pos[g].reshape(bsz, seq, 2 * DIL_GW)
                outs["dil%d_p" % g] = kv3[:, seq - keep:].reshape(bsz, keep, 2, DIL_HEADS, DIL_HD)
                assert bufs[g].shape[1] == win
            res_p = _dil_out(o_list, wo, res_p, seq)
            o_s, new_bufs = _dil_sample(qkv_s[:N_DIL], qkv_s[N_DIL:], [jnp.transpose(b, (0, 2, 3, 4, 1)) for b in bufs])
            for g in range(N_DIL):
                outs["dil%d_s" % g] = jnp.transpose(new_bufs[g], (0, 4, 1, 2, 3))
            res_s = _proj_res([o_s], [wo2], res_s, name="dil_out_sample")

        res_p = _xattn_prompt(res_p, norm_cross, xwq, mem_kv_b, xwo, i, bsz, seq)
        (q_s,) = _norm_proj(res_s, norm_cross[i], [xwq[i]], name="xattn_q_sample")
        a_s = _xattn_sample(q_s, cache_mem_kv, i)
        res_s = _proj_res([a_s], [xwo[i]], res_s, name="xattn_o_sample")

        final_g = norm_final if i == depth - 1 else None
        res_p = _ffn(res_p, norm_ffn2, w2gu, w2d, i, final_g)
        res_s = _ffn(res_s, norm_ffn2, w2gu, w2d, i, final_g)

    return (res_p.reshape(bsz, seq, d), res_s.reshape(dec, 1, d),
            outs["mla_p"], outs["mla_s"], outs["conv_p"], outs["conv_s"], outs["ssmc_p"], outs["ssmc_s"],
            outs["ssm_p"], outs["ssm_s"],
            outs["dil0_p"], outs["dil0_s"], outs["dil1_p"], outs["dil1_s"], outs["dil2_p"], outs["dil2_s"],
            mem_kv.reshape(depth, bsz, mem_len, 2, MEM_HEADS, MEM_HD))
```

```python
import functools
import math

import jax
import jax.numpy as jnp
from jax import lax
from jax.experimental import pallas as pl
from jax.experimental.pallas import tpu as pltpu

F32 = jnp.float32
BF16 = jnp.bfloat16

D_MODEL = 1024
D_FF = 2816
RMS_EPS = 1e-6
LN_EPS = 1e-5
ROPE_THETA = 10000.0

MLA_HEADS = 16
MLA_Q_LORA = 256
MLA_KV_LORA = 128
MLA_NOPE = 64
MLA_ROPE = 32
MLA_V_HD = 64
MLA_LATENT = MLA_KV_LORA + MLA_ROPE
MLA_SCALE = (MLA_NOPE + MLA_ROPE) ** -0.5

CONV_WIDTH = 31
CONV_HALO = 32

D_INNER = 2 * D_MODEL
SSM_HD = 64
SSM_HEADS = D_INNER // SSM_HD
SSM_GROUPS = 4
SSM_REP = SSM_HEADS // SSM_GROUPS
SSM_STATE = 128
SSM_CONV = 4
SSM_CHUNK = 128
SSM_GN = SSM_GROUPS * SSM_STATE
SSM_CONV_DIM = D_INNER + 2 * SSM_GN
SSM_HALO = 8

DIL_GROUPS = ((128, 1), (512, 4), (2048, 16))
N_DIL = 3
DIL_HEADS = 5
DIL_HD = 64
DIL_GW = DIL_HEADS * DIL_HD
DIL_WIDTH = N_DIL * DIL_GW
DIL_NK = 128
DIL_OW = 384

MEM_HEADS = 4
MEM_HD = D_MODEL // MEM_HEADS

LANE = 128
NEG_INF = float("-inf")
LOG2E = 1.4426950408889634


def _cparams(sem, vmem_mb=48):
    return pltpu.CompilerParams(dimension_semantics=sem, vmem_limit_bytes=vmem_mb * 1024 * 1024)


def _const_spec(shape):
    nd = len(shape)
    return pl.BlockSpec(shape, lambda *_: (0,) * nd, pipeline_mode=pl.Buffered(1))


def _dot(a, b):
    return jnp.dot(a.astype(BF16), b.astype(BF16), preferred_element_type=F32)


def _dot_nt(a, b):
    return lax.dot_general(a.astype(BF16), b.astype(BF16), (((1,), (1,)), ((), ())),
                           preferred_element_type=F32)


def _dot_exact(a, b):
    return jnp.dot(a, b, preferred_element_type=F32, precision=lax.Precision.HIGHEST)


def _dot_nt_exact(a, b):
    return lax.dot_general(a, b, (((1,), (1,)), ((), ())), preferred_element_type=F32,
                           precision=lax.Precision.HIGHEST)


def _rms(x, g):
    return x * lax.rsqrt(jnp.mean(x * x, axis=-1, keepdims=True) + RMS_EPS) * g


def _silu(x):
    return x * jax.nn.sigmoid(x)


def _rot_half(t, dim):
    half = dim // 2
    lane = lax.broadcasted_iota(jnp.int32, t.shape, 1)
    return jnp.where((lane & (dim - 1)) < half, pltpu.roll(t, LANE - half, 1), pltpu.roll(t, half, 1))


def _rope_tables(pos, dim):
    half = dim // 2
    inv_freq = jnp.exp(-math.log(ROPE_THETA) * (2.0 / dim) * jnp.arange(half, dtype=F32))
    ang = pos.astype(F32)[:, None] * inv_freq[None, :]
    cos, sin = jnp.cos(ang), jnp.sin(ang)
    reps = LANE // dim
    return (jnp.tile(jnp.concatenate([cos, cos], axis=1), (1, reps)),
            jnp.tile(jnp.concatenate([-sin, sin], axis=1), (1, reps)))


def _softmax_rows(s):
    m = jnp.max(s, axis=-1, keepdims=True)
    p = jnp.exp(s - m)
    return p / jnp.sum(p, axis=-1, keepdims=True)


def _cast_kernel(x_ref, o_ref):
    o_ref[...] = x_ref[...].astype(BF16)


def _cast_bf16(w, rows=256):
    shape = w.shape
    w2 = w.reshape(-1, shape[-1])
    m, n = w2.shape
    rows = min(rows, m)
    out = pl.pallas_call(
        _cast_kernel, grid=(pl.cdiv(m, rows),),
        in_specs=[pl.BlockSpec((rows, n), lambda i: (i, 0))],
        out_specs=pl.BlockSpec((rows, n), lambda i: (i, 0)),
        out_shape=jax.ShapeDtypeStruct((m, n), BF16),
        compiler_params=_cparams(("parallel",)), name="cast_bf16")(w2)
    return out.reshape(shape)


def _layer_spec(shape, layer):
    nd = len(shape)
    return pl.BlockSpec((1,) + tuple(shape[1:]), lambda *_: (layer,) + (0,) * (nd - 1), pipeline_mode=pl.Buffered(1))


def _ffn_kernel(has_final, x_ref, g_ref, wgu_ref, wd_ref, *rest):
    o_ref = rest[-1]
    x = x_ref[...]
    h = _dot(_rms(x, g_ref[0]), wgu_ref[0])
    act = _silu(h[:, :D_FF]) * h[:, D_FF:]
    y = x + 0.5 * _dot(act, wd_ref[0])
    if has_final:
        y = _rms(y, rest[0][...])
    o_ref[...] = y


def _ffn(x, g, wgu, wd, layer, final_g=None, tm=512):
    m, d = x.shape
    tm = min(tm, m)
    g = g.reshape(g.shape[0], 1, d)
    args = [x, g, wgu, wd]
    in_specs = [pl.BlockSpec((tm, d), lambda i: (i, 0)), _layer_spec(g.shape, layer),
                _layer_spec(wgu.shape, layer), _layer_spec(wd.shape, layer)]
    if final_g is not None:
        args.append(final_g.reshape(1, d))
        in_specs.append(_const_spec((1, d)))
    return pl.pallas_call(
        functools.partial(_ffn_kernel, final_g is not None),
        grid=(m // tm,), in_specs=in_specs,
        out_specs=pl.BlockSpec((tm, d), lambda i: (i, 0)),
        out_shape=jax.ShapeDtypeStruct((m, d), F32),
        compiler_params=_cparams(("parallel",), 56), name="ffn")(*args)


def _norm_proj_kernel(n, has_bias, x_ref, g_ref, *refs):
    xn = _rms(x_ref[...], g_ref[...]).astype(BF16)
    for i in range(n):
        y = jnp.dot(xn, refs[i][...], preferred_element_type=F32)
        if has_bias:
            y = y + refs[n + i][...]
        refs[-n + i][...] = y


def _norm_proj(x, g, ws, bs=None, tm=256, name="norm_proj"):
    m, d = x.shape
    tm = min(tm, m)
    n = len(ws)
    args = [x, g.reshape(1, d)] + list(ws)
    in_specs = [pl.BlockSpec((tm, d), lambda i: (i, 0)), _const_spec((1, d))] + [_const_spec(w.shape) for w in ws]
    if bs is not None:
        args += [b.reshape(1, -1) for b in bs]
        in_specs += [_const_spec((1, b.shape[-1])) for b in bs]
    outs = pl.pallas_call(
        functools.partial(_norm_proj_kernel, n, bs is not None),
        grid=(m // tm,), in_specs=in_specs,
        out_specs=[pl.BlockSpec((tm, w.shape[1]), lambda i: (i, 0)) for w in ws],
        out_shape=[jax.ShapeDtypeStruct((m, w.shape[1]), F32) for w in ws],
        compiler_params=_cparams(("parallel",)), name=name)(*args)
    return outs


def _proj_res_kernel(n, has_bias, *refs):
    a_refs, w_refs = refs[:n], refs[n:2 * n]
    res_ref, o_ref = refs[2 * n], refs[-1]
    y = res_ref[...]
    for a_ref, w_ref in zip(a_refs, w_refs):
        y = y + _dot(a_ref[...], w_ref[...])
    if has_bias:
        y = y + refs[2 * n + 1][...]
    o_ref[...] = y


def _proj_res(a_list, w_list, res, bias=None, tm=256, name="proj_res"):
    m, d = res.shape
    tm = min(tm, m)
    n = len(a_list)
    args = list(a_list) + list(w_list) + [res]
    in_specs = ([pl.BlockSpec((tm, a.shape[1]), lambda i: (i, 0)) for a in a_list]
                + [_const_spec(w.shape) for w in w_list] + [pl.BlockSpec((tm, d), lambda i: (i, 0))])
    if bias is not None:
        args.append(bias.reshape(1, d))
        in_specs.append(_const_spec((1, d)))
    return pl.pallas_call(
        functools.partial(_proj_res_kernel, n, bias is not None),
        grid=(m // tm,), in_specs=in_specs,
        out_specs=pl.BlockSpec((tm, d), lambda i: (i, 0)),
        out_shape=jax.ShapeDtypeStruct((m, d), F32),
        compiler_params=_cparams(("parallel",)), name=name)(*args)


def _memkv_kernel(x_ref, g_ref, w_ref, o_ref):
    o_ref[0] = _dot(_rms(x_ref[...], g_ref[0]), w_ref[0])


def _memkv(mem2d, norm_mem, w_kv, tm=256):
    m, d = mem2d.shape
    depth = w_kv.shape[0]
    n = w_kv.shape[2]
    return pl.pallas_call(
        _memkv_kernel, grid=(depth, m // tm),
        in_specs=[pl.BlockSpec((tm, d), lambda l, i: (i, 0)),
                  pl.BlockSpec((1, 1, d), lambda l, i: (l, 0, 0)),
                  pl.BlockSpec((1, d, n), lambda l, i: (l, 0, 0))],
        out_specs=pl.BlockSpec((1, tm, n), lambda l, i: (l, i, 0)),
        out_shape=jax.ShapeDtypeStruct((depth, m, n), F32),
        compiler_params=_cparams(("parallel", "parallel")), name="memkv")(mem2d, norm_mem.reshape(depth, 1, d), w_kv)


def _xattn_core(q, kv):
    kvb = kv.astype(BF16)
    outs = []
    for h in range(MEM_HEADS):
        kh = kvb[:, h * MEM_HD:(h + 1) * MEM_HD]
        vh = kvb[:, D_MODEL + h * MEM_HD:D_MODEL + (h + 1) * MEM_HD]
        s = _dot_nt(q[:, h * MEM_HD:(h + 1) * MEM_HD], kh) * MEM_HD ** -0.5
        outs.append(_dot(_softmax_rows(s), vh))
    return jnp.concatenate(outs, axis=1)


def _xattn_prompt_kernel(x_ref, g_ref, wq_ref, kv_ref, wo_ref, o_ref):
    x = x_ref[...]
    q = _dot(_rms(x, g_ref[0]), wq_ref[0])
    o_ref[...] = x + _dot(_xattn_core(q, kv_ref[0]), wo_ref[0])


def _xattn_prompt(x, g, wq, kv, wo, layer, bsz, seq, tq=512):
    m, d = x.shape
    nq = seq // tq
    _, ml, kvw = kv.shape
    g = g.reshape(g.shape[0], 1, d)
    return pl.pallas_call(
        _xattn_prompt_kernel, grid=(bsz, nq),
        in_specs=[pl.BlockSpec((tq, d), lambda b, i: (b * nq + i, 0)), _layer_spec(g.shape, layer),
                  _layer_spec(wq.shape, layer),
                  pl.BlockSpec((1, ml, kvw), lambda b, i: (layer * bsz + b, 0, 0)), _layer_spec(wo.shape, layer)],
        out_specs=pl.BlockSpec((tq, d), lambda b, i: (b * nq + i, 0)),
        out_shape=jax.ShapeDtypeStruct((m, d), F32),
        compiler_params=_cparams(("parallel", "parallel")), name="xattn_prompt")(x, g, wq, kv, wo)


def _xattn_sample_kernel(q_ref, kv_ref, o_ref):
    q = q_ref[0]
    k = kv_ref[0, 0, :, 0]
    v = kv_ref[0, 0, :, 1]
    s = jnp.sum(k * q[None], axis=-1, keepdims=True) * MEM_HD ** -0.5
    p = jnp.exp(s - jnp.max(s, axis=0, keepdims=True))
    p = p / jnp.sum(p, axis=0, keepdims=True)
    o_ref[0] = jnp.sum(p * v, axis=0)


def _xattn_sample(q, cache, layer):
    bsz, d = q.shape
    out = pl.pallas_call(
        _xattn_sample_kernel, grid=(bsz,),
        in_specs=[pl.BlockSpec((1, MEM_HEADS, MEM_HD), lambda b: (b, 0, 0)),
                  pl.BlockSpec((1, 1) + cache.shape[2:], lambda b: (layer, b, 0, 0, 0, 0))],
        out_specs=pl.BlockSpec((1, MEM_HEADS, MEM_HD), lambda b: (b, 0, 0)),
        out_shape=jax.ShapeDtypeStruct((bsz, MEM_HEADS, MEM_HD), F32),
        compiler_params=_cparams(("parallel",)), name="xattn_sample")(q.reshape(bsz, MEM_HEADS, MEM_HD), cache)
    return out.reshape(bsz, d)


def _mla_proj_kernel(x_ref, g_ref, win_ref, nq_ref, wuq_ref, nkv_ref, wuk_ref, cos_ref, sin_ref, q_ref, lat_ref):
    cos, sin = cos_ref[...], sin_ref[...]
    a = _dot(_rms(x_ref[...], g_ref[...]), win_ref[...])
    ckv = _rms(a[:, MLA_Q_LORA:MLA_Q_LORA + MLA_KV_LORA], nkv_ref[...])
    pe = a[:, MLA_Q_LORA + MLA_KV_LORA:]
    kpe = pe * cos + _rot_half(pe, MLA_ROPE) * sin
    lat_ref[:, :MLA_KV_LORA] = ckv
    lat_ref[:, MLA_KV_LORA:] = kpe[:, :MLA_ROPE]
    q = _dot(_rms(a[:, :MLA_Q_LORA], nq_ref[...]), wuq_ref[...])
    n_nope = MLA_HEADS * MLA_NOPE
    for p in range(MLA_HEADS // 2):
        ql = _dot(q[:, p * LANE:(p + 1) * LANE], wuk_ref[p])
        q_ref[2 * p, :, :MLA_KV_LORA] = ql[:, :MLA_KV_LORA]
        q_ref[2 * p + 1, :, :MLA_KV_LORA] = ql[:, MLA_KV_LORA:]
    per_tile = LANE // MLA_ROPE
    for t in range(MLA_HEADS // per_tile):
        tile = q[:, n_nope + t * LANE:n_nope + (t + 1) * LANE]
        qpe = tile * cos + _rot_half(tile, MLA_ROPE) * sin
        for j in range(per_tile):
            q_ref[t * per_tile + j, :, MLA_KV_LORA:] = qpe[:, j * MLA_ROPE:(j + 1) * MLA_ROPE]


def _mla_proj(x, g, w, cos, sin, tm=256):
    m, d = x.shape
    tm = min(tm, m)
    nt = cos.shape[0] // tm
    consts = [g.reshape(1, d), w["win"], w["nq"], w["wuq"], w["nkv"], w["wuk"]]
    return pl.pallas_call(
        _mla_proj_kernel, grid=(m // tm,),
        in_specs=([pl.BlockSpec((tm, d), lambda i: (i, 0))] + [_const_spec(c.shape) for c in consts]
                  + [pl.BlockSpec((tm, LANE), lambda i: (i % nt, 0))] * 2),
        out_specs=[pl.BlockSpec((MLA_HEADS, tm, MLA_LATENT), lambda i: (0, i, 0)),
                   pl.BlockSpec((tm, MLA_LATENT), lambda i: (i, 0))],
        out_shape=[jax.ShapeDtypeStruct((MLA_HEADS, m, MLA_LATENT), F32),
                   jax.ShapeDtypeStruct((m, MLA_LATENT), F32)],
        compiler_params=_cparams(("parallel",)), name="mla_proj")(x, *consts, cos, sin)


def _mla_proj_prompt_kernel(tq, x_ref, g_ref, win_ref, nq_ref, wuq_ref, nkv_ref, wuk_ref, cos_ref, sin_ref,
                            qt_ref, lat_ref, ckvt_ref):
    cos, sin = cos_ref[...], sin_ref[...]
    tm = x_ref.shape[0]
    a = _dot(_rms(x_ref[...], g_ref[...]), win_ref[...])
    ckv = _rms(a[:, MLA_Q_LORA:MLA_Q_LORA + MLA_KV_LORA], nkv_ref[...])
    pe = a[:, MLA_Q_LORA + MLA_KV_LORA:]
    kpe = pe * cos + _rot_half(pe, MLA_ROPE) * sin
    lat_ref[:, :MLA_KV_LORA] = ckv
    lat_ref[:, MLA_KV_LORA:] = kpe[:, :MLA_ROPE]
    ckvt_ref[0] = ckv.T
    q = _dot(_rms(a[:, :MLA_Q_LORA], nq_ref[...]), wuq_ref[...])
    n_nope = MLA_HEADS * MLA_NOPE

    def put(h, lo, hi, val_t):
        for j in range(tm // tq):
            qt_ref[j, lo:hi, h * tq:(h + 1) * tq] = val_t[:, j * tq:(j + 1) * tq]

    for p in range(MLA_HEADS // 2):
        ql = _dot(q[:, p * LANE:(p + 1) * LANE], wuk_ref[p])
        put(2 * p, 0, MLA_KV_LORA, ql[:, :MLA_KV_LORA].T)
        put(2 * p + 1, 0, MLA_KV_LORA, ql[:, MLA_KV_LORA:].T)
    per_tile = LANE // MLA_ROPE
    for t in range(MLA_HEADS // per_tile):
        tile = q[:, n_nope + t * LANE:n_nope + (t + 1) * LANE]
        qpe_t = (tile * cos + _rot_half(tile, MLA_ROPE) * sin).T
        for j in range(per_tile):
            put(t * per_tile + j, MLA_KV_LORA, MLA_LATENT, qpe_t[j * MLA_ROPE:(j + 1) * MLA_ROPE])


def _mla_proj_prompt(x, g, w, cos, sin, seq, tq, tk, tm=256):
    m, d = x.shape
    nt = seq // tm
    per_chunk = tk // tm
    consts = [g.reshape(1, d), w["win"], w["nq"], w["wuq"], w["nkv"], w["wuk"]]
    return pl.pallas_call(
        functools.partial(_mla_proj_prompt_kernel, tq), grid=(m // tm,),
        in_specs=([pl.BlockSpec((tm, d), lambda i: (i, 0))] + [_const_spec(c.shape) for c in consts]
                  + [pl.BlockSpec((tm, LANE), lambda i: (i % nt, 0))] * 2),
        out_specs=[pl.BlockSpec((tm // tq, MLA_LATENT, MLA_HEADS * tq), lambda i: (i, 0, 0)),
                   pl.BlockSpec((tm, MLA_LATENT), lambda i: (i, 0)),
                   pl.BlockSpec((1, MLA_KV_LORA, tm), lambda i: (i // per_chunk, 0, i % per_chunk))],
        out_shape=[jax.ShapeDtypeStruct((m // tq, MLA_LATENT, MLA_HEADS * tq), F32),
                   jax.ShapeDtypeStruct((m, MLA_LATENT), F32),
                   jax.ShapeDtypeStruct((m // tk, MLA_KV_LORA, tk), F32)],
        compiler_params=_cparams(("parallel",)), name="mla_proj_prompt")(x, *consts, cos, sin)


def _mla_attn_kernel(tq, tk, qt_ref, lat_ref, ckvt_ref, wuv_ref, wo_ref, res_ref, o_ref):
    i = pl.program_id(1)
    cols = MLA_HEADS * tq
    qt = (qt_ref[0] * (MLA_SCALE * LOG2E)).astype(BF16)

    def chunk(c, carry, diagonal):
        m, l, acc = carry
        keys = lat_ref[pl.ds(pl.multiple_of(c * tk, tk), tk), :].astype(BF16)
        s = jnp.dot(keys, qt, preferred_element_type=F32)
        if diagonal:
            row = lax.broadcasted_iota(jnp.int32, (tk, cols), 0)
            col = lax.broadcasted_iota(jnp.int32, (tk, cols), 1)
            s = jnp.where(row - (col & (tq - 1)) <= i * tq - c * tk, s, NEG_INF)
        m_new = jnp.maximum(m, jnp.max(s, axis=0, keepdims=True))
        alpha = jnp.exp2(m - m_new)
        p = jnp.exp2(s - m_new)
        l = alpha * l + jnp.sum(p, axis=0, keepdims=True)
        acc = alpha * acc + _dot(ckvt_ref[c], p)
        return m_new, l, acc

    init = (jnp.full((1, cols), NEG_INF, F32), jnp.zeros((1, cols), F32), jnp.zeros((MLA_KV_LORA, cols), F32))
    n_full = (i * tq) // tk
    carry = lax.fori_loop(0, n_full, lambda c, cr: chunk(c, cr, False), init)
    _, l, acc = chunk(n_full, carry, True)
    o = (acc / l).T
    parts = []
    for p in range(MLA_HEADS // 2):
        pair = jnp.concatenate([o[(2 * p) * tq:(2 * p + 1) * tq], o[(2 * p + 1) * tq:(2 * p + 2) * tq]], axis=1)
        parts.append(_dot(pair, wuv_ref[p]))
    o_ref[...] = res_ref[...] + _dot(jnp.concatenate(parts, axis=1), wo_ref[...])


MLA_TQ, MLA_TK = 128, 512


def _mla_attn_prompt(q_t, lat, ckv_t, wuv, wo, res, bsz, seq, tq=MLA_TQ, tk=MLA_TK):
    m, d = res.shape
    nq = seq // tq
    nk = seq // tk
    assert tk % tq == 0
    return pl.pallas_call(
        functools.partial(_mla_attn_kernel, tq, tk), grid=(bsz, nq),
        in_specs=[pl.BlockSpec((1, MLA_LATENT, MLA_HEADS * tq), lambda b, i: (b * nq + i, 0, 0)),
                  pl.BlockSpec((seq, MLA_LATENT), lambda b, i: (b, 0)),
                  pl.BlockSpec((nk, MLA_KV_LORA, tk), lambda b, i: (b, 0, 0)),
                  _const_spec(wuv.shape), _const_spec(wo.shape),
                  pl.BlockSpec((tq, d), lambda b, i: (b * nq + i, 0))],
        out_specs=pl.BlockSpec((tq, d), lambda b, i: (b * nq + i, 0)),
        out_shape=jax.ShapeDtypeStruct((m, d), F32),
        compiler_params=_cparams(("parallel", "parallel")), name="mla_attn")(q_t, lat, ckv_t, wuv, wo, res)


def _mla_decode_kernel(pages_per_chunk, n_pages, page, pt_ref, q_ref, new_ref, cache_ref, o_ref, buf, sem):
    b = pl.program_id(0)
    n_chunks = n_pages // pages_per_chunk

    def copies(seq_idx, c, slot):
        return [pltpu.make_async_copy(cache_ref.at[pt_ref[seq_idx, c * pages_per_chunk + j]],
                                      buf.at[slot, :, pl.ds(j * page, page)], sem.at[slot])
                for j in range(pages_per_chunk)]

    @pl.when(b == 0)
    def _():
        for cp in copies(0, 0, 0):
            cp.start()

    q = (q_ref[0] * (MLA_SCALE * LOG2E)).astype(BF16)
    m = jnp.full((MLA_HEADS, 1), NEG_INF, F32)
    l = jnp.zeros((MLA_HEADS, 1), F32)
    acc = jnp.zeros((MLA_HEADS, MLA_KV_LORA), F32)
    for c in range(n_chunks):
        slot = c % 2
        if c + 1 < n_chunks:
            for cp in copies(b, c + 1, 1 - slot):
                cp.start()
        else:
            @pl.when(b + 1 < pl.num_programs(0))
            def _():
                for cp in copies(b + 1, 0, 1 - slot):
                    cp.start()
        for cp in copies(b, c, slot):
            cp.wait()
        keys_t = buf[slot].astype(BF16)
        s = _dot(q, keys_t)
        m_new = jnp.maximum(m, jnp.max(s, axis=-1, keepdims=True))
        alpha = jnp.exp2(m - m_new)
        p = jnp.exp2(s - m_new)
        l = alpha * l + jnp.sum(p, axis=-1, keepdims=True)
        acc = alpha * acc + _dot_nt(p, keys_t[:MLA_KV_LORA, :])
        m = m_new
    new = new_ref[0].astype(BF16).astype(F32)
    s_new = jnp.sum(q.astype(F32) * new, axis=-1, keepdims=True)
    m_new = jnp.maximum(m, s_new)
    alpha = jnp.exp2(m - m_new)
    p_new = jnp.exp2(s_new - m_new)
    l = alpha * l + p_new
    acc = alpha * acc + p_new.astype(BF16).astype(F32) * new[:, :MLA_KV_LORA]
    o_ref[0] = acc / l


def _mla_attn_sample(q, new_rows, cache_t, page_table, pages_per_chunk=32):
    bsz = q.shape[0]
    n_pages = page_table.shape[1]
    page = cache_t.shape[2]
    pages_per_chunk = min(pages_per_chunk, n_pages // 2)
    assert n_pages % (2 * pages_per_chunk) == 0
    grid_spec = pltpu.PrefetchScalarGridSpec(
        num_scalar_prefetch=1, grid=(bsz,),
        in_specs=[pl.BlockSpec((1, MLA_HEADS, MLA_LATENT), lambda b, pt: (b, 0, 0)),
                  pl.BlockSpec((1, 1, MLA_LATENT), lambda b, pt: (b, 0, 0)),
                  pl.BlockSpec(memory_space=pl.ANY)],
        out_specs=pl.BlockSpec((1, MLA_HEADS, MLA_KV_LORA), lambda b, pt: (b, 0, 0)),
        scratch_shapes=[pltpu.VMEM((2, MLA_LATENT, pages_per_chunk * page), F32), pltpu.SemaphoreType.DMA((2,))])
    return pl.pallas_call(
        functools.partial(_mla_decode_kernel, pages_per_chunk, n_pages, page), grid_spec=grid_spec,
        out_shape=jax.ShapeDtypeStruct((bsz, MLA_HEADS, MLA_KV_LORA), F32),
        compiler_params=_cparams(("arbitrary",)), name="mla_decode")(page_table, q, new_rows, cache_t)


def _mla_out_kernel(o_ref, wuv_ref, wo_ref, res_ref, out_ref):
    parts = []
    for p in range(MLA_HEADS // 2):
        pair = jnp.concatenate([o_ref[2 * p], o_ref[2 * p + 1]], axis=1)
        parts.append(_dot(pair, wuv_ref[p]))
    out_ref[...] = res_ref[...] + _dot(jnp.concatenate(parts, axis=1), wo_ref[...])


def _mla_out(o_lat, wuv, wo, res, tm=256):
    m, d = res.shape
    tm = min(tm, m)
    return pl.pallas_call(
        _mla_out_kernel, grid=(m // tm,),
        in_specs=[pl.BlockSpec((MLA_HEADS, tm, MLA_KV_LORA), lambda i: (0, i, 0)), _const_spec(wuv.shape),
                  _const_spec(wo.shape), pl.BlockSpec((tm, d), lambda i: (i, 0))],
        out_specs=pl.BlockSpec((tm, d), lambda i: (i, 0)),
        out_shape=jax.ShapeDtypeStruct((m, d), F32),
        compiler_params=_cparams(("parallel",)), name="mla_out")(o_lat, wuv, wo, res)


def _conv_tail(y, lng, lnb, wout, bout):
    mu = jnp.mean(y, axis=-1, keepdims=True)
    var = jnp.mean(jnp.square(y - mu), axis=-1, keepdims=True)
    yn = (y - mu) * lax.rsqrt(var + LN_EPS) * lng + lnb
    return _dot(_silu(yn), wout) + bout


def _conv_prompt_kernel(ts, x_ref, g_ref, win_ref, bin_ref, wdw_ref, bdw_ref, lng_ref, lnb_ref, wout_ref, bout_ref,
                        o_ref, st_ref, ext):
    j = pl.program_id(1)

    @pl.when(j == 0)
    def _():
        ext[0:CONV_HALO, :] = jnp.zeros((CONV_HALO, D_MODEL), F32)

    x = x_ref[...]
    a = _dot(_rms(x, g_ref[...]), win_ref[...]) + bin_ref[...]
    ext[CONV_HALO:CONV_HALO + ts, :] = a[:, :D_MODEL] * jax.nn.sigmoid(a[:, D_MODEL:])
    base = CONV_HALO - (CONV_WIDTH - 1)
    sub = 8
    y = bdw_ref[...]
    for phase in range(sub):
        taps = [r for r in range(phase, base + CONV_WIDTH, sub) if r >= base]
        span = ts + (sub if phase else 0)
        part = None
        for r in taps:
            term = wdw_ref[r - base:r - base + 1, :] * ext[r - phase:r - phase + span, :]
            part = term if part is None else part + term
        y = y + part[phase:phase + ts]
    o_ref[...] = x + _conv_tail(y, lng_ref[...], lnb_ref[...], wout_ref[...], bout_ref[...])

    @pl.when(j == pl.num_programs(1) - 1)
    def _():
        st_ref[0] = ext[CONV_HALO + ts - (CONV_WIDTH - 1):CONV_HALO + ts, :]

    ext[0:CONV_HALO, :] = ext[ts:ts + CONV_HALO, :]


def _conv_prompt(x, g, w, bsz, seq, ts=256):
    m, d = x.shape
    ns = seq // ts
    consts = [g.reshape(1, d), w["win"], w["bin"], w["wdw"], w["bdw"], w["lng"], w["lnb"], w["wout"], w["bout"]]
    return pl.pallas_call(
        functools.partial(_conv_prompt_kernel, ts), grid=(bsz, ns),
        in_specs=[pl.BlockSpec((ts, d), lambda b, j: (b * ns + j, 0))] + [_const_spec(c.shape) for c in consts],
        out_specs=[pl.BlockSpec((ts, d), lambda b, j: (b * ns + j, 0)),
                   pl.BlockSpec((1, CONV_WIDTH - 1, d), lambda b, j: (b, 0, 0))],
        out_shape=[jax.ShapeDtypeStruct((m, d), F32), jax.ShapeDtypeStruct((bsz, CONV_WIDTH - 1, d), F32)],
        scratch_shapes=[pltpu.VMEM((CONV_HALO + ts, d), F32)],
        compiler_params=_cparams(("parallel", "arbitrary")), name="conv_prompt")(x, *consts)


def _conv_sample_kernel(x_ref, g_ref, win_ref, bin_ref, st_ref, wdw_ref, bdw_ref, lng_ref, lnb_ref, wout_ref,
                        bout_ref, o_ref, u_ref):
    x = x_ref[...]
    a = _dot(_rms(x, g_ref[...]), win_ref[...]) + bin_ref[...]
    u = a[:, :D_MODEL] * jax.nn.sigmoid(a[:, D_MODEL:])
    u_ref[...] = u
    y = jnp.zeros_like(u)
    for k in range(CONV_WIDTH - 1):
        y = y + wdw_ref[k:k + 1, :] * st_ref[k]
    y = y + wdw_ref[CONV_WIDTH - 1:CONV_WIDTH, :] * u + bdw_ref[...]
    o_ref[...] = x + _conv_tail(y, lng_ref[...], lnb_ref[...], wout_ref[...], bout_ref[...])


def _conv_sample(x, g, w, state_t):
    m, d = x.shape
    args = [x, g.reshape(1, d), w["win"], w["bin"], state_t, w["wdw"], w["bdw"], w["lng"], w["lnb"], w["wout"],
            w["bout"]]
    return pl.pallas_call(
        _conv_sample_kernel, grid=(1,),
        in_specs=[_const_spec(a.shape) for a in args],
        out_specs=[_const_spec((m, d)), _const_spec((m, d))],
        out_shape=[jax.ShapeDtypeStruct((m, d), F32), jax.ShapeDtypeStruct((m, d), F32)],
        compiler_params=_cparams(("arbitrary",)), name="conv_sample")(*args)


def _ssm_in_prompt_kernel(ts, x_ref, g_ref, wz_ref, wxbc_ref, wdt_ref, cw_ref, cb_ref, dtb_ref,
                          z_ref, act_ref, dt_ref, cst_ref, ext):
    j = pl.program_id(1)

    @pl.when(j == 0)
    def _():
        ext[0:SSM_HALO, :] = jnp.zeros((SSM_HALO, SSM_CONV_DIM), F32)

    xn = _rms(x_ref[...], g_ref[...]).astype(BF16)
    z_ref[...] = jnp.dot(xn, wz_ref[...], preferred_element_type=F32)
    dt_ref[...] = jax.nn.softplus(jnp.dot(xn, wdt_ref[...], preferred_element_type=F32) + dtb_ref[...])
    ext[SSM_HALO:SSM_HALO + ts, :] = jnp.dot(xn, wxbc_ref[...], preferred_element_type=F32)
    base = SSM_HALO - (SSM_CONV - 1)
    y = jnp.zeros((ts, SSM_CONV_DIM), F32)
    for k in range(SSM_CONV):
        y = y + cw_ref[k:k + 1, :] * ext[base + k:base + k + ts, :]
    act_ref[...] = _silu(y + cb_ref[...])

    @pl.when(j == pl.num_programs(1) - 1)
    def _():
        cst_ref[0] = ext[SSM_HALO + ts - (SSM_CONV - 1):SSM_HALO + ts, :]

    ext[0:SSM_HALO, :] = ext[ts:ts + SSM_HALO, :]


def _ssm_in_prompt(x, g, w, bsz, seq, ts=256):
    m, d = x.shape
    ns = seq // ts
    consts = [g.reshape(1, d), w["wz"], w["wxbc"], w["wdt"], w["cw"], w["cb"], w["dtb"]]
    row = lambda b, j: (b * ns + j, 0)
    return pl.pallas_call(
        functools.partial(_ssm_in_prompt_kernel, ts), grid=(bsz, ns),
        in_specs=[pl.BlockSpec((ts, d), row)] + [_const_spec(c.shape) for c in consts],
        out_specs=[pl.BlockSpec((ts, D_INNER), row), pl.BlockSpec((ts, SSM_CONV_DIM), row),
                   pl.BlockSpec((ts, SSM_HEADS), row),
                   pl.BlockSpec((1, SSM_CONV - 1, SSM_CONV_DIM), lambda b, j: (b, 0, 0))],
        out_shape=[jax.ShapeDtypeStruct((m, D_INNER), F32), jax.ShapeDtypeStruct((m, SSM_CONV_DIM), F32),
                   jax.ShapeDtypeStruct((m, SSM_HEADS), F32),
                   jax.ShapeDtypeStruct((bsz, SSM_CONV - 1, SSM_CONV_DIM), F32)],
        scratch_shapes=[pltpu.VMEM((SSM_HALO + ts, SSM_CONV_DIM), F32)],
        compiler_params=_cparams(("parallel", "arbitrary")), name="ssm_in_prompt")(x, *consts)


def _ssd_scan_kernel(act_ref, dt_ref, alog_ref, d_ref, y_ref, h_ref):
    c = pl.program_id(1)
    L = SSM_CHUNK

    @pl.when(c == 0)
    def _():
        h_ref[...] = jnp.zeros(h_ref.shape, F32)

    dt = dt_ref[...]
    da = dt * (-jnp.exp(alog_ref[...]))
    ri = lax.broadcasted_iota(jnp.int32, (L, L), 0)
    ci = lax.broadcasted_iota(jnp.int32, (L, L), 1)
    tril = ri >= ci
    cs = _dot_exact(tril.astype(F32), da)
    eye = (lax.broadcasted_iota(jnp.int32, (SSM_HEADS, SSM_HEADS), 0)
           == lax.broadcasted_iota(jnp.int32, (SSM_HEADS, SSM_HEADS), 1)).astype(F32)
    cs_t = _dot_nt_exact(eye, cs)
    dt_t = _dot_nt_exact(eye, dt)
    w_t = dt_t * jnp.exp(cs_t[:, L - 1:L] - cs_t)
    cdec = jnp.exp(cs[L - 1:L, :])
    dvec = d_ref[...]
    for g in range(SSM_GROUPS):
        bg = act_ref[:, D_INNER + g * SSM_STATE:D_INNER + (g + 1) * SSM_STATE]
        cg = act_ref[:, D_INNER + SSM_GN + g * SSM_STATE:D_INNER + SSM_GN + (g + 1) * SSM_STATE]
        cb = _dot_nt(cg, bg)
        bg_t = bg.T
        for r8 in range(SSM_REP):
            r = g * SSM_REP + r8
            xr = act_ref[:, r * SSM_HD:(r + 1) * SSM_HD]
            h_t = h_ref[0, r]
            csb = jnp.broadcast_to(cs[:, r:r + 1], (L, L))
            lmat = jnp.exp(jnp.where(tril, csb - cs_t[r:r + 1, :], NEG_INF)) * (cb * dt_t[r:r + 1, :])
            lhs = jnp.concatenate([lmat, cg * jnp.exp(csb)], axis=1)
            rhs = jnp.concatenate([xr, h_t], axis=0)
            y_ref[:, r * SSM_HD:(r + 1) * SSM_HD] = _dot(lhs, rhs) + xr * dvec[:, r:r + 1]
            h_ref[0, r] = cdec[:, r:r + 1] * h_t + _dot(bg_t * w_t[r:r + 1, :], xr)


def _ssd_scan(act, dt, alog, dvec, bsz, seq):
    m = act.shape[0]
    nc = seq // SSM_CHUNK
    row = lambda b, c: (b * nc + c, 0)
    return pl.pallas_call(
        _ssd_scan_kernel, grid=(bsz, nc),
        in_specs=[pl.BlockSpec((SSM_CHUNK, SSM_CONV_DIM), row), pl.BlockSpec((SSM_CHUNK, SSM_HEADS), row),
                  _const_spec((1, SSM_HEADS)), _const_spec((1, SSM_HEADS))],
        out_specs=[pl.BlockSpec((SSM_CHUNK, D_INNER), row),
                   pl.BlockSpec((1, SSM_HEADS, SSM_STATE, SSM_HD), lambda b, c: (b, 0, 0, 0))],
        out_shape=[jax.ShapeDtypeStruct((m, D_INNER), F32),
                   jax.ShapeDtypeStruct((bsz, SSM_HEADS, SSM_STATE, SSM_HD), F32)],
        compiler_params=_cparams(("parallel", "arbitrary")), name="ssd_scan")(
            act, dt, alog.reshape(1, SSM_HEADS), dvec.reshape(1, SSM_HEADS))


def _ssm_out_kernel(y_ref, z_ref, n_ref, w_ref, res_ref, o_ref):
    yz = y_ref[...] * _silu(z_ref[...])
    gw = D_INNER // SSM_GROUPS
    parts = []
    for g in range(SSM_GROUPS):
        seg = yz[:, g * gw:(g + 1) * gw]
        parts.append(seg * lax.rsqrt(jnp.mean(seg * seg, axis=-1, keepdims=True) + RMS_EPS))
    o_ref[...] = res_ref[...] + _dot(jnp.concatenate(parts, axis=1) * n_ref[...], w_ref[...])


def _ssm_out(y, z, norm, wout, res, tm=256):
    m, d = res.shape
    tm = min(tm, m)
    row = lambda i: (i, 0)
    return pl.pallas_call(
        _ssm_out_kernel, grid=(m // tm,),
        in_specs=[pl.BlockSpec((tm, D_INNER), row), pl.BlockSpec((tm, D_INNER), row), _const_spec((1, D_INNER)),
                  _const_spec(wout.shape), pl.BlockSpec((tm, d), row)],
        out_specs=pl.BlockSpec((tm, d), row),
        out_shape=jax.ShapeDtypeStruct((m, d), F32),
        compiler_params=_cparams(("parallel",)), name="ssm_out")(y, z, norm.reshape(1, D_INNER), wout, res)


def _ssm_in_sample_kernel(x_ref, g_ref, wz_ref, wxbc_ref, wdt_ref, cw_ref, cb_ref, dtb_ref, st_ref,
                          z_ref, xbc_ref, act_ref, dt_ref):
    xn = _rms(x_ref[...], g_ref[...]).astype(BF16)
    z_ref[...] = jnp.dot(xn, wz_ref[...], preferred_element_type=F32)
    dt_ref[...] = jax.nn.softplus(jnp.dot(xn, wdt_ref[...], preferred_element_type=F32) + dtb_ref[...])
    xbc = jnp.dot(xn, wxbc_ref[...], preferred_element_type=F32)
    xbc_ref[...] = xbc
    y = jnp.zeros_like(xbc)
    for k in range(SSM_CONV - 1):
        y = y + cw_ref[k:k + 1, :] * st_ref[k]
    y = y + cw_ref[SSM_CONV - 1:SSM_CONV, :] * xbc
    act_ref[...] = _silu(y + cb_ref[...])


def _ssm_in_sample(x, g, w, state_t):
    m, d = x.shape
    args = [x, g.reshape(1, d), w["wz"], w["wxbc"], w["wdt"], w["cw"], w["cb"], w["dtb"], state_t]
    shapes = [(m, D_INNER), (m, SSM_CONV_DIM), (m, SSM_CONV_DIM), (m, SSM_HEADS)]
    return pl.pallas_call(
        _ssm_in_sample_kernel, grid=(1,),
        in_specs=[_const_spec(a.shape) for a in args],
        out_specs=[_const_spec(s) for s in shapes],
        out_shape=[jax.ShapeDtypeStruct(s, F32) for s in shapes],
        compiler_params=_cparams(("arbitrary",)), name="ssm_in_sample")(*args)


def _ssm_step_kernel(act_ref, dt_ref, alog_ref, d_ref, h_ref, y_ref, hn_ref):
    act = act_ref[0]
    dt = dt_ref[0]
    decay = jnp.exp(dt * (-jnp.exp(alog_ref[...])))
    dvec = d_ref[...]
    eye = (lax.broadcasted_iota(jnp.int32, (SSM_HD, SSM_HD), 0)
           == lax.broadcasted_iota(jnp.int32, (SSM_HD, SSM_HD), 1))
    for g in range(SSM_GROUPS):
        bg = act[:, D_INNER + g * SSM_STATE:D_INNER + (g + 1) * SSM_STATE]
        cg = jnp.broadcast_to(act[:, D_INNER + SSM_GN + g * SSM_STATE:D_INNER + SSM_GN + (g + 1) * SSM_STATE],
                              (8, SSM_STATE))
        for r8 in range(SSM_REP):
            r = g * SSM_REP + r8
            xr = act[:, r * SSM_HD:(r + 1) * SSM_HD]
            xdt = xr * dt[:, r:r + 1]
            xcol = jnp.sum(jnp.where(eye, jnp.broadcast_to(xdt, (SSM_HD, SSM_HD)), 0.0), axis=1, keepdims=True)
            h_new = decay[:, r:r + 1] * h_ref[0, r] + xcol * bg
            hn_ref[0, r] = h_new
            y = _dot_nt(cg, h_new)[0:1] + xr * dvec[:, r:r + 1]
            y_ref[0, :, r * SSM_HD:(r + 1) * SSM_HD] = y


def _ssm_step(act, dt, alog, dvec, state):
    bsz = act.shape[0]
    blk3 = lambda b: (b, 0, 0)
    y, hn = pl.pallas_call(
        _ssm_step_kernel, grid=(bsz,),
        in_specs=[pl.BlockSpec((1, 1, SSM_CONV_DIM), blk3), pl.BlockSpec((1, 1, SSM_HEADS), blk3),
                  _const_spec((1, SSM_HEADS)), _const_spec((1, SSM_HEADS)),
                  pl.BlockSpec((1, SSM_HEADS, SSM_HD, SSM_STATE), lambda b: (b, 0, 0, 0))],
        out_specs=[pl.BlockSpec((1, 1, D_INNER), blk3),
                   pl.BlockSpec((1, SSM_HEADS, SSM_HD, SSM_STATE), lambda b: (b, 0, 0, 0))],
        out_shape=[jax.ShapeDtypeStruct((bsz, 1, D_INNER), F32),
                   jax.ShapeDtypeStruct((bsz, SSM_HEADS, SSM_HD, SSM_STATE), F32)],
        compiler_params=_cparams(("parallel",)), name="ssm_step")(
            act.reshape(bsz, 1, SSM_CONV_DIM), dt.reshape(bsz, 1, SSM_HEADS), alog.reshape(1, SSM_HEADS),
            dvec.reshape(1, SSM_HEADS), state)
    return y.reshape(bsz, D_INNER), hn


def _dil_rope_qk(h, cos, sin):
    tiles = []
    for t in range(2 * DIL_WIDTH // LANE):
        tile = h[:, t * LANE:(t + 1) * LANE]
        tiles.append(tile * cos + _rot_half(tile, DIL_HD) * sin)
    return jnp.concatenate(tiles, axis=1)


def _dil_qkv_kernel(x_ref, g_ref, w_ref, cos_ref, sin_ref, q0_ref, q1_ref, q2_ref, kv0_ref, kv1_ref, kv2_ref):
    h = _dot(_rms(x_ref[...], g_ref[...]), w_ref[...])
    qk = _dil_rope_qk(h, cos_ref[...], sin_ref[...])
    for g, (q_ref, kv_ref) in enumerate(((q0_ref, kv0_ref), (q1_ref, kv1_ref), (q2_ref, kv2_ref))):
        q_ref[...] = qk[:, g * DIL_GW:(g + 1) * DIL_GW]
        kv_ref[:, :DIL_GW] = qk[:, DIL_WIDTH + g * DIL_GW:DIL_WIDTH + (g + 1) * DIL_GW]
        kv_ref[:, DIL_GW:] = h[:, 2 * DIL_WIDTH + g * DIL_GW:2 * DIL_WIDTH + (g + 1) * DIL_GW]


def _lane_chunks(width):
    return [(c, min(LANE, width - c)) for c in range(0, width, LANE)]


def _store_by_residue(val, dil, perm, out_ref):
    tm, w = val.shape
    if dil == 1:
        out_ref[0, 0] = val
        return
    for i, (c, wc) in enumerate(_lane_chunks(w)):
        perm[i, :, :wc] = val[:, c:c + wc]
    for r in range(dil):
        for i, (c, wc) in enumerate(_lane_chunks(w)):
            out_ref[0, r, :, c:c + wc] = perm[i, pl.ds(r, tm // dil, stride=dil), :][:, :wc]


def _load_by_position(in_ref, dil, perm):
    _, _, sub, w = in_ref.shape
    if dil == 1:
        return in_ref[0, 0]
    for r in range(dil):
        for i, (c, wc) in enumerate(_lane_chunks(w)):
            perm[i, pl.ds(r, sub, stride=dil), :] = in_ref[0, r, :, c:c + wc]
    return jnp.concatenate([perm[i] for i in range(len(_lane_chunks(w)))], axis=1)


def _dil_qkv_prompt_kernel(x_ref, g_ref, w_ref, cos_ref, sin_ref, q0_ref, q1_ref, q2_ref, kv0_ref, kv1_ref, kv2_ref,
                           pos1_ref, pos2_ref, perm):
    h = _dot(_rms(x_ref[...], g_ref[...]), w_ref[...])
    qk = _dil_rope_qk(h, cos_ref[...], sin_ref[...])
    pos_refs = (None, pos1_ref, pos2_ref)
    for g, (q_ref, kv_ref) in enumerate(((q0_ref, kv0_ref), (q1_ref, kv1_ref), (q2_ref, kv2_ref))):
        dil = DIL_GROUPS[g][1]
        kv = jnp.concatenate([qk[:, DIL_WIDTH + g * DIL_GW:DIL_WIDTH + (g + 1) * DIL_GW],
                              h[:, 2 * DIL_WIDTH + g * DIL_GW:2 * DIL_WIDTH + (g + 1) * DIL_GW]], axis=1)
        if pos_refs[g] is not None:
            pos_refs[g][...] = kv
        _store_by_residue(qk[:, g * DIL_GW:(g + 1) * DIL_GW], dil, perm, q_ref)
        _store_by_residue(kv, dil, perm, kv_ref)


def _dil_qkv_prompt(x, g, w, cos, sin, bsz, seq, tm=256):
    m, d = x.shape
    nj = seq // tm
    row = lambda i: (i, 0)
    res_blk = lambda i: (i // nj, 0, i % nj, 0)
    dils = [dl for _, dl in DIL_GROUPS]
    shapes = ([(bsz, dl, seq // dl, DIL_GW) for dl in dils] + [(bsz, dl, seq // dl, 2 * DIL_GW) for dl in dils])
    out_specs = ([pl.BlockSpec((1, dl, tm // dl, DIL_GW), res_blk) for dl in dils]
                 + [pl.BlockSpec((1, dl, tm // dl, 2 * DIL_GW), res_blk) for dl in dils]
                 + [pl.BlockSpec((tm, 2 * DIL_GW), row)] * 2)
    return pl.pallas_call(
        _dil_qkv_prompt_kernel, grid=(m // tm,),
        in_specs=[pl.BlockSpec((tm, d), row), _const_spec((1, d)), _const_spec(w.shape)]
        + [pl.BlockSpec((tm, LANE), lambda i: (i % nj, 0))] * 2,
        out_specs=out_specs,
        out_shape=[jax.ShapeDtypeStruct(sh, F32) for sh in shapes] + [jax.ShapeDtypeStruct((m, 2 * DIL_GW), F32)] * 2,
        scratch_shapes=[pltpu.VMEM((2 * DIL_GW // LANE, tm, LANE), F32)],
        compiler_params=_cparams(("parallel",)), name="dil_qkv_prompt")(x, g.reshape(1, d), w, cos, sin)


def _dil_qkv(x, g, w, cos, sin, tm=256):
    m, d = x.shape
    tm = min(tm, m)
    nt = cos.shape[0] // tm
    row = lambda i: (i, 0)
    widths = [DIL_GW] * N_DIL + [2 * DIL_GW] * N_DIL
    return pl.pallas_call(
        _dil_qkv_kernel, grid=(m // tm,),
        in_specs=[pl.BlockSpec((tm, d), row), _const_spec((1, d)), _const_spec(w.shape)]
        + [pl.BlockSpec((tm, LANE), lambda i: (i % nt, 0))] * 2,
        out_specs=[pl.BlockSpec((tm, wd), row) for wd in widths],
        out_shape=[jax.ShapeDtypeStruct((m, wd), F32) for wd in widths],
        compiler_params=_cparams(("parallel",)), name="dil_qkv")(x, g.reshape(1, d), w, cos, sin)


def _dil_band_kernel(n_sub, sub_len, q_ref, kv_ref, o_ref):
    nk = DIL_NK
    qi = lax.broadcasted_iota(jnp.int32, (nk, 2 * nk), 0)
    ki = lax.broadcasted_iota(jnp.int32, (nk, 2 * nk), 1)
    band = (ki <= qi + nk) & (ki >= qi)
    causal = lax.broadcasted_iota(jnp.int32, (nk, nk), 1) <= lax.broadcasted_iota(jnp.int32, (nk, nk), 0)
    for u in range(n_sub):
        for blk in range(sub_len // nk):
            rows = pl.ds(blk * nk, nk)
            q = q_ref[u, rows, :]
            if blk == 0:
                kvb, mask = kv_ref[u, rows, :].astype(BF16), causal
            else:
                kvb, mask = kv_ref[u, pl.ds((blk - 1) * nk, 2 * nk), :].astype(BF16), band
            heads = range(DIL_HEADS)
            s = jnp.concatenate([_dot_nt(q[:, h * DIL_HD:(h + 1) * DIL_HD], kvb[:, h * DIL_HD:(h + 1) * DIL_HD])
                                 for h in heads], axis=0) * DIL_HD ** -0.5
            s = jnp.where(jnp.concatenate([mask] * DIL_HEADS, axis=0), s, NEG_INF)
            mx = jnp.max(s, axis=-1, keepdims=True)
            e = jnp.exp(s - mx)
            l = jnp.sum(e, axis=-1, keepdims=True)
            p = e * (1.0 / l)
            lse = mx + jnp.log(l)
            o_ref[u, rows, DIL_GW:] = jnp.zeros((nk, DIL_OW - DIL_GW), F32)
            for h in heads:
                cols = slice(h * DIL_HD, (h + 1) * DIL_HD)
                o_ref[u, rows, cols] = _dot(p[h * nk:(h + 1) * nk], kvb[:, DIL_GW + h * DIL_HD:DIL_GW + (h + 1) * DIL_HD])
                o_ref[u, rows, DIL_GW + h:DIL_GW + h + 1] = lse[h * nk:(h + 1) * nk]


def _dil_prompt(q, kv, dil):
    bsz, _, sub_len, _ = q.shape
    blk = lambda b: (b, 0, 0)
    n_sub = max(1, min(dil, 16 * DIL_NK // sub_len))
    o = pl.pallas_call(
        functools.partial(_dil_band_kernel, n_sub, sub_len), grid=(bsz * dil // n_sub,),
        in_specs=[pl.BlockSpec((n_sub, sub_len, DIL_GW), blk), pl.BlockSpec((n_sub, sub_len, 2 * DIL_GW), blk)],
        out_specs=pl.BlockSpec((n_sub, sub_len, DIL_OW), blk),
        out_shape=jax.ShapeDtypeStruct((bsz * dil, sub_len, DIL_OW), F32),
        compiler_params=_cparams(("parallel",)), name="dil_band_d%d" % dil)(
            q.reshape(bsz * dil, sub_len, DIL_GW), kv.reshape(bsz * dil, sub_len, 2 * DIL_GW))
    return o.reshape(bsz, dil, sub_len, DIL_OW)


def _dil_out_kernel(o0_ref, o1_ref, o2_ref, w_ref, res_ref, out_ref, perm):
    o = [_load_by_position(r, dl, perm) for r, (_, dl) in zip((o0_ref, o1_ref, o2_ref), DIL_GROUPS)]
    tails = [x[:, DIL_OW - LANE:] for x in o]
    off = DIL_GW - (DIL_OW - LANE)
    top = jnp.maximum(jnp.maximum(tails[0], tails[1]), tails[2])
    es = [jnp.exp(t - top) for t in tails]
    den = es[0] + es[1] + es[2]
    y = res_ref[...]
    for g in range(N_DIL):
        share = es[g] / den
        part = jnp.concatenate([o[g][:, h * DIL_HD:(h + 1) * DIL_HD] * share[:, off + h:off + h + 1]
                                for h in range(DIL_HEADS)], axis=1)
        y = y + _dot(part, w_ref[g])
    out_ref[...] = y


def _dil_out(outs, w, res, seq, tm=256):
    m, d = res.shape
    nj = seq // tm
    row = lambda i: (i, 0)
    res_blk = lambda i: (i // nj, 0, i % nj, 0)
    return pl.pallas_call(
        _dil_out_kernel, grid=(m // tm,),
        in_specs=[pl.BlockSpec((1, dl, tm // dl, DIL_OW), res_blk) for _, dl in DIL_GROUPS]
        + [_const_spec(w.shape), pl.BlockSpec((tm, d), row)],
        out_specs=pl.BlockSpec((tm, d), row),
        out_shape=jax.ShapeDtypeStruct((m, d), F32),
        scratch_shapes=[pltpu.VMEM((DIL_OW // LANE, tm, LANE), F32)],
        compiler_params=_cparams(("parallel",)), name="dil_out")(*outs, w, res)


def _dil_sample_kernel(q0_ref, q1_ref, q2_ref, n0_ref, n1_ref, n2_ref, b0_ref, b1_ref, b2_ref,
                       o_ref, nb0_ref, nb1_ref, nb2_ref):
    eye = (lax.broadcasted_iota(jnp.int32, (DIL_HD, DIL_HD), 0) == lax.broadcasted_iota(jnp.int32, (DIL_HD, DIL_HD), 1))

    def to_col(row):
        return jnp.sum(jnp.where(eye, jnp.broadcast_to(row, (DIL_HD, DIL_HD)), 0.0), axis=1, keepdims=True)

    def to_row(col):
        return jnp.sum(jnp.where(eye, jnp.broadcast_to(col, (DIL_HD, DIL_HD)), 0.0), axis=0, keepdims=True)

    scale = DIL_HD ** -0.5
    heads = range(DIL_HEADS)
    outs, lses = [], []
    for (win, dil), q_ref, n_ref, b_ref, nb_ref in zip(DIL_GROUPS, (q0_ref, q1_ref, q2_ref), (n0_ref, n1_ref, n2_ref),
                                                       (b0_ref, b1_ref, b2_ref), (nb0_ref, nb1_ref, nb2_ref)):
        rows = b_ref.shape[-1]
        reads = (lax.broadcasted_iota(jnp.int32, (1, rows), 1) & (dil - 1)) == 0
        last = lax.broadcasted_iota(jnp.int32, (DIL_HD, rows), 1) == rows - 1
        q = q_ref[0]
        new = n_ref[0]
        qh = [q[:, h * DIL_HD:(h + 1) * DIL_HD] for h in heads]
        k_new = [new[:, h * DIL_HD:(h + 1) * DIL_HD] for h in heads]
        v_new = [new[:, DIL_GW + h * DIL_HD:DIL_GW + (h + 1) * DIL_HD] for h in heads]
        q_col = [to_col(x) for x in qh]
        s = jnp.concatenate([jnp.sum(b_ref[0, 0, h] * q_col[h], axis=0, keepdims=True) for h in heads], axis=0)
        s = jnp.where(reads, s * scale, NEG_INF)
        s_new = jnp.concatenate([jnp.sum(qh[h] * k_new[h], axis=-1, keepdims=True) for h in heads], axis=0) * scale
        mx = jnp.maximum(jnp.max(s, axis=-1, keepdims=True), s_new)
        lse = mx + jnp.log(jnp.sum(jnp.exp(s - mx), axis=-1, keepdims=True) + jnp.exp(s_new - mx))
        p = jnp.exp(s - lse)
        p_new = jnp.exp(s_new - lse)
        outs.append([to_row(jnp.sum(b_ref[0, 1, h] * p[h:h + 1], axis=1, keepdims=True)) + p_new[h:h + 1] * v_new[h]
                     for h in heads])
        lses.append(lse)
        for h in heads:
            nb_ref[0, 0, h] = jnp.where(last, to_col(k_new[h]), pltpu.roll(b_ref[0, 0, h], rows - 1, 1))
            nb_ref[0, 1, h] = jnp.where(last, to_col(v_new[h]), pltpu.roll(b_ref[0, 1, h], rows - 1, 1))
    top = jnp.maximum(jnp.maximum(lses[0], lses[1]), lses[2])
    es = [jnp.exp(x - top) for x in lses]
    den = es[0] + es[1] + es[2]
    o_ref[0] = jnp.concatenate([outs[g][h] * (es[g][h:h + 1] / den[h:h + 1]) for g in range(N_DIL) for h in heads],
                               axis=1)


def _dil_sample(qs, news, bufs_t):
    bsz = qs[0].shape[0]
    blk = lambda b: (b, 0, 0)
    blk5 = lambda b: (b, 0, 0, 0, 0)
    small = [q.reshape(bsz, 1, DIL_GW) for q in qs] + [n.reshape(bsz, 1, 2 * DIL_GW) for n in news]
    buf_specs = [pl.BlockSpec((1,) + b.shape[1:], blk5) for b in bufs_t]
    res = pl.pallas_call(
        _dil_sample_kernel, grid=(bsz,),
        in_specs=[pl.BlockSpec((1,) + a.shape[1:], blk) for a in small] + buf_specs,
        out_specs=[pl.BlockSpec((1, 1, DIL_WIDTH), blk)] + buf_specs,
        out_shape=[jax.ShapeDtypeStruct((bsz, 1, DIL_WIDTH), F32)]
        + [jax.ShapeDtypeStruct(b.shape, F32) for b in bufs_t],
        compiler_params=_cparams(("parallel",)), name="dil_sample")(*small, *bufs_t)
    return res[0].reshape(bsz, DIL_WIDTH), res[1:]


def _block_diag_pairs(w):
    h, k, n = w.shape
    z = jnp.zeros((h // 2, k, n), w.dtype)
    top = jnp.concatenate([w[0::2], z], axis=2)
    bot = jnp.concatenate([z, w[1::2]], axis=2)
    return jnp.concatenate([top, bot], axis=1)


def _prep_mla(mla_w_in, mla_norm_q, mla_w_uq, mla_norm_kv, mla_w_uk, mla_w_uv, mla_w_o):
    pad = jnp.zeros((D_MODEL, LANE - MLA_ROPE), F32)
    wuq = mla_w_uq.reshape(MLA_Q_LORA, MLA_HEADS, MLA_NOPE + MLA_ROPE)
    wuq = jnp.concatenate([wuq[:, :, :MLA_NOPE].reshape(MLA_Q_LORA, -1), wuq[:, :, MLA_NOPE:].reshape(MLA_Q_LORA, -1)],
                          axis=1)
    return {
        "win": jnp.concatenate([mla_w_in, pad], axis=1).astype(BF16),
        "nq": mla_norm_q.reshape(1, -1), "nkv": mla_norm_kv.reshape(1, -1),
        "wuq": wuq.astype(BF16),
        "wuk": _block_diag_pairs(jnp.transpose(mla_w_uk, (1, 2, 0))).astype(BF16),
        "wuv": _block_diag_pairs(jnp.transpose(mla_w_uv, (1, 0, 2))).astype(BF16),
        "wo": _cast_bf16(mla_w_o),
    }


def _prep_conv(conv_w_in, conv_b_in, conv_w_dw, conv_b_dw, conv_ln_g, conv_ln_b, conv_w_out, conv_b_out):
    return {"win": _cast_bf16(conv_w_in), "bin": conv_b_in.reshape(1, -1), "wdw": conv_w_dw,
            "bdw": conv_b_dw.reshape(1, -1), "lng": conv_ln_g.reshape(1, -1), "lnb": conv_ln_b.reshape(1, -1),
            "wout": _cast_bf16(conv_w_out), "bout": conv_b_out.reshape(1, -1)}


def _prep_ssm(ssm_w_in, ssm_conv_w, ssm_conv_b, ssm_dt_bias):
    w = ssm_w_in.astype(BF16)
    return {"wz": w[:, :D_INNER], "wxbc": w[:, D_INNER:D_INNER + SSM_CONV_DIM], "wdt": w[:, D_INNER + SSM_CONV_DIM:],
            "cw": ssm_conv_w, "cb": ssm_conv_b.reshape(1, -1), "dtb": ssm_dt_bias.reshape(1, -1)}


def kernel(x_prompt, x_sample, cache_mla_latent, cache_mem_kv, state_conv, state_ssm_conv, state_ssm, state_dil_kv0, state_dil_kv1, state_dil_kv2, page_table, mem_prompt, norm_ffn1, ffn1_w_gu, ffn1_w_down, norm_mix, norm_cross, norm_mem, cross_w_q, cross_w_kv, cross_w_o, norm_ffn2, ffn2_w_gu, ffn2_w_down, norm_final, mla_w_in, mla_norm_q, mla_w_uq, mla_norm_kv, mla_w_uk, mla_w_uv, mla_w_o, conv_w_in, conv_b_in, conv_w_dw, conv_b_dw, conv_ln_g, conv_ln_b, conv_w_out, conv_b_out, ssm_w_in, ssm_conv_w, ssm_conv_b, ssm_dt_bias, ssm_a_log, ssm_d, ssm_norm, ssm_w_out, dil_w_qkv, dil_w_o):
    bsz, seq, d = x_prompt.shape
    dec = x_sample.shape[0]
    assert x_sample.shape[1] == 1 and d == D_MODEL
    depth = norm_ffn1.shape[0]
    past_len = page_table.shape[1] * cache_mla_latent.shape[1]
    pos_p = jnp.arange(seq, dtype=jnp.int32)
    pos_s = jnp.full((dec,), past_len, dtype=jnp.int32)

    res_p = x_prompt.reshape(bsz * seq, d)
    res_s = x_sample.reshape(dec, d)
    mem_len = mem_prompt.shape[1]
    w1gu, w1d = _cast_bf16(ffn1_w_gu), _cast_bf16(ffn1_w_down)
    w2gu, w2d = _cast_bf16(ffn2_w_gu), _cast_bf16(ffn2_w_down)
    xwq, xwo = _cast_bf16(cross_w_q), _cast_bf16(cross_w_o)
    mem_kv = _memkv(mem_prompt.reshape(bsz * mem_len, d), norm_mem, _cast_bf16(cross_w_kv))
    mem_kv_b = mem_kv.reshape(depth * bsz, mem_len, 2 * d)
    outs = {}

    for i in range(depth):
        kind = i % 4
        res_p = _ffn(res_p, norm_ffn1, w1gu, w1d, i)
        res_s = _ffn(res_s, norm_ffn1, w1gu, w1d, i)
        g_mix = norm_mix[i]
        if kind == 0:
            w = _prep_mla(mla_w_in, mla_norm_q, mla_w_uq, mla_norm_kv, mla_w_uk, mla_w_uv, mla_w_o)
            cos_p, sin_p = _rope_tables(pos_p, MLA_ROPE)
            cos_s, sin_s = _rope_tables(pos_s, MLA_ROPE)
            q_p, lat_p, ckv_t = _mla_proj_prompt(res_p, g_mix, w, cos_p, sin_p, seq, MLA_TQ, MLA_TK)
            res_p = _mla_attn_prompt(q_p, lat_p, ckv_t, w["wuv"], w["wo"], res_p, bsz, seq)
            q_s, lat_s = _mla_proj(res_s, g_mix, w, cos_s, sin_s)
            o_s = _mla_attn_sample(jnp.transpose(q_s, (1, 0, 2)), lat_s.reshape(dec, 1, MLA_LATENT),
                                   jnp.transpose(cache_mla_latent, (0, 2, 1)), page_table)
            res_s = _mla_out(jnp.transpose(o_s, (1, 0, 2)), w["wuv"], w["wo"], res_s)
            outs["mla_p"] = lat_p.reshape(bsz, seq, MLA_LATENT)
            outs["mla_s"] = lat_s.reshape(dec, 1, MLA_LATENT)
        elif kind == 1:
            w = _prep_conv(conv_w_in, conv_b_in, conv_w_dw, conv_b_dw, conv_ln_g, conv_ln_b, conv_w_out, conv_b_out)
            res_p, outs["conv_p"] = _conv_prompt(res_p, g_mix, w, bsz, seq)
            res_s, u_s = _conv_sample(res_s, g_mix, w, jnp.transpose(state_conv, (1, 0, 2)))
            outs["conv_s"] = jnp.concatenate([state_conv[:, 1:], u_s[:, None, :]], axis=1)
        elif kind == 2:
            w = _prep_ssm(ssm_w_in, ssm_conv_w, ssm_conv_b, ssm_dt_bias)
            wout = _cast_bf16(ssm_w_out)
            z_p, act_p, dt_p, outs["ssmc_p"] = _ssm_in_prompt(res_p, g_mix, w, bsz, seq)
            y_p, h_t = _ssd_scan(act_p, dt_p, ssm_a_log, ssm_d, bsz, seq)
            outs["ssm_p"] = jnp.swapaxes(h_t, 2, 3)
            res_p = _ssm_out(y_p, z_p, ssm_norm, wout, res_p)
            z_s, xbc_s, act_s, dt_s = _ssm_in_sample(res_s, g_mix, w, jnp.transpose(state_ssm_conv, (1, 0, 2)))
            outs["ssmc_s"] = jnp.concatenate([state_ssm_conv[:, 1:], xbc_s[:, None, :]], axis=1)
            y_s, outs["ssm_s"] = _ssm_step(act_s, dt_s, ssm_a_log, ssm_d, state_ssm)
            res_s = _ssm_out(y_s, z_s, ssm_norm, wout, res_s)
        else:
            wqkv = _cast_bf16(dil_w_qkv)
            wo2 = _cast_bf16(dil_w_o)
            wo = wo2.reshape(N_DIL, DIL_GW, d)
            cos_p, sin_p = _rope_tables(pos_p, DIL_HD)
            cos_s, sin_s = _rope_tables(pos_s, DIL_HD)
            qkv_p = _dil_qkv_prompt(res_p, g_mix, wqkv, cos_p, sin_p, bsz, seq)
            qkv_s = _dil_qkv(res_s, g_mix, wqkv, cos_s, sin_s)
            bufs = (state_dil_kv0, state_dil_kv1, state_dil_kv2)
            kv_pos = (qkv_p[N_DIL], qkv_p[2 * N_DIL], qkv_p[2 * N_DIL + 1])
            o_list = []
            for g, (win, dil) in enumerate(DIL_GROUPS):
                o_list.append(_dil_prompt(qkv_p[g], qkv_p[N_DIL + g], dil))
                keep = min(win, seq)
                kv3 = kv_pos[g].reshape(bsz, seq, 2 * DIL_GW)
                outs["dil%d_p" % g] = kv3[:, seq - keep:].reshape(bsz, keep, 2, DIL_HEADS, DIL_HD)
                assert bufs[g].shape[1] == win
            res_p = _dil_out(o_list, wo, res_p, seq)
            o_s, new_bufs = _dil_sample(qkv_s[:N_DIL], qkv_s[N_DIL:], [jnp.transpose(b, (0, 2, 3, 4, 1)) for b in bufs])
            for g in range(N_DIL):
                outs["dil%d_s" % g] = jnp.transpose(new_bufs[g], (0, 4, 1, 2, 3))
            res_s = _proj_res([o_s], [wo2], res_s, name="dil_out_sample")

        res_p = _xattn_prompt(res_p, norm_cross, xwq, mem_kv_b, xwo, i, bsz, seq)
        (q_s,) = _norm_proj(res_s, norm_cross[i], [xwq[i]], name="xattn_q_sample")
        a_s = _xattn_sample(q_s, cache_mem_kv, i)
        res_s = _proj_res([a_s], [xwo[i]], res_s, name="xattn_o_sample")

        final_g = norm_final if i == depth - 1 else None
        res_p = _ffn(res_p, norm_ffn2, w2gu, w2d, i, final_g)
        res_s = _ffn(res_s, norm_ffn2, w2gu, w2d, i, final_g)

    return (res_p.reshape(bsz, seq, d), res_s.reshape(dec, 1, d),
            outs["mla_p"], outs["mla_s"], outs["conv_p"], outs["conv_s"], outs["ssmc_p"], outs["ssmc_s"],
            outs["ssm_p"], outs["ssm_s"],
            outs["dil0_p"], outs["dil0_s"], outs["dil1_p"], outs["dil1_s"], outs["dil2_p"], outs["dil2_s"],
            mem_kv.reshape(depth, bsz, mem_len, 2, MEM_HEADS, MEM_HD))
```
